```python
import math
import jax
import jax.numpy as jnp
from jax import lax
import numpy as np


D_MODEL = 2048
BATCH = 1
SEQ = 8192
DEPTH = 4

PLE_DIM = 256
RMS_EPS = 1e-6

MLA_HEADS = 8
MLA_Q_RANK = 512
MLA_KV_RANK = 256
MLA_NOPE = 128
MLA_ROPE = 64
MLA_V = 128
MLA_QBLOCK = 128
ROPE_THETA = 10000.0
GDN_HEADS = 8
GDN_DK = 128
GDN_DV = 128
GDN_CHUNK = 64
SSD_HEADS = 16
SSD_HEADDIM = 64
SSD_GROUPS = 2
SSD_STATE = 128
SSD_CHUNK = 128
SSD_INNER = SSD_HEADS * SSD_HEADDIM
SSD_CONV_DIM = SSD_INNER + 2 * SSD_GROUPS * SSD_STATE
RWKV_HEADS = 16
RWKV_HEAD = 64
RWKV_INNER = RWKV_HEADS * RWKV_HEAD
RWKV_CHUNK = 16
RWKV_W_LORA = 64
RWKV_A_LORA = 64
RWKV_G_LORA = 160
RWKV_PROJ = 3 * RWKV_INNER + RWKV_W_LORA + RWKV_A_LORA + RWKV_G_LORA
RWKV_GN_EPS = 64e-5
CONV_WIDTH = 4
PEER_HEADS = 8
PEER_NKEYS = 128
PEER_EXPERTS = PEER_NKEYS * PEER_NKEYS
PEER_QDIM = 256
PEER_TOPK = 16
PEER_TOKEN_BLOCK = 128

N_EVEN = (DEPTH + 1) // 2
N_ODD = DEPTH // 2

AB_COLS = (MLA_Q_RANK, MLA_KV_RANK, MLA_ROPE, GDN_HEADS * GDN_DK, GDN_HEADS * GDN_DK, GDN_HEADS * GDN_DV, GDN_HEADS * GDN_DV, GDN_HEADS, GDN_HEADS)
CD_COLS = (SSD_INNER, SSD_CONV_DIM, SSD_HEADS, RWKV_PROJ)
AB_WIDTH = sum(AB_COLS)
CD_WIDTH = sum(CD_COLS)
AB_OUT = MLA_HEADS * MLA_V + GDN_HEADS * GDN_DV
CD_OUT = SSD_INNER + RWKV_INNER

kernel_name = 'hybrid_mla_gdn_ssd_rwkv7_peer'


def _split(y, sizes):
    offsets, acc = [], 0
    for sz in sizes[:-1]:
        acc += sz
        offsets.append(acc)
    return jnp.split(y, offsets, axis=-1)


def rms_norm(x, gain, eps=RMS_EPS):
    xf = x.astype(jnp.float32)
    y = xf * lax.rsqrt(jnp.mean(xf * xf, axis=-1, keepdims=True) + eps)
    return (y * gain.astype(jnp.float32)).astype(x.dtype)


def l2_normalize(x, eps=1e-6):
    xf = x.astype(jnp.float32)
    return xf * lax.rsqrt(jnp.sum(xf * xf, axis=-1, keepdims=True) + eps)


def causal_depthwise_conv(x, w):
    width, ch = w.shape
    return lax.conv_general_dilated(x, w[:, None, :], window_strides=(1,), padding=[(width - 1, 0)], dimension_numbers=('NWC', 'WIO', 'NWC'), feature_group_count=ch)


def rope_tables(seq_len, dim):
    inv = 1.0 / (ROPE_THETA ** (jnp.arange(0, dim, 2, dtype=jnp.float32) / dim))
    ang = jnp.arange(seq_len, dtype=jnp.float32)[:, None] * inv[None, :]
    return jnp.cos(ang), jnp.sin(ang)


def apply_rope(x, cos, sin):
    x1, x2 = jnp.split(x, 2, axis=-1)
    c = cos.astype(x.dtype)
    s = sin.astype(x.dtype)
    return jnp.concatenate([x1 * c - x2 * s, x2 * c + x1 * s], axis=-1)


def _to_chunks(t, c):
    b, s = t.shape[0], t.shape[1]
    t = t.astype(jnp.float32).reshape((b, s // c, c) + t.shape[2:])
    return jnp.swapaxes(t, 2, 3)


def _from_chunks(o):
    n, b, h, c, v = o.shape
    return o.transpose(1, 0, 3, 2, 4).reshape(b, n * c, h, v)


def _causal_masks(c):
    idx = jnp.arange(c)
    return idx[:, None] >= idx[None, :], idx[:, None] > idx[None, :]


def _dots(a, b):
    return jnp.einsum('bnhtk,bnhjk->bnhtj', a, b)


def mla_attention(c_q, c_kv, k_rope, q_norm, w_uq, kv_norm, w_ukv, cos, sin):
    bsz, s, _ = c_q.shape
    q = (rms_norm(c_q, q_norm) @ w_uq).reshape(bsz, s, MLA_HEADS, MLA_NOPE + MLA_ROPE)
    q = jnp.concatenate([q[..., :MLA_NOPE], apply_rope(q[..., MLA_NOPE:], cos[:, None, :], sin[:, None, :])], axis=-1)
    kv = (rms_norm(c_kv, kv_norm) @ w_ukv).reshape(bsz, s, MLA_HEADS, MLA_NOPE + MLA_V)
    k_nope, v = kv[..., :MLA_NOPE], kv[..., MLA_NOPE:]
    k_r = apply_rope(k_rope, cos, sin)
    k = jnp.concatenate([k_nope, jnp.broadcast_to(k_r[:, :, None, :], (bsz, s, MLA_HEADS, MLA_ROPE))], axis=-1)
    scale = (MLA_NOPE + MLA_ROPE) ** -0.5
    nb = s // MLA_QBLOCK
    q_blocks = q.reshape(bsz, nb, MLA_QBLOCK, MLA_HEADS, MLA_NOPE + MLA_ROPE).transpose(1, 0, 2, 3, 4)
    k_pos = jnp.arange(s)

    def block(args):
        qb, bi = args
        sc = jnp.einsum('bqhd,bkhd->bhqk', qb, k).astype(jnp.float32) * scale
        q_pos = bi * MLA_QBLOCK + jnp.arange(MLA_QBLOCK)
        sc = jnp.where(k_pos[None, :] <= q_pos[:, None], sc, -jnp.inf)
        pr = jax.nn.softmax(sc, axis=-1).astype(v.dtype)
        return jnp.einsum('bhqk,bkhd->bqhd', pr, v)

    o = lax.map(block, (q_blocks, jnp.arange(nb)))
    return o.transpose(1, 0, 2, 3, 4).reshape(bsz, s, MLA_HEADS * MLA_V)


def chunk_gated_delta_rule(q, k, v, g, beta):
    bsz, s, h, dk = q.shape
    dv = v.shape[-1]
    c = GDN_CHUNK
    qc, kc, vc, gc, bc = (_to_chunks(t, c) for t in (q, k, v, g, beta))
    incl, strict = _causal_masks(c)
    G = jnp.cumsum(gc, axis=-1)
    diff = G[..., :, None] - G[..., None, :]
    decay = jnp.where(incl, jnp.exp(jnp.where(incl, diff, 0.0)), 0.0)
    L = jnp.where(strict, bc[..., :, None] * _dots(kc, kc) * decay, 0.0)
    rhs = jnp.concatenate([bc[..., None] * vc, (bc * jnp.exp(G))[..., None] * kc], axis=-1)
    sol = lax.linalg.triangular_solve(L + jnp.eye(c, dtype=jnp.float32), rhs, left_side=True, lower=True, unit_diagonal=True)
    w1, w2 = sol[..., :dv], sol[..., dv:]
    q_dec = qc * jnp.exp(G)[..., None]
    a_qk = _dots(qc, kc) * decay
    k_end = kc * jnp.exp(G[..., -1:] - G)[..., None]
    g_end = jnp.exp(G[..., -1])

    def step(state, inp):
        w1_c, w2_c, q_c, aqk_c, kend_c, gend_c = inp
        u = w1_c - jnp.einsum('bhtk,bhkv->bhtv', w2_c, state)
        o = jnp.einsum('bhtk,bhkv->bhtv', q_c, state) + jnp.einsum('bhtj,bhjv->bhtv', aqk_c, u)
        state = gend_c[..., None, None] * state + jnp.einsum('bhtk,bhtv->bhkv', kend_c, u)
        return state, o

    xs = tuple(jnp.moveaxis(t, 1, 0) for t in (w1, w2, q_dec, a_qk, k_end, g_end))
    _, o = lax.scan(step, jnp.zeros((bsz, h, dk, dv), jnp.float32), xs)
    return _from_chunks(o)


def gated_delta_net(qkv, gate, a_raw, b_raw, conv_w, a_log, dt_bias, norm_g):
    bsz, s, _ = qkv.shape
    qkv = jax.nn.silu(causal_depthwise_conv(qkv, conv_w))
    q, k, v = _split(qkv, (GDN_HEADS * GDN_DK, GDN_HEADS * GDN_DK, GDN_HEADS * GDN_DV))
    q = l2_normalize(q.reshape(bsz, s, GDN_HEADS, GDN_DK)) * (GDN_DK ** -0.5)
    k = l2_normalize(k.reshape(bsz, s, GDN_HEADS, GDN_DK))
    v = v.reshape(bsz, s, GDN_HEADS, GDN_DV)
    beta = jax.nn.sigmoid(b_raw.astype(jnp.float32))
    g = -jnp.exp(a_log.astype(jnp.float32)) * jax.nn.softplus(a_raw.astype(jnp.float32) + dt_bias.astype(jnp.float32))
    o = chunk_gated_delta_rule(q, k, v, g, beta)
    o = rms_norm(o, norm_g) * jax.nn.silu(gate.reshape(bsz, s, GDN_HEADS, GDN_DV).astype(jnp.float32))
    return o.reshape(bsz, s, GDN_HEADS * GDN_DV).astype(gate.dtype)


def mixer_ab(h, w_in, q_norm, w_uq, kv_norm, w_ukv, conv_w, a_log, dt_bias, gdn_norm, w_out, cos, sin):
    proj = h @ w_in
    c_q, c_kv, k_rope, gq, gk, gv, gz, ga, gb = _split(proj, AB_COLS)
    o_a = mla_attention(c_q, c_kv, k_rope, q_norm, w_uq, kv_norm, w_ukv, cos, sin)
    o_b = gated_delta_net(jnp.concatenate([gq, gk, gv], axis=-1), gz, ga, gb, conv_w, a_log, dt_bias, gdn_norm)
    return jnp.concatenate([o_a, o_b.astype(o_a.dtype)], axis=-1) @ w_out


def ssd_chunked(x, a, bm, cm):
    bsz, s, h, p = x.shape
    g, d_state = bm.shape[2], bm.shape[3]
    hpg = h // g
    c = SSD_CHUNK
    n = s // c
    x5 = x.reshape(bsz, n, c, g, hpg, p)
    a_cs = jnp.cumsum(a.reshape(bsz, n, c, g, hpg), axis=2)
    bm = bm.reshape(bsz, n, c, g, d_state)
    cm = cm.reshape(bsz, n, c, g, d_state)
    incl, _ = _causal_masks(c)
    a_t = jnp.moveaxis(a_cs, 2, -1)
    seg = a_t[..., :, None] - a_t[..., None, :]
    lmat = jnp.where(incl, jnp.exp(jnp.where(incl, seg, 0.0)), 0.0)
    cb = jnp.einsum('bnlgd,bnsgd->bngls', cm, bm)
    y_diag = jnp.einsum('bngkls,bnsgkp->bnlgkp', cb[:, :, :, None] * lmat, x5)
    xd = x5 * jnp.exp(a_cs[:, :, -1:] - a_cs)[..., None]
    states = jnp.einsum('bnlgd,bnlgkp->bngkpd', bm, xd)
    chunk_decay = jnp.exp(a_cs[:, :, -1])

    def step(state, inp):
        st, dec = inp
        return dec[..., None, None] * state + st, state

    _, prev = lax.scan(step, jnp.zeros((bsz, g, hpg, p, d_state), jnp.float32), (jnp.moveaxis(states, 1, 0), jnp.moveaxis(chunk_decay, 1, 0)))
    prev = jnp.moveaxis(prev, 0, 1)
    y_off = jnp.einsum('bnlgd,bngkpd->bnlgkp', cm, prev) * jnp.exp(a_cs)[..., None]
    return (y_diag + y_off).reshape(bsz, s, h, p)


def mamba2_ssd(z, xbc, dt_raw, conv_w, conv_b, dt_bias, a_log, d_skip, norm_g):
    bsz, s, _ = z.shape
    xbc = jax.nn.silu(causal_depthwise_conv(xbc, conv_w) + conv_b)
    xs, bm, cm = _split(xbc, (SSD_INNER, SSD_GROUPS * SSD_STATE, SSD_GROUPS * SSD_STATE))
    xs = xs.reshape(bsz, s, SSD_HEADS, SSD_HEADDIM).astype(jnp.float32)
    bm = bm.reshape(bsz, s, SSD_GROUPS, SSD_STATE).astype(jnp.float32)
    cm = cm.reshape(bsz, s, SSD_GROUPS, SSD_STATE).astype(jnp.float32)
    dt = jax.nn.softplus(dt_raw.astype(jnp.float32) + dt_bias.astype(jnp.float32))
    a = -jnp.exp(a_log.astype(jnp.float32))
    y = ssd_chunked(xs * dt[..., None], a * dt, bm, cm)
    y = y + xs * d_skip.astype(jnp.float32)[:, None]
    y = y.reshape(bsz, s, SSD_INNER) * jax.nn.silu(z.astype(jnp.float32))
    y = rms_norm(y.reshape(bsz, s, SSD_GROUPS, -1), norm_g.reshape(SSD_GROUPS, -1)).reshape(bsz, s, SSD_INNER)
    return y.astype(z.dtype)


def chunk_rwkv7(r, log_w, k, v, p, q):
    bsz, s, h, dk = r.shape
    dv = v.shape[-1]
    c = RWKV_CHUNK
    rc, wc, kc, vc, pc, qc = (_to_chunks(t, c) for t in (r, log_w, k, v, p, q))
    incl, strict = _causal_masks(c)
    lw = jnp.cumsum(wc, axis=3)
    lam_in = jnp.exp(lw)
    inv_lam = jnp.exp(-lw)
    q_bar = qc * jnp.exp(lw - wc)
    r_bar = rc * lam_in
    p_t = pc * inv_lam
    k_t = kc * inv_lam
    m_qp = jnp.where(strict, _dots(q_bar, p_t), 0.0)
    m_qk = jnp.where(strict, _dots(q_bar, k_t), 0.0)
    m_rp = jnp.where(incl, _dots(r_bar, p_t), 0.0)
    m_rk = jnp.where(incl, _dots(r_bar, k_t), 0.0)
    rhs = jnp.concatenate([q_bar, jnp.einsum('bnhtj,bnhjv->bnhtv', m_qk, vc)], axis=-1)
    sol = lax.linalg.triangular_solve(jnp.eye(c, dtype=jnp.float32) - m_qp, rhs, left_side=True, lower=True, unit_diagonal=True)
    w_s, w_v = sol[..., :dk], sol[..., dk:]
    y_loc = jnp.einsum('bnhtj,bnhjv->bnhtv', m_rk, vc)
    lam_end = lam_in[..., -1, :]
    p_end = p_t * lam_end[..., None, :]
    k_end = k_t * lam_end[..., None, :]

    def step(state, inp):
        ws, wv, rb, mrp, yl, pe, ke, vv, le = inp
        u = jnp.einsum('bhtk,bhkv->bhtv', ws, state) + wv
        y = jnp.einsum('bhtk,bhkv->bhtv', rb, state) + jnp.einsum('bhtj,bhjv->bhtv', mrp, u) + yl
        state = le[..., None] * state + jnp.einsum('bhtk,bhtv->bhkv', pe, u) + jnp.einsum('bhtk,bhtv->bhkv', ke, vv)
        return state, y

    xs = tuple(jnp.moveaxis(t, 1, 0) for t in (w_s, w_v, r_bar, m_rp, y_loc, p_end, k_end, vc, lam_end))
    _, y = lax.scan(step, jnp.zeros((bsz, h, dk, dv), jnp.float32), xs)
    return _from_chunks(y)


def rwkv7_time_mix(proj, mu, w0, w2, a0, a2, g2, k_k, k_a, r_k, ln_w, ln_b):
    bsz, s, _ = proj.shape
    shifted = jnp.pad(proj, ((0, 0), (1, 0), (0, 0)))[:, :-1]
    mixed = proj + (shifted - proj) * mu
    r, k, v, wl, al, gl = _split(mixed, (RWKV_INNER, RWKV_INNER, RWKV_INNER, RWKV_W_LORA, RWKV_A_LORA, RWKV_G_LORA))
    f32 = jnp.float32

    def heads(t):
        return t.reshape(bsz, s, RWKV_HEADS, RWKV_HEAD)

    log_w = -math.exp(-0.5) * jax.nn.sigmoid((w0 + jnp.tanh(wl) @ w2).astype(f32))
    a = jax.nn.sigmoid((a0 + al @ a2).astype(f32))
    gate = (jax.nn.sigmoid(gl) @ g2).astype(f32)
    kk = l2_normalize(heads(k * k_k))
    k_mod = k.astype(f32) * (1.0 + (a - 1.0) * k_a.astype(f32))
    r_h, k_h, v_h, a_h = heads(r.astype(f32)), heads(k_mod), heads(v.astype(f32)), heads(a)
    y = chunk_rwkv7(r_h, heads(log_w), k_h, v_h, -kk * a_h, kk)
    mean = jnp.mean(y, axis=-1, keepdims=True)
    var = jnp.mean(jnp.square(y - mean), axis=-1, keepdims=True)
    gn_w = ln_w.astype(f32).reshape(RWKV_HEADS, RWKV_HEAD)
    gn_b = ln_b.astype(f32).reshape(RWKV_HEADS, RWKV_HEAD)
    y = (y - mean) * lax.rsqrt(var + RWKV_GN_EPS) * gn_w + gn_b
    y = y + jnp.sum(r_h * k_h * r_k.astype(f32), axis=-1, keepdims=True) * v_h
    return (y.reshape(bsz, s, RWKV_INNER) * gate).astype(proj.dtype)


def mixer_cd(h, w_in, conv_w, conv_b, dt_bias, a_log, d_skip, ssd_norm, mu, w0, w2, a0, a2, g2, k_k, k_a, r_k, ln_w, ln_b, w_out):
    proj = h @ w_in
    z, xbc, dt_raw, rw = _split(proj, CD_COLS)
    o_c = mamba2_ssd(z, xbc, dt_raw, conv_w, conv_b, dt_bias, a_log, d_skip, ssd_norm)
    o_d = rwkv7_time_mix(rw, mu, w0, w2, a0, a2, g2, k_k, k_a, r_k, ln_w, ln_b)
    return jnp.concatenate([o_c, o_d], axis=-1) @ w_out


def peer(h, w_q, keys, u_tab, v_tab):
    bsz, s, d = h.shape
    q = (h @ w_q).reshape(bsz, s, PEER_HEADS, 2, PEER_QDIM // 2)
    sub = jnp.einsum('bshcd,hckd->bshck', q, keys).astype(jnp.float32)
    s1, i1 = lax.top_k(sub[..., 0, :], PEER_TOPK)
    s2, i2 = lax.top_k(sub[..., 1, :], PEER_TOPK)
    cand_s = (s1[..., :, None] + s2[..., None, :]).reshape(bsz, s, PEER_HEADS, PEER_TOPK * PEER_TOPK)
    cand_i = (i1[..., :, None] * PEER_NKEYS + i2[..., None, :]).reshape(bsz, s, PEER_HEADS, PEER_TOPK * PEER_TOPK)
    best_s, best_pos = lax.top_k(cand_s, PEER_TOPK)
    idx = jnp.take_along_axis(cand_i, best_pos, axis=-1)
    gates = jax.nn.softmax(best_s, axis=-1)
    nb = (bsz * s) // PEER_TOKEN_BLOCK
    hb = h.reshape(nb, PEER_TOKEN_BLOCK, d)
    ib = idx.reshape(nb, PEER_TOKEN_BLOCK, PEER_HEADS * PEER_TOPK)
    gb = gates.reshape(nb, PEER_TOKEN_BLOCK, PEER_HEADS * PEER_TOPK).astype(h.dtype)

    def block(args):
        x_b, i_b, g_b = args
        act = jax.nn.gelu(jnp.einsum('td,ted->te', x_b, u_tab[i_b]))
        return jnp.einsum('te,ted->td', g_b * act, v_tab[i_b])

    return lax.map(block, (hb, ib, gb)).reshape(bsz, s, d)


def per_layer_embedding(x, p_i, w_proj, norm_g, w_gate):
    gate = jax.nn.sigmoid((rms_norm(x, norm_g) @ w_gate).astype(jnp.float32)).astype(x.dtype)
    return gate * (p_i @ w_proj)


def setup_inputs(seed: int = 0) -> dict:
    key = jax.random.key(seed)
    ks = iter(jax.random.split(key, 64))

    def nrm(shape, scale):
        return jax.random.normal(next(ks), shape, jnp.float32) * scale

    def unif(shape, lo, hi):
        return jax.random.uniform(next(ks), shape, jnp.float32, lo, hi)

    def gain(shape):
        return 1.0 + nrm(shape, 0.02)

    def dt_bias(shape):
        dt = jnp.exp(unif(shape, math.log(1e-3), math.log(1e-1)))
        return dt + jnp.log(-jnp.expm1(-dt))

    return {
        'x': nrm((BATCH, SEQ, D_MODEL), 1.0),
        'p': nrm((DEPTH, BATCH, SEQ, PLE_DIM), 1.0),
        'ln_mix': gain((DEPTH, D_MODEL)),
        'ln_ffn': gain((DEPTH, D_MODEL)),
        'ab_w_in': nrm((N_EVEN, D_MODEL, AB_WIDTH), D_MODEL ** -0.5),
        'mla_q_norm': gain((N_EVEN, MLA_Q_RANK)),
        'mla_w_uq': nrm((N_EVEN, MLA_Q_RANK, MLA_HEADS * (MLA_NOPE + MLA_ROPE)), MLA_Q_RANK ** -0.5),
        'mla_kv_norm': gain((N_EVEN, MLA_KV_RANK)),
        'mla_w_ukv': nrm((N_EVEN, MLA_KV_RANK, MLA_HEADS * (MLA_NOPE + MLA_V)), MLA_KV_RANK ** -0.5),
        'gdn_conv_w': nrm((N_EVEN, CONV_WIDTH, GDN_HEADS * (2 * GDN_DK + GDN_DV)), 0.5),
        'gdn_a_log': jnp.log(unif((N_EVEN, GDN_HEADS), 1.0, 16.0)),
        'gdn_dt_bias': dt_bias((N_EVEN, GDN_HEADS)),
        'gdn_norm': gain((N_EVEN, GDN_DV)),
        'ab_w_out': nrm((N_EVEN, AB_OUT, D_MODEL), AB_OUT ** -0.5),
        'cd_w_in': nrm((N_ODD, D_MODEL, CD_WIDTH), D_MODEL ** -0.5),
        'ssd_conv_w': nrm((N_ODD, CONV_WIDTH, SSD_CONV_DIM), 0.5),
        'ssd_conv_b': nrm((N_ODD, SSD_CONV_DIM), 0.02),
        'ssd_dt_bias': dt_bias((N_ODD, SSD_HEADS)),
        'ssd_a_log': jnp.log(unif((N_ODD, SSD_HEADS), 1.0, 16.0)),
        'ssd_d': gain((N_ODD, SSD_HEADS)),
        'ssd_norm': gain((N_ODD, SSD_INNER)),
        'rwkv_mu': unif((N_ODD, RWKV_PROJ), 0.0, 1.0),
        'rwkv_w0': nrm((N_ODD, RWKV_INNER), 0.5),
        'rwkv_w2': nrm((N_ODD, RWKV_W_LORA, RWKV_INNER), 0.1),
        'rwkv_a0': nrm((N_ODD, RWKV_INNER), 0.1),
        'rwkv_a2': nrm((N_ODD, RWKV_A_LORA, RWKV_INNER), 0.5 * RWKV_A_LORA ** -0.5),
        'rwkv_g2': nrm((N_ODD, RWKV_G_LORA, RWKV_INNER), RWKV_G_LORA ** -0.5),
        'rwkv_k_k': 0.85 + nrm((N_ODD, RWKV_INNER), 0.02),
        'rwkv_k_a': gain((N_ODD, RWKV_INNER)),
        'rwkv_r_k': nrm((N_ODD, RWKV_HEADS, RWKV_HEAD), 0.1),
        'rwkv_ln_w': gain((N_ODD, RWKV_INNER)),
        'rwkv_ln_b': nrm((N_ODD, RWKV_INNER), 0.02),
        'cd_w_out': nrm((N_ODD, CD_OUT, D_MODEL), CD_OUT ** -0.5),
        'peer_w_q': nrm((DEPTH, D_MODEL, PEER_HEADS * PEER_QDIM), D_MODEL ** -0.5),
        'peer_keys': nrm((DEPTH, PEER_HEADS, 2, PEER_NKEYS, PEER_QDIM // 2), (PEER_QDIM // 2) ** -0.5),
        'peer_u': nrm((DEPTH, PEER_EXPERTS, D_MODEL), D_MODEL ** -0.5),
        'peer_v': nrm((DEPTH, PEER_EXPERTS, D_MODEL), 0.5 * PEER_HEADS ** -0.5),
        'ple_w_proj': nrm((DEPTH, PLE_DIM, D_MODEL), PLE_DIM ** -0.5),
        'ple_norm': gain((DEPTH, D_MODEL)),
        'ple_w_gate': nrm((DEPTH, D_MODEL, D_MODEL), D_MODEL ** -0.5),
        'final_norm': gain((D_MODEL,)),
    }


def reference(x, p, ln_mix, ln_ffn, ab_w_in, mla_q_norm, mla_w_uq, mla_kv_norm, mla_w_ukv, gdn_conv_w, gdn_a_log, gdn_dt_bias, gdn_norm, ab_w_out, cd_w_in, ssd_conv_w, ssd_conv_b, ssd_dt_bias, ssd_a_log, ssd_d, ssd_norm, rwkv_mu, rwkv_w0, rwkv_w2, rwkv_a0, rwkv_a2, rwkv_g2, rwkv_k_k, rwkv_k_a, rwkv_r_k, rwkv_ln_w, rwkv_ln_b, cd_w_out, peer_w_q, peer_keys, peer_u, peer_v, ple_w_proj, ple_norm, ple_w_gate, final_norm):
    cos, sin = rope_tables(x.shape[1], MLA_ROPE)
    for i in range(DEPTH):
        j = i // 2
        h = rms_norm(x, ln_mix[i])
        if i % 2 == 0:
            mix = mixer_ab(h, ab_w_in[j], mla_q_norm[j], mla_w_uq[j], mla_kv_norm[j], mla_w_ukv[j], gdn_conv_w[j], gdn_a_log[j], gdn_dt_bias[j], gdn_norm[j], ab_w_out[j], cos, sin)
        else:
            mix = mixer_cd(h, cd_w_in[j], ssd_conv_w[j], ssd_conv_b[j], ssd_dt_bias[j], ssd_a_log[j], ssd_d[j], ssd_norm[j], rwkv_mu[j], rwkv_w0[j], rwkv_w2[j], rwkv_a0[j], rwkv_a2[j], rwkv_g2[j], rwkv_k_k[j], rwkv_k_a[j], rwkv_r_k[j], rwkv_ln_w[j], rwkv_ln_b[j], cd_w_out[j])
        x = x + mix
        x = x + peer(rms_norm(x, ln_ffn[i]), peer_w_q[i], peer_keys[i], peer_u[i], peer_v[i])
        x = x + per_layer_embedding(x, p[i], ple_w_proj[i], ple_norm[i], ple_w_gate[i])
    return rms_norm(x, final_norm)
```

```python
import functools
import math

import jax
import jax.numpy as jnp
from jax import lax
from jax.experimental import pallas as pl
from jax.experimental.pallas import tpu as pltpu

F32 = jnp.float32
BF16 = jnp.bfloat16
HIGHEST = lax.Precision.HIGHEST

D_MODEL = 2048
DEPTH = 4
PLE_DIM = 256
RMS_EPS = 1e-6
MLA_HEADS = 8
MLA_Q_RANK = 512
MLA_KV_RANK = 256
MLA_NOPE = 128
MLA_ROPE = 64
MLA_V = 128
ROPE_THETA = 10000.0
GDN_HEADS = 8
GDN_DK = 128
GDN_DV = 128
GDN_CHUNK = 64
SSD_HEADS = 16
SSD_HEADDIM = 64
SSD_GROUPS = 2
SSD_STATE = 128
SSD_CHUNK = 128
SSD_INNER = SSD_HEADS * SSD_HEADDIM
RWKV_HEADS = 16
RWKV_HEAD = 64
RWKV_INNER = RWKV_HEADS * RWKV_HEAD
RWKV_W_LORA = 64
RWKV_A_LORA = 64
RWKV_G_LORA = 160
RWKV_GN_EPS = 64e-5
CONV_WIDTH = 4
PEER_HEADS = 8
PEER_NKEYS = 128
PEER_EXPERTS = PEER_NKEYS * PEER_NKEYS
PEER_QDIM = 256
PEER_TOPK = 16

LANES = 128
SUBLANES = 8
VMEM_LIMIT = 56 * 1024 * 1024


def _params(*sem):
    return pltpu.CompilerParams(dimension_semantics=sem, vmem_limit_bytes=VMEM_LIMIT)


def _bdot(a, b):
    return jnp.dot(a.astype(BF16), b.astype(BF16), preferred_element_type=F32)


def _bdot_nt(a, b):
    return lax.dot_general(a.astype(BF16), b.astype(BF16), (((1,), (1,)), ((), ())), preferred_element_type=F32)


def _hdot(a, b):
    return jnp.dot(a, b, preferred_element_type=F32, precision=HIGHEST)


def _hdot_nt(a, b):
    return lax.dot_general(a, b, (((1,), (1,)), ((), ())), preferred_element_type=F32, precision=HIGHEST)


def _rms(x, gain):
    return x * lax.rsqrt(jnp.mean(x * x, axis=-1, keepdims=True) + RMS_EPS) * gain


def _sigmoid(x):
    return 1.0 / (1.0 + jnp.exp(-x))


def _silu(x):
    return x * _sigmoid(x)


def _softplus(x):
    return jnp.maximum(x, 0.0) + jnp.log(1.0 + jnp.exp(-jnp.abs(x)))


def _nm_kernel(*refs, has_norm, has_res):
    it = iter(refs)
    x_ref = next(it)
    g_ref = next(it) if has_norm else None
    w_ref = next(it)
    r_ref = next(it) if has_res else None
    o_ref = next(it)
    xn_ref = next(it)

    @pl.when(pl.program_id(1) == 0)
    def _():
        x = x_ref[...].astype(F32)
        if has_norm:
            x = _rms(x, g_ref[...])
        xn_ref[...] = x.astype(BF16)

    acc = jnp.dot(xn_ref[...], w_ref[...], preferred_element_type=F32)
    if has_res:
        acc = acc + r_ref[...]
    o_ref[...] = acc


def norm_matmul(x, w, gain=None, residual=None, tm=512, tn=512):
    m, k = x.shape
    n = w.shape[1]
    tn = min(tn, n)
    assert m % tm == 0 and n % tn == 0
    in_specs = [pl.BlockSpec((tm, k), lambda i, j: (i, 0))]
    args = [x]
    if gain is not None:
        in_specs.append(pl.BlockSpec((1, k), lambda i, j: (0, 0)))
        args.append(gain.reshape(1, k).astype(F32))
    in_specs.append(pl.BlockSpec((k, tn), lambda i, j: (0, j)))
    args.append(w)
    if residual is not None:
        in_specs.append(pl.BlockSpec((tm, tn), lambda i, j: (i, j)))
        args.append(residual)
    return pl.pallas_call(
        functools.partial(_nm_kernel, has_norm=gain is not None, has_res=residual is not None),
        grid=(m // tm, n // tn),
        in_specs=in_specs,
        out_specs=pl.BlockSpec((tm, tn), lambda i, j: (i, j)),
        out_shape=jax.ShapeDtypeStruct((m, n), F32),
        scratch_shapes=[pltpu.VMEM((tm, k), BF16)],
        compiler_params=_params("parallel", "arbitrary"),
        name="norm_matmul",
    )(*args)


def _ple_kernel(x_ref, g_ref, wg_ref, p_ref, wp_ref, xc_ref, o_ref, xn_ref):
    @pl.when(pl.program_id(1) == 0)
    def _():
        xn_ref[...] = _rms(x_ref[...], g_ref[...]).astype(BF16)

    gate = _sigmoid(jnp.dot(xn_ref[...], wg_ref[...], preferred_element_type=F32))
    emb = _bdot(p_ref[...], wp_ref[...])
    o_ref[...] = xc_ref[...] + gate * emb


def ple_update(x, p_i, norm_g, w_gate, w_proj, tm=512, tn=512):
    m, d = x.shape
    pd = p_i.shape[1]
    return pl.pallas_call(
        _ple_kernel,
        grid=(m // tm, d // tn),
        in_specs=[
            pl.BlockSpec((tm, d), lambda i, j: (i, 0)),
            pl.BlockSpec((1, d), lambda i, j: (0, 0)),
            pl.BlockSpec((d, tn), lambda i, j: (0, j)),
            pl.BlockSpec((tm, pd), lambda i, j: (i, 0)),
            pl.BlockSpec((pd, tn), lambda i, j: (0, j)),
            pl.BlockSpec((tm, tn), lambda i, j: (i, j)),
        ],
        out_specs=pl.BlockSpec((tm, tn), lambda i, j: (i, j)),
        out_shape=jax.ShapeDtypeStruct((m, d), F32),
        scratch_shapes=[pltpu.VMEM((tm, d), BF16)],
        compiler_params=_params("parallel", "arbitrary"),
        name="ple_update",
    )(x, norm_g.reshape(1, d), w_gate, p_i, w_proj, x)


def _rmsnorm_kernel(x_ref, g_ref, o_ref):
    o_ref[...] = _rms(x_ref[...], g_ref[...])


def rmsnorm(x, gain, tm=512):
    m, d = x.shape
    return pl.pallas_call(
        _rmsnorm_kernel,
        grid=(m // tm,),
        in_specs=[pl.BlockSpec((tm, d), lambda i: (i, 0)), pl.BlockSpec((1, d), lambda i: (0, 0))],
        out_specs=pl.BlockSpec((tm, d), lambda i: (i, 0)),
        out_shape=jax.ShapeDtypeStruct((m, d), F32),
        compiler_params=_params("parallel"),
        name="rmsnorm",
    )(x, gain.reshape(1, d))


def _peer_fold_kernel(keys_ref, wq_ref, o_ref):
    o_ref[0, 0] = _hdot_nt(keys_ref[0, 0], wq_ref[...])


def peer_fold(w_q, keys):
    hk = PEER_QDIM // 2
    return pl.pallas_call(
        _peer_fold_kernel,
        grid=(2, PEER_HEADS),
        in_specs=[
            pl.BlockSpec((1, 1, PEER_NKEYS, hk), lambda c, h: (h, c, 0, 0)),
            pl.BlockSpec((D_MODEL, hk), lambda c, h: (0, h * 2 + c)),
        ],
        out_specs=pl.BlockSpec((1, 1, PEER_NKEYS, D_MODEL), lambda c, h: (c, h, 0, 0)),
        out_shape=jax.ShapeDtypeStruct((2, PEER_HEADS, PEER_NKEYS, D_MODEL), F32),
        compiler_params=_params("parallel", "parallel"),
        name="peer_fold",
    )(keys, w_q)


def _rmsnorm_t_kernel(x_ref, g_ref, o_ref):
    o_ref[...] = _rms(x_ref[...], g_ref[...]).T.astype(BF16)


def rmsnorm_t(x, gain, tm=512):
    m, d = x.shape
    return pl.pallas_call(
        _rmsnorm_t_kernel,
        grid=(m // tm,),
        in_specs=[pl.BlockSpec((tm, d), lambda i: (i, 0)), pl.BlockSpec((1, d), lambda i: (0, 0))],
        out_specs=pl.BlockSpec((d, tm), lambda i: (0, i)),
        out_shape=jax.ShapeDtypeStruct((d, m), BF16),
        compiler_params=_params("parallel"),
        name="rmsnorm_t",
    )(x, gain.reshape(1, d))


def _sort_desc(v):
    v = list(v)
    n = len(v)
    k = 2
    while k <= n:
        j = k // 2
        while j >= 1:
            for i in range(n):
                l = i ^ j
                if l > i:
                    hi, lo = jnp.maximum(v[i], v[l]), jnp.minimum(v[i], v[l])
                    v[i], v[l] = (hi, lo) if (i & k) == 0 else (lo, hi)
            j //= 2
        k *= 2
    return v


def _merge_top(a, b):
    n = len(a)
    v = [jnp.maximum(a[i], b[n - 1 - i]) for i in range(n)]
    j = n // 2
    while j >= 1:
        for i in range(n):
            l = i ^ j
            if l > i:
                v[i], v[l] = jnp.maximum(v[i], v[l]), jnp.minimum(v[i], v[l])
        j //= 2
    return v


def _top_sorted(vals, n):
    vals = list(vals)
    while len(vals) % n:
        vals.append(jnp.full_like(vals[0], -jnp.inf))
    acc = _sort_desc(vals[:n])
    for g in range(1, len(vals) // n):
        acc = _merge_top(acc, _sort_desc(vals[g * n:(g + 1) * n]))
    return acc


_PEER_PAIRS = [(i, j) for i in range(PEER_TOPK) for j in range(PEER_TOPK) if (i + 1) * (j + 1) <= PEER_TOPK]


def _peer_select_kernel(wf_ref, ht_ref, e1_ref, s1_ref, s2_ref, e2_ref, thr_ref, sub_ref):
    nk, k = PEER_NKEYS, PEER_TOPK
    tt = ht_ref.shape[1]
    ht = ht_ref[...]
    for c in range(2):
        sub_ref[c] = jnp.dot(wf_ref[c], ht, preferred_element_type=F32)
    row = lax.broadcasted_iota(jnp.int32, (SUBLANES, LANES), 0)

    def head_row(v, h):
        return jnp.sum(jnp.where(row == h, v, 0.0), axis=0, keepdims=True)

    def lane_group(lg, carry):
        lanes = pl.ds(pl.multiple_of(lg * LANES, LANES), LANES)

        def top_of_head(h, packed, c):
            base = pl.multiple_of(h * nk, nk)
            slabs = [sub_ref[c, pl.ds(base + SUBLANES * j, SUBLANES), lanes] for j in range(nk // SUBLANES)]
            top = _sort_desc(slabs)
            for sh in (4, 2, 1):
                top = _merge_top(top, [pltpu.roll(t, sh, 0) for t in top])
            return tuple(jnp.where(row == h, top[i], packed[i]) for i in range(k))

        zero = tuple(jnp.zeros((SUBLANES, LANES), F32) for _ in range(k))
        a = lax.fori_loop(0, PEER_HEADS, functools.partial(top_of_head, c=0), zero)
        b = lax.fori_loop(0, PEER_HEADS, functools.partial(top_of_head, c=1), zero)
        best = _top_sorted([a[i] + b[j] for i, j in _PEER_PAIRS], k)
        thr, vmax = best[k - 1], best[0]
        z = jnp.zeros((SUBLANES, LANES), F32)
        for i in range(k):
            z = z + jnp.exp(best[i] - vmax)
        inv_z = 1.0 / z

        def emit(h, carry):
            base = pl.multiple_of(h * nk, nk)
            s1 = sub_ref[0, pl.ds(base, nk), lanes]
            s2 = sub_ref[1, pl.ds(base, nk), lanes]
            e1_ref[h, :, lanes] = jnp.exp(s1 - head_row(a[0], h)) * head_row(inv_z, h)
            s1_ref[h, :, lanes] = s1
            s2_ref[h, :, lanes] = s2
            e2_ref[h, :, lanes] = jnp.exp(s2 - head_row(b[0], h))
            return carry

        lax.fori_loop(0, PEER_HEADS, emit, 0)
        thr_ref[:, lanes] = thr
        return carry

    lax.fori_loop(0, tt // LANES, lane_group, 0)


def peer_select(wf, ht, tt=256):
    d, s = ht.shape
    nrow = PEER_HEADS * PEER_NKEYS
    out = jax.ShapeDtypeStruct((PEER_HEADS, PEER_NKEYS, s), F32)
    ospec = pl.BlockSpec((PEER_HEADS, PEER_NKEYS, tt), lambda i: (0, 0, i))
    return pl.pallas_call(
        _peer_select_kernel,
        grid=(s // tt,),
        in_specs=[pl.BlockSpec((2, nrow, d), lambda i: (0, 0, 0)), pl.BlockSpec((d, tt), lambda i: (0, i))],
        out_specs=[ospec] * 4 + [pl.BlockSpec((PEER_HEADS, tt), lambda i: (0, i))],
        out_shape=[out] * 4 + [jax.ShapeDtypeStruct((PEER_HEADS, s), F32)],
        scratch_shapes=[pltpu.VMEM((2, nrow, tt), F32)],
        compiler_params=_params("parallel"),
        name="peer_select",
    )(wf, ht)


def _gelu_tanh(x):
    return 0.5 * x * (1.0 + jnp.tanh(math.sqrt(2.0 / math.pi) * (x + 0.044715 * (x * x * x))))


def _peer_dense_kernel(u_ref, ht_ref, vt_ref, e1_ref, s1_ref, s2_ref, e2_ref, thr_ref, o_ref, act_ref, ga_ref):
    nk = PEER_NKEYS
    j = pl.program_id(1)
    eb = u_ref.shape[0]

    @pl.when(j == 0)
    def _():
        o_ref[...] = jnp.zeros_like(o_ref)

    act_ref[...] = jnp.dot(u_ref[...], ht_ref[...], preferred_element_type=F32)
    for ii in range(eb // nk):
        i1 = j * (eb // nk) + ii
        act = _gelu_tanh(act_ref[pl.ds(ii * nk, nk), :])
        gate = None
        for h in range(PEER_HEADS):
            s1 = s1_ref[h, pl.ds(i1, 1), :]
            e1 = e1_ref[h, pl.ds(i1, 1), :]
            g = jnp.where(s2_ref[h] + s1 >= thr_ref[pl.ds(h, 1), :], e2_ref[h] * e1, 0.0)
            gate = g if gate is None else gate + g
        ga_ref[pl.ds(ii * nk, nk), :] = (gate * act).astype(BF16)
    o_ref[...] += jnp.dot(vt_ref[...], ga_ref[...], preferred_element_type=F32)


def peer_dense(u, ht, vt, e1, s1, s2, e2, thr, tt=512, eb=1024):
    d, s = ht.shape
    ne = u.shape[0]
    gspec = pl.BlockSpec((PEER_HEADS, PEER_NKEYS, tt), lambda i, j: (0, 0, i))
    return pl.pallas_call(
        _peer_dense_kernel,
        grid=(s // tt, ne // eb),
        in_specs=[
            pl.BlockSpec((eb, d), lambda i, j: (j, 0)),
            pl.BlockSpec((d, tt), lambda i, j: (0, i)),
            pl.BlockSpec((d, eb), lambda i, j: (0, j)),
            gspec, gspec, gspec, gspec,
            pl.BlockSpec((PEER_HEADS, tt), lambda i, j: (0, i)),
        ],
        out_specs=pl.BlockSpec((d, tt), lambda i, j: (0, i)),
        out_shape=jax.ShapeDtypeStruct((d, s), F32),
        scratch_shapes=[pltpu.VMEM((eb, tt), F32), pltpu.VMEM((eb, tt), BF16)],
        compiler_params=_params("parallel", "arbitrary"),
        name="peer_dense",
    )(u, ht, vt, e1, s1, s2, e2, thr)


def peer(x, ln_g, w_q, keys, u_bf, vt_bf, tm=512, tt=256):
    wf = peer_fold(w_q, keys).reshape(2, PEER_HEADS * PEER_NKEYS, D_MODEL).astype(BF16)
    ht = rmsnorm_t(x, ln_g, tm=tm)
    return peer_dense(u_bf, ht, vt_bf, *peer_select(wf, ht, tt=tt), tt=tm)


def _mla_kernel(qn_ref, qr_ref, qrot_ref, cq_ref, sq_ref, kn_ref, kr_ref, krot_ref, ck_ref, sk_ref, v_ref,
                o_ref, q1_ref, q2_ref, m_ref, l_ref, acc_ref, *, scale):
    qi, ki = pl.program_id(1), pl.program_id(2)
    tq, tk = qn_ref.shape[0], kn_ref.shape[0]

    @pl.when(ki == 0)
    def _():
        q1_ref[...] = (qn_ref[...] * scale).astype(BF16)
        q2_ref[...] = ((qr_ref[0] * cq_ref[...] + qrot_ref[0] * sq_ref[...]) * scale).astype(BF16)
        m_ref[...] = jnp.full_like(m_ref, -jnp.inf)
        l_ref[...] = jnp.zeros_like(l_ref)
        acc_ref[...] = jnp.zeros_like(acc_ref)

    @pl.when(ki <= qi)
    def _():
        kr = kr_ref[...] * ck_ref[...] + krot_ref[...] * sk_ref[...]
        s = _bdot_nt(q1_ref[...], kn_ref[...]) + _bdot_nt(q2_ref[...], kr)
        qpos = qi * tq + lax.broadcasted_iota(jnp.int32, (tq, tk), 0)
        kpos = ki * tk + lax.broadcasted_iota(jnp.int32, (tq, tk), 1)
        s = jnp.where(kpos <= qpos, s, -jnp.inf)
        m_old = m_ref[...]
        m_new = jnp.maximum(m_old, jnp.max(s, axis=-1, keepdims=True))
        alpha = jnp.exp(m_old - m_new)
        p = jnp.exp(s - m_new)
        l_ref[...] = alpha * l_ref[...] + jnp.sum(p, axis=-1, keepdims=True)
        acc_ref[...] = alpha * acc_ref[...] + _bdot(p, v_ref[...])
        m_ref[...] = m_new

    @pl.when(ki == qi)
    def _():
        o_ref[...] = acc_ref[...] / l_ref[...]


def mla_attention(qfull, kv, qr, qrot, kr, krot, cos, sin, t=512):
    s = qfull.shape[0]
    t = min(t, s)
    n = s // t
    hh, dn, dr = MLA_HEADS, MLA_NOPE, MLA_ROPE
    kmap = lambda h, qi, ki: (jnp.minimum(ki, qi), 0)
    return pl.pallas_call(
        functools.partial(_mla_kernel, scale=(MLA_NOPE + MLA_ROPE) ** -0.5),
        grid=(hh, n, n),
        in_specs=[
            pl.BlockSpec((t, dn), lambda h, qi, ki: (qi, h)),
            pl.BlockSpec((1, t, dr), lambda h, qi, ki: (h, qi, 0)),
            pl.BlockSpec((1, t, dr), lambda h, qi, ki: (h, qi, 0)),
            pl.BlockSpec((t, dr), lambda h, qi, ki: (qi, 0)),
            pl.BlockSpec((t, dr), lambda h, qi, ki: (qi, 0)),
            pl.BlockSpec((t, dn), lambda h, qi, ki: (jnp.minimum(ki, qi), h)),
            pl.BlockSpec((t, dr), kmap),
            pl.BlockSpec((t, dr), kmap),
            pl.BlockSpec((t, dr), kmap),
            pl.BlockSpec((t, dr), kmap),
            pl.BlockSpec((t, MLA_V), lambda h, qi, ki: (jnp.minimum(ki, qi), hh + h)),
        ],
        out_specs=pl.BlockSpec((t, MLA_V), lambda h, qi, ki: (qi, h)),
        out_shape=jax.ShapeDtypeStruct((s, hh * MLA_V), F32),
        scratch_shapes=[pltpu.VMEM((t, dn), BF16), pltpu.VMEM((t, dr), BF16), pltpu.VMEM((t, 1), F32),
                        pltpu.VMEM((t, 1), F32), pltpu.VMEM((t, MLA_V), F32)],
        compiler_params=_params("parallel", "parallel", "arbitrary"),
        name="mla_attention",
    )(qfull, qr, qrot, cos, sin, kv, kr, krot, cos, sin, kv)


def _causal_conv(cur_ref, prev_ref, w_ref, first, bias=None):
    prev = jnp.where(first, 0.0, prev_ref[...])
    xe = jnp.concatenate([prev, cur_ref[...]], axis=0)
    w = w_ref[...]
    acc = w[CONV_WIDTH - 1:CONV_WIDTH] * xe[SUBLANES:]
    for j in range(CONV_WIDTH - 1):
        acc = acc + w[j:j + 1] * pltpu.roll(xe, CONV_WIDTH - 1 - j, 0)[SUBLANES:]
    return acc if bias is None else acc + bias


def _unit_lower_inverse(low, c):
    r = lax.broadcasted_iota(jnp.int32, (c, c), 0)
    q = lax.broadcasted_iota(jnp.int32, (c, c), 1)
    eye = (r == q).astype(F32)
    d16 = jnp.where((r // 16) == (q // 16), low, 0.0)
    p = -d16
    x = eye + p
    for _ in range(3):
        p = _hdot(p, p)
        x = x + _hdot(x, p)
    prev = d16
    size = 32
    while size <= c:
        cur = jnp.where((r // size) == (q // size), low, 0.0) if size < c else low
        x = x - _hdot(_hdot(x, cur - prev), x)
        prev = cur
        size *= 2
    return x


def _gdn_kernel(q_ref, qp_ref, k_ref, kp_ref, v_ref, vp_ref, z_ref, wq_ref, wk_ref, wv_ref,
                ac_ref, ar_ref, bc_ref, alog_ref, dtb_ref, ng_ref, o_ref, st_ref):
    c = GDN_CHUNK
    tt = q_ref.shape[0]
    first = pl.program_id(1) == 0

    @pl.when(first)
    def _():
        st_ref[...] = jnp.zeros_like(st_ref)

    def l2n(x):
        return x * lax.rsqrt(jnp.sum(x * x, axis=-1, keepdims=True) + 1e-6)

    q = l2n(_silu(_causal_conv(q_ref, qp_ref, wq_ref, first))) * (GDN_DK ** -0.5)
    k = l2n(_silu(_causal_conv(k_ref, kp_ref, wk_ref, first)))
    v = _silu(_causal_conv(v_ref, vp_ref, wv_ref, first))
    neg_a = -jnp.exp(alog_ref[0, :, 0:1])
    dtb = dtb_ref[0, :, 0:1]
    r = lax.broadcasted_iota(jnp.int32, (c, c), 0)
    cc = lax.broadcasted_iota(jnp.int32, (c, c), 1)
    incl, strict = r >= cc, r > cc
    state = st_ref[...]
    outs = []
    for n in range(tt // c):
        sl = slice(n * c, (n + 1) * c)
        g_col = neg_a * _softplus(ac_ref[0, n] + dtb)
        g_row = neg_a * _softplus(ar_ref[0, n] + dtb)
        beta = _sigmoid(bc_ref[0, n])
        gc = jnp.sum(jnp.where(incl, g_row, 0.0), axis=1, keepdims=True)
        gr = jnp.sum(jnp.where(r <= cc, g_col, 0.0), axis=0, keepdims=True)
        decay = jnp.where(incl, jnp.exp(jnp.where(incl, gc - gr, 0.0)), 0.0)
        qc, kc, vc = q[sl], k[sl], v[sl]
        low = jnp.where(strict, beta * _hdot_nt(kc, kc) * decay, 0.0)
        eg = jnp.exp(gc)
        rhs = jnp.concatenate([beta * vc, (beta * eg) * kc], axis=1)
        sol = _hdot(_unit_lower_inverse(low, c), rhs)
        w1, w2 = sol[:, :GDN_DV], sol[:, GDN_DV:]
        a_qk = _hdot_nt(qc, kc) * decay
        g_last = gc[c - 1:c]
        k_end = kc * jnp.exp(g_last - gc)
        u = w1 - _hdot(w2, state)
        outs.append(_hdot(qc * eg, state) + _hdot(a_qk, u))
        state = jnp.exp(g_last) * state + _hdot(k_end.T, u)
    st_ref[...] = state
    o = jnp.concatenate(outs, axis=0)
    o_ref[...] = _rms(o, ng_ref[...]) * _silu(z_ref[...])


def gated_delta_net(proj, col0, a_raw, b_raw, conv_w, a_log, dt_bias, norm_g, tt=256):
    s = proj.shape[0]
    tt = min(tt, s)
    hh, c = GDN_HEADS, GDN_CHUNK
    b0 = col0 // LANES
    nblk = GDN_HEADS * GDN_DK // LANES

    def cur(g):
        return pl.BlockSpec((tt, LANES), lambda h, i: (i, b0 + g * nblk + h))

    def prev(g):
        return pl.BlockSpec((SUBLANES, LANES), lambda h, i: (jnp.maximum(i * (tt // SUBLANES) - 1, 0), b0 + g * nblk + h))

    def wspec(g):
        return pl.BlockSpec((CONV_WIDTH, LANES), lambda h, i: (0, g * nblk + h))

    a_t, b_t = a_raw.T, b_raw.T
    colspec = pl.BlockSpec((1, tt // c, c, 1), lambda h, i: (h, i, 0, 0))
    rowspec = pl.BlockSpec((1, tt // c, 1, c), lambda h, i: (h, i, 0, 0))
    hspec = pl.BlockSpec((1, 1, LANES), lambda h, i: (h, 0, 0))
    bcast = lambda p: jnp.broadcast_to(p.astype(F32)[:, None, None], (hh, 1, LANES))
    return pl.pallas_call(
        _gdn_kernel,
        grid=(hh, s // tt),
        in_specs=[cur(0), prev(0), cur(1), prev(1), cur(2), prev(2), cur(3), wspec(0), wspec(1), wspec(2),
                  colspec, rowspec, colspec, hspec, hspec, pl.BlockSpec((1, GDN_DV), lambda h, i: (0, 0))],
        out_specs=pl.BlockSpec((tt, GDN_DV), lambda h, i: (i, h)),
        out_shape=jax.ShapeDtypeStruct((s, hh * GDN_DV), F32),
        scratch_shapes=[pltpu.VMEM((GDN_DK, GDN_DV), F32)],
        compiler_params=_params("parallel", "arbitrary"),
        name="gated_delta_net",
    )(proj, proj, proj, proj, proj, proj, proj, conv_w, conv_w, conv_w,
      a_t.reshape(hh, s // c, c, 1), a_t.reshape(hh, s // c, 1, c), b_t.reshape(hh, s // c, c, 1),
      bcast(a_log), bcast(dt_bias), norm_g.reshape(1, GDN_DV).astype(F32))


def _rot_half_cols(w, half):
    return jnp.concatenate([-w[..., half:], w[..., :half]], axis=-1)


def _pad_cols(w, n):
    return jnp.pad(w, ((0, 0), (0, n - w.shape[1])))


AB_QKVZ = MLA_Q_RANK + MLA_KV_RANK + 2 * MLA_ROPE
AB_GATES = AB_QKVZ + 4 * GDN_HEADS * GDN_DK
AB_PAD = AB_GATES + LANES


def prep_ab(w_in, w_uq, w_ukv, w_out):
    rq, rkv, rr = MLA_Q_RANK, MLA_KV_RANK, MLA_ROPE
    w_kr = w_in[:, rq + rkv:rq + rkv + rr]
    w_main = jnp.concatenate([w_in[:, :rq + rkv], w_kr, _rot_half_cols(w_kr, rr // 2),
                              w_in[:, rq + rkv + rr:]], axis=1)
    w_main = _pad_cols(w_main, AB_PAD).astype(BF16)
    uq = w_uq.reshape(rq, MLA_HEADS, MLA_NOPE + MLA_ROPE)
    uq_r = uq[..., MLA_NOPE:]
    uq2 = jnp.concatenate([uq[..., :MLA_NOPE].reshape(rq, -1), uq_r.reshape(rq, -1),
                           _rot_half_cols(uq_r, rr // 2).reshape(rq, -1)], axis=1).astype(BF16)
    ukv = w_ukv.reshape(rkv, MLA_HEADS, MLA_NOPE + MLA_V)
    ukv2 = jnp.concatenate([ukv[..., :MLA_NOPE].reshape(rkv, -1), ukv[..., MLA_NOPE:].reshape(rkv, -1)], axis=1).astype(BF16)
    return w_main, uq2, ukv2, w_out.astype(BF16)


def mixer_ab(x, ln, w_main, uq2, ukv2, w_out, q_norm, kv_norm, conv_w, a_log, dt_bias, gdn_norm, cos2, sin2,
             tm=512, t_attn=512, t_gdn=256):
    s = x.shape[0]
    rq, rkv, rr = MLA_Q_RANK, MLA_KV_RANK, MLA_ROPE
    proj = norm_matmul(x, w_main, gain=ln, tm=tm)
    qfull = norm_matmul(proj[:, :rq], uq2, gain=q_norm, tm=tm)
    kv = norm_matmul(proj[:, rq:rq + rkv], ukv2, gain=kv_norm, tm=tm)
    nn = MLA_HEADS * MLA_NOPE
    heads = lambda a: a.reshape(s, MLA_HEADS, rr).transpose(1, 0, 2)
    o_a = mla_attention(qfull, kv, heads(qfull[:, nn:nn + MLA_HEADS * rr]), heads(qfull[:, nn + MLA_HEADS * rr:]),
                        proj[:, rq + rkv:rq + rkv + rr], proj[:, rq + rkv + rr:AB_QKVZ], cos2, sin2, t=t_attn)
    o_b = gated_delta_net(proj, AB_QKVZ, proj[:, AB_GATES:AB_GATES + GDN_HEADS],
                          proj[:, AB_GATES + GDN_HEADS:AB_GATES + 2 * GDN_HEADS], conv_w, a_log, dt_bias, gdn_norm, tt=t_gdn)
    return norm_matmul(jnp.concatenate([o_a, o_b], axis=1), w_out, residual=x, tm=tm)


CD_R, CD_K, CD_V, CD_Z, CD_DT, CD_X = (i * 1024 for i in range(6))
CD_B = 6144
CD_C = CD_B + SSD_GROUPS * SSD_STATE
CD_WA = CD_C + SSD_GROUPS * SSD_STATE
CD_G = CD_WA + LANES
CD_PAD = 7168
GROUP_W = SSD_INNER // SSD_GROUPS


def _lower_ones(c):
    r = lax.broadcasted_iota(jnp.int32, (c, c), 0)
    q = lax.broadcasted_iota(jnp.int32, (c, c), 1)
    return r >= q, r > q


def _ssd_kernel(x_ref, xp_ref, b_ref, bp_ref, c_ref, cp_ref, z_ref, dt_ref, wx_ref, wb_ref, wc_ref,
                bx_ref, bb_ref, bc_ref, dtb_ref, alog_ref, dskip_ref, ng_ref, o_ref, st_ref):
    c = SSD_CHUNK
    tt = x_ref.shape[0]
    first = pl.program_id(1) == 0

    @pl.when(first)
    def _():
        st_ref[...] = jnp.zeros_like(st_ref)

    xs_all = _silu(_causal_conv(x_ref, xp_ref, wx_ref, first, bx_ref[...]))
    bm_all = _silu(_causal_conv(b_ref, bp_ref, wb_ref, first, bb_ref[...]))
    cm_all = _silu(_causal_conv(c_ref, cp_ref, wc_ref, first, bc_ref[...]))
    dt_all = _softplus(dt_ref[...] + dtb_ref[...])
    a_all = -jnp.exp(alog_ref[...]) * dt_all
    incl, _ = _lower_ones(c)
    tri = incl.astype(F32)
    left = lax.broadcasted_iota(jnp.int32, (c, LANES), 1) < SSD_HEADDIM
    npair = GROUP_W // LANES
    outs = []
    for n in range(tt // c):
        sl = slice(n * c, (n + 1) * c)
        xs, bm, cm, dt = xs_all[sl], bm_all[sl], cm_all[sl], dt_all[sl]
        acs = _hdot(tri, a_all[sl])
        xdt = xs * dt
        cb = _hdot_nt(cm, bm)
        bm_t = bm.T
        ys = []
        for p in range(npair):
            ls = slice(p * LANES, (p + 1) * LANES)
            acs_p = acs[:, ls]
            acs_t = acs_p.T
            xp = xdt[:, ls]
            yd = []
            for hd in range(2):
                col = acs_p[:, hd * SSD_HEADDIM:hd * SSD_HEADDIM + 1]
                row = acs_t[hd * SSD_HEADDIM:hd * SSD_HEADDIM + 1, :]
                lmat = jnp.where(incl, jnp.exp(jnp.where(incl, col - row, 0.0)), 0.0)
                yd.append(_hdot(cb * lmat, xp))
            last = acs_p[c - 1:c]
            prev_t = st_ref[p]
            y_off = _hdot(cm, prev_t) * jnp.exp(acs_p)
            st_ref[p] = jnp.exp(last) * prev_t + _hdot(bm_t, xp * jnp.exp(last - acs_p))
            ys.append(jnp.where(left, yd[0], yd[1]) + y_off)
        outs.append(jnp.concatenate(ys, axis=1) + xs * dskip_ref[...])
    y = jnp.concatenate(outs, axis=0) * _silu(z_ref[...])
    o_ref[...] = _rms(y, ng_ref[...])


def mamba2_ssd(proj, conv_w, conv_b, dt_bias, a_log, d_skip, norm_g, tt=256):
    s = proj.shape[0]
    tt = min(tt, s)
    gw, ns = GROUP_W, SSD_STATE
    per = lambda v: jnp.repeat(v.astype(F32), SSD_HEADDIM).reshape(1, SSD_INNER)

    def cur(col, w):
        return pl.BlockSpec((tt, w), lambda g, i: (i, col // w + g))

    def prev(col, w):
        return pl.BlockSpec((SUBLANES, w), lambda g, i: (jnp.maximum(i * (tt // SUBLANES) - 1, 0), col // w + g))

    def par(rows, col, w):
        return pl.BlockSpec((rows, w), lambda g, i: (0, col // w + g))

    cb = conv_b.reshape(1, -1).astype(F32)
    return pl.pallas_call(
        _ssd_kernel,
        grid=(SSD_GROUPS, s // tt),
        in_specs=[cur(CD_X, gw), prev(CD_X, gw), cur(CD_B, ns), prev(CD_B, ns), cur(CD_C, ns), prev(CD_C, ns),
                  cur(CD_Z, gw), cur(CD_DT, gw),
                  par(CONV_WIDTH, 0, gw), par(CONV_WIDTH, SSD_INNER, ns), par(CONV_WIDTH, SSD_INNER + SSD_GROUPS * ns, ns),
                  par(1, 0, gw), par(1, SSD_INNER, ns), par(1, SSD_INNER + SSD_GROUPS * ns, ns),
                  par(1, 0, gw), par(1, 0, gw), par(1, 0, gw), par(1, 0, gw)],
        out_specs=pl.BlockSpec((tt, gw), lambda g, i: (i, g)),
        out_shape=jax.ShapeDtypeStruct((s, SSD_INNER), F32),
        scratch_shapes=[pltpu.VMEM((gw // LANES, ns, LANES), F32)],
        compiler_params=_params("parallel", "arbitrary"),
        name="mamba2_ssd",
    )(proj, proj, proj, proj, proj, proj, proj, proj, conv_w, conv_w, conv_w, cb, cb, cb,
      per(dt_bias), per(a_log), per(d_skip), norm_g.reshape(1, SSD_INNER).astype(F32))


def _pair_ones():
    r = lax.broadcasted_iota(jnp.int32, (LANES, LANES), 0)
    q = lax.broadcasted_iota(jnp.int32, (LANES, LANES), 1)
    return (r // RWKV_HEAD) == (q // RWKV_HEAD)


def _head_sums(x, ones):
    return jnp.concatenate([_hdot(x[:, i * LANES:(i + 1) * LANES], ones) for i in range(x.shape[1] // LANES)], axis=1)


def _rwkv_prep_kernel(r_ref, rp_ref, k_ref, kp_ref, v_ref, vp_ref, wa_ref, wap_ref, g0_ref, g0p_ref, g1_ref, g1p_ref,
                      mur_ref, muk_ref, muv_ref, muwa_ref, mug0_ref, mug1_ref, w0_ref, w2_ref, a0_ref, a2_ref, g2_ref,
                      kk_ref, ka_ref, rk_ref,
                      ro_ref, lw_ref, ko_ref, vo_ref, po_ref, qo_ref, go_ref, bo_ref):
    first = pl.program_id(0) == 0

    def mix(cur_ref, prev_ref, mu_ref):
        cur = cur_ref[...]
        prev = jnp.where(first, 0.0, prev_ref[...])
        shifted = pltpu.roll(jnp.concatenate([prev, cur], axis=0), 1, 0)[SUBLANES:]
        return cur + (shifted - cur) * mu_ref[...]

    r = mix(r_ref, rp_ref, mur_ref)
    k = mix(k_ref, kp_ref, muk_ref)
    v = mix(v_ref, vp_ref, muv_ref)
    wa = mix(wa_ref, wap_ref, muwa_ref)
    g0 = mix(g0_ref, g0p_ref, mug0_ref)
    g1 = mix(g1_ref, g1p_ref, mug1_ref)
    log_w = -math.exp(-0.5) * _sigmoid(w0_ref[...] + _bdot(jnp.tanh(wa), w2_ref[...]))
    a = _sigmoid(a0_ref[...] + _bdot(wa, a2_ref[...]))
    gate = _bdot(_sigmoid(g0), g2_ref[0:LANES, :]) + _bdot(_sigmoid(g1), g2_ref[LANES:2 * LANES, :])
    ones = _pair_ones().astype(F32)
    kx = k * kk_ref[...]
    kk = kx * lax.rsqrt(_head_sums(kx * kx, ones) + 1e-6)
    k_mod = k * (1.0 + (a - 1.0) * ka_ref[...])
    ro_ref[...] = r
    lw_ref[...] = log_w
    ko_ref[...] = k_mod
    vo_ref[...] = v
    po_ref[...] = -kk * a
    qo_ref[...] = kk
    go_ref[...] = gate
    bo_ref[...] = _head_sums(r * k_mod * rk_ref[...], ones) * v


def rwkv_prep(proj, mu, w0, w2, a0, a2, g2, k_k, k_a, r_k, tt=256):
    s = proj.shape[0]
    tt = min(tt, s)
    ri = RWKV_INNER
    row = lambda v: v.reshape(1, -1).astype(F32)
    mu_r, mu_k, mu_v = (row(mu[i * ri:(i + 1) * ri]) for i in range(3))
    mu_wa = row(mu[3 * ri:3 * ri + LANES])
    mu_g = row(jnp.pad(mu[3 * ri + LANES:], (0, 2 * LANES - RWKV_G_LORA)))
    zeros = jnp.zeros((RWKV_W_LORA, ri), F32)
    w2p = jnp.concatenate([w2, zeros], axis=0).astype(BF16)
    a2p = jnp.concatenate([zeros, a2], axis=0).astype(BF16)
    g2p = jnp.pad(g2, ((0, 2 * LANES - RWKV_G_LORA), (0, 0))).astype(BF16)

    def cur(col, w):
        return pl.BlockSpec((tt, w), lambda i: (i, col // w))

    def prev(col, w):
        return pl.BlockSpec((SUBLANES, w), lambda i: (jnp.maximum(i * (tt // SUBLANES) - 1, 0), col // w))

    full = lambda a: pl.BlockSpec(a.shape, lambda i: (0, 0))
    params = [mu_r, mu_k, mu_v, mu_wa, mu_g[:, :LANES], mu_g[:, LANES:], row(w0), w2p, row(a0), a2p, g2p,
              row(k_k), row(k_a), row(r_k)]
    out = jax.ShapeDtypeStruct((s, ri), F32)
    return pl.pallas_call(
        _rwkv_prep_kernel,
        grid=(s // tt,),
        in_specs=[cur(CD_R, ri), prev(CD_R, ri), cur(CD_K, ri), prev(CD_K, ri), cur(CD_V, ri), prev(CD_V, ri),
                  cur(CD_WA, LANES), prev(CD_WA, LANES), cur(CD_G, LANES), prev(CD_G, LANES),
                  cur(CD_G + LANES, LANES), prev(CD_G + LANES, LANES)] + [full(a) for a in params],
        out_specs=[pl.BlockSpec((tt, ri), lambda i: (i, 0))] * 8,
        out_shape=[out] * 8,
        compiler_params=_params("arbitrary"),
        name="rwkv_prep",
    )(*([proj] * 12), *params)


RWKV_MY_CHUNK = 64


def _rwkv_scan_kernel(r_ref, lw_ref, k_ref, v_ref, p_ref, q_ref, g_ref, b_ref, lnw_ref, lnb_ref, o_ref, st_ref):
    c = RWKV_MY_CHUNK
    tt = r_ref.shape[0]

    @pl.when(pl.program_id(1) == 0)
    def _():
        st_ref[...] = jnp.zeros_like(st_ref)

    incl, strict = _lower_ones(c)
    tri = incl.astype(F32)
    left = lax.broadcasted_iota(jnp.int32, (c, LANES), 1) < RWKV_HEAD
    pair = _pair_ones()
    ones = pair.astype(F32)
    r_id = lax.broadcasted_iota(jnp.int32, (LANES, LANES), 0)
    c_id = lax.broadcasted_iota(jnp.int32, (LANES, LANES), 1)
    state = st_ref[...]
    outs = []
    for n in range(tt // c):
        sl = slice(n * c, (n + 1) * c)
        r, w, k, v, p, q = (ref[sl, :] for ref in (r_ref, lw_ref, k_ref, v_ref, p_ref, q_ref))
        lw = _hdot(tri, w)
        lam_in, inv_lam = jnp.exp(lw), jnp.exp(-lw)
        q_bar, r_bar, p_t, k_t = q * jnp.exp(lw - w), r * lam_in, p * inv_lam, k * inv_lam
        ws, wv, y_loc, y_rp = [], [], [], []
        for hd in range(2):
            mine = left if hd == 0 else jnp.logical_not(left)
            qb = jnp.where(mine, q_bar, 0.0)
            rb = jnp.where(mine, r_bar, 0.0)
            m_qp = jnp.where(strict, _hdot_nt(qb, p_t), 0.0)
            m_qk = jnp.where(strict, _hdot_nt(qb, k_t), 0.0)
            m_rp = jnp.where(incl, _hdot_nt(rb, p_t), 0.0)
            m_rk = jnp.where(incl, _hdot_nt(rb, k_t), 0.0)
            inv = _unit_lower_inverse(-m_qp, c)
            ws.append(_hdot(inv, q_bar))
            wv.append(_hdot(inv, _hdot(m_qk, v)))
            y_loc.append(_hdot(m_rk, v))
            y_rp.append(m_rp)
        sel = lambda two: jnp.where(left, two[0], two[1])
        lam_end = lam_in[c - 1:c]
        u = _hdot(sel(ws), state) + sel(wv)
        y = _hdot(r_bar, state) + sel([_hdot(y_rp[0], u), _hdot(y_rp[1], u)]) + sel(y_loc)
        decay = jnp.where(r_id == c_id, lam_end, 0.0)
        upd = _hdot((p_t * lam_end).T, u) + _hdot((k_t * lam_end).T, v)
        state = _hdot(decay, state) + jnp.where(pair, upd, 0.0)
        outs.append(y)
    st_ref[...] = state
    y = jnp.concatenate(outs, axis=0)
    mean = _hdot(y, ones) * (1.0 / RWKV_HEAD)
    yc = y - mean
    var = _hdot(yc * yc, ones) * (1.0 / RWKV_HEAD)
    y = yc * lax.rsqrt(var + RWKV_GN_EPS) * lnw_ref[...] + lnb_ref[...]
    o_ref[...] = (y + b_ref[...]) * g_ref[...]


def rwkv_scan(r, lw, k, v, p, q, gate, bonus, ln_w, ln_b, tt=256):
    s = r.shape[0]
    tt = min(tt, s)
    spec = pl.BlockSpec((tt, LANES), lambda h, i: (i, h))
    pspec = pl.BlockSpec((1, LANES), lambda h, i: (0, h))
    return pl.pallas_call(
        _rwkv_scan_kernel,
        grid=(RWKV_INNER // LANES, s // tt),
        in_specs=[spec] * 8 + [pspec] * 2,
        out_specs=spec,
        out_shape=jax.ShapeDtypeStruct((s, RWKV_INNER), F32),
        scratch_shapes=[pltpu.VMEM((LANES, LANES), F32)],
        compiler_params=_params("parallel", "arbitrary"),
        name="rwkv_scan",
    )(r, lw, k, v, p, q, gate, bonus, ln_w.reshape(1, -1).astype(F32), ln_b.reshape(1, -1).astype(F32))


def prep_cd(w_in, w_out):
    si, ri = SSD_INNER, RWKV_INNER
    z, xbc, dt, rw = w_in[:, :si], w_in[:, si:2 * si + 512], w_in[:, 2 * si + 512:2 * si + 528], w_in[:, 2 * si + 528:]
    dt_exp = jnp.repeat(dt, SSD_HEADDIM, axis=1)
    cols = [rw[:, :3 * ri], z, dt_exp, xbc, rw[:, 3 * ri:]]
    return _pad_cols(jnp.concatenate(cols, axis=1), CD_PAD).astype(BF16), w_out.astype(BF16)


def mixer_cd(x, ln, w_main, w_out, ssd_conv_w, ssd_conv_b, ssd_dt_bias, ssd_a_log, ssd_d, ssd_norm,
             mu, w0, w2, a0, a2, g2, k_k, k_a, r_k, ln_w, ln_b, tm=512, tt=256):
    proj = norm_matmul(x, w_main, gain=ln, tm=tm)
    o_c = mamba2_ssd(proj, ssd_conv_w, ssd_conv_b, ssd_dt_bias, ssd_a_log, ssd_d, ssd_norm, tt=tt)
    o_d = rwkv_scan(*rwkv_prep(proj, mu, w0, w2, a0, a2, g2, k_k, k_a, r_k.reshape(-1), tt=tt), ln_w, ln_b, tt=tt)
    return norm_matmul(jnp.concatenate([o_c, o_d], axis=1), w_out, residual=x, tm=tm)


def _rope_tables(s):
    inv = 1.0 / (ROPE_THETA ** (jnp.arange(0, MLA_ROPE, 2, dtype=F32) / MLA_ROPE))
    ang = jnp.arange(s, dtype=F32)[:, None] * inv[None, :]
    cos, sin = jnp.cos(ang), jnp.sin(ang)
    return jnp.concatenate([cos, cos], axis=1), jnp.concatenate([sin, sin], axis=1)


def kernel(x, p, ln_mix, ln_ffn, ab_w_in, mla_q_norm, mla_w_uq, mla_kv_norm, mla_w_ukv, gdn_conv_w, gdn_a_log, gdn_dt_bias, gdn_norm, ab_w_out, cd_w_in, ssd_conv_w, ssd_conv_b, ssd_dt_bias, ssd_a_log, ssd_d, ssd_norm, rwkv_mu, rwkv_w0, rwkv_w2, rwkv_a0, rwkv_a2, rwkv_g2, rwkv_k_k, rwkv_k_a, rwkv_r_k, rwkv_ln_w, rwkv_ln_b, cd_w_out, peer_w_q, peer_keys, peer_u, peer_v, ple_w_proj, ple_norm, ple_w_gate, final_norm):
    assert x.shape[0] == 1
    s = x.shape[1]
    tm = min(512, s)
    tt = min(256, s)
    cos2, sin2 = _rope_tables(s)
    xs = x[0]
    for i in range(DEPTH):
        j = i // 2
        if i % 2 == 0:
            wts = prep_ab(ab_w_in[j], mla_w_uq[j], mla_w_ukv[j], ab_w_out[j])
            xs = mixer_ab(xs, ln_mix[i], *wts, mla_q_norm[j], mla_kv_norm[j], gdn_conv_w[j], gdn_a_log[j],
                          gdn_dt_bias[j], gdn_norm[j], cos2, sin2, tm=tm, t_attn=tm, t_gdn=tt)
        else:
            wts = prep_cd(cd_w_in[j], cd_w_out[j])
            xs = mixer_cd(xs, ln_mix[i], *wts, ssd_conv_w[j], ssd_conv_b[j], ssd_dt_bias[j], ssd_a_log[j], ssd_d[j],
                          ssd_norm[j], rwkv_mu[j], rwkv_w0[j], rwkv_w2[j], rwkv_a0[j], rwkv_a2[j], rwkv_g2[j],
                          rwkv_k_k[j], rwkv_k_a[j], rwkv_r_k[j], rwkv_ln_w[j], rwkv_ln_b[j], tm=tm, tt=tt)
        xs = xs + peer(xs, ln_ffn[i], peer_w_q[i], peer_keys[i], peer_u[i].astype(BF16), peer_v[i].T.astype(BF16), tm=tm, tt=tt).T
        xs = ple_update(xs, p[i, 0], ple_norm[i], ple_w_gate[i].astype(BF16), ple_w_proj[i].astype(BF16), tm=tm)
    return rmsnorm(xs, final_norm, tm=tm)[None]
```

```python
import functools
import math

import jax
import jax.numpy as jnp
from jax import lax
from jax.experimental import pallas as pl
from jax.experimental.pallas import tpu as pltpu

F32 = jnp.float32
BF16 = jnp.bfloat16
HIGHEST = lax.Precision.HIGHEST

D_MODEL = 2048
DEPTH = 4
PLE_DIM = 256
RMS_EPS = 1e-6
MLA_HEADS = 8
MLA_Q_RANK = 512
MLA_KV_RANK = 256
MLA_NOPE = 128
MLA_ROPE = 64
MLA_V = 128
ROPE_THETA = 10000.0
GDN_HEADS = 8
GDN_DK = 128
GDN_DV = 128
GDN_CHUNK = 64
SSD_HEADS = 16
SSD_HEADDIM = 64
SSD_GROUPS = 2
SSD_STATE = 128
SSD_CHUNK = 128
SSD_INNER = SSD_HEADS * SSD_HEADDIM
RWKV_HEADS = 16
RWKV_HEAD = 64
RWKV_INNER = RWKV_HEADS * RWKV_HEAD
RWKV_W_LORA = 64
RWKV_A_LORA = 64
RWKV_G_LORA = 160
RWKV_GN_EPS = 64e-5
CONV_WIDTH = 4
PEER_HEADS = 8
PEER_NKEYS = 128
PEER_EXPERTS = PEER_NKEYS * PEER_NKEYS
PEER_QDIM = 256
PEER_TOPK = 16

LANES = 128
SUBLANES = 8
VMEM_LIMIT = 56 * 1024 * 1024


def _params(*sem):
    return pltpu.CompilerParams(dimension_semantics=sem, vmem_limit_bytes=VMEM_LIMIT)


def _bdot(a, b):
    return jnp.dot(a.astype(BF16), b.astype(BF16), preferred_element_type=F32)


def _bdot_nt(a, b):
    return lax.dot_general(a.astype(BF16), b.astype(BF16), (((1,), (1,)), ((), ())), preferred_element_type=F32)


def _hdot(a, b):
    return jnp.dot(a, b, preferred_element_type=F32, precision=HIGHEST)


def _hdot_nt(a, b):
    return lax.dot_general(a, b, (((1,), (1,)), ((), ())), preferred_element_type=F32, precision=HIGHEST)


def _rms(x, gain):
    return x * lax.rsqrt(jnp.mean(x * x, axis=-1, keepdims=True) + RMS_EPS) * gain


def _sigmoid(x):
    return 1.0 / (1.0 + jnp.exp(-x))


def _silu(x):
    return x * _sigmoid(x)


def _softplus(x):
    return jnp.maximum(x, 0.0) + jnp.log(1.0 + jnp.exp(-jnp.abs(x)))


def _nm_kernel(*refs, has_norm, has_res):
    it = iter(refs)
    x_ref = next(it)
    g_ref = next(it) if has_norm else None
    w_ref = next(it)
    r_ref = next(it) if has_res else None
    o_ref = next(it)
    xn_ref = next(it)

    @pl.when(pl.program_id(1) == 0)
    def _():
        x = x_ref[...].astype(F32)
        if has_norm:
            x = _rms(x, g_ref[...])
        xn_ref[...] = x.astype(BF16)

    acc = jnp.dot(xn_ref[...], w_ref[...], preferred_element_type=F32)
    if has_res:
        acc = acc + r_ref[...]
    o_ref[...] = acc.astype(o_ref.dtype)


def norm_matmul(x, w, gain=None, residual=None, tm=512, tn=512, out_dtype=F32):
    m, k = x.shape
    n = w.shape[1]
    tn = min(tn, n)
    assert m % tm == 0 and n % tn == 0
    in_specs = [pl.BlockSpec((tm, k), lambda i, j: (i, 0))]
    args = [x]
    if gain is not None:
        in_specs.append(pl.BlockSpec((1, k), lambda i, j: (0, 0)))
        args.append(gain.reshape(1, k).astype(F32))
    in_specs.append(pl.BlockSpec((k, tn), lambda i, j: (0, j)))
    args.append(w)
    if residual is not None:
        in_specs.append(pl.BlockSpec((tm, tn), lambda i, j: (i, j)))
        args.append(residual)
    return pl.pallas_call(
        functools.partial(_nm_kernel, has_norm=gain is not None, has_res=residual is not None),
        grid=(m // tm, n // tn),
        in_specs=in_specs,
        out_specs=pl.BlockSpec((tm, tn), lambda i, j: (i, j)),
        out_shape=jax.ShapeDtypeStruct((m, n), out_dtype),
        scratch_shapes=[pltpu.VMEM((tm, k), BF16)],
        compiler_params=_params("parallel", "arbitrary"),
        name="norm_matmul",
    )(*args)


def _ple_kernel(x_ref, g_ref, wg_ref, p_ref, wp_ref, xc_ref, o_ref, xn_ref):
    @pl.when(pl.program_id(1) == 0)
    def _():
        xn_ref[...] = _rms(x_ref[...], g_ref[...]).astype(BF16)

    gate = _sigmoid(jnp.dot(xn_ref[...], wg_ref[...], preferred_element_type=F32))
    emb = _bdot(p_ref[...], wp_ref[...])
    o_ref[...] = xc_ref[...] + gate * emb


def ple_update(x, p_i, norm_g, w_gate, w_proj, tm=512, tn=512):
    m, d = x.shape
    pd = p_i.shape[1]
    return pl.pallas_call(
        _ple_kernel,
        grid=(m // tm, d // tn),
        in_specs=[
            pl.BlockSpec((tm, d), lambda i, j: (i, 0)),
            pl.BlockSpec((1, d), lambda i, j: (0, 0)),
            pl.BlockSpec((d, tn), lambda i, j: (0, j)),
            pl.BlockSpec((tm, pd), lambda i, j: (i, 0)),
            pl.BlockSpec((pd, tn), lambda i, j: (0, j)),
            pl.BlockSpec((tm, tn), lambda i, j: (i, j)),
        ],
        out_specs=pl.BlockSpec((tm, tn), lambda i, j: (i, j)),
        out_shape=jax.ShapeDtypeStruct((m, d), F32),
        scratch_shapes=[pltpu.VMEM((tm, d), BF16)],
        compiler_params=_params("parallel", "arbitrary"),
        name="ple_update",
    )(x, norm_g.reshape(1, d), w_gate, p_i, w_proj, x)


def _rmsnorm_kernel(x_ref, g_ref, o_ref):
    o_ref[...] = _rms(x_ref[...], g_ref[...])


def rmsnorm(x, gain, tm=512):
    m, d = x.shape
    return pl.pallas_call(
        _rmsnorm_kernel,
        grid=(m // tm,),
        in_specs=[pl.BlockSpec((tm, d), lambda i: (i, 0)), pl.BlockSpec((1, d), lambda i: (0, 0))],
        out_specs=pl.BlockSpec((tm, d), lambda i: (i, 0)),
        out_shape=jax.ShapeDtypeStruct((m, d), F32),
        compiler_params=_params("parallel"),
        name="rmsnorm",
    )(x, gain.reshape(1, d))


def _peer_fold_kernel(keys_ref, wq_ref, o_ref):
    o_ref[0, 0] = _hdot_nt(keys_ref[0, 0], wq_ref[...])


def peer_fold(w_q, keys):
    hk = PEER_QDIM // 2
    return pl.pallas_call(
        _peer_fold_kernel,
        grid=(2, PEER_HEADS),
        in_specs=[
            pl.BlockSpec((1, 1, PEER_NKEYS, hk), lambda c, h: (h, c, 0, 0)),
            pl.BlockSpec((D_MODEL, hk), lambda c, h: (0, h * 2 + c)),
        ],
        out_specs=pl.BlockSpec((1, 1, PEER_NKEYS, D_MODEL), lambda c, h: (c, h, 0, 0)),
        out_shape=jax.ShapeDtypeStruct((2, PEER_HEADS, PEER_NKEYS, D_MODEL), F32),
        compiler_params=_params("parallel", "parallel"),
        name="peer_fold",
    )(keys, w_q)


def _rmsnorm_t_kernel(x_ref, g_ref, o_ref):
    o_ref[...] = _rms(x_ref[...], g_ref[...]).T.astype(BF16)


def rmsnorm_t(x, gain, tm=512):
    m, d = x.shape
    return pl.pallas_call(
        _rmsnorm_t_kernel,
        grid=(m // tm,),
        in_specs=[pl.BlockSpec((tm, d), lambda i: (i, 0)), pl.BlockSpec((1, d), lambda i: (0, 0))],
        out_specs=pl.BlockSpec((d, tm), lambda i: (0, i)),
        out_shape=jax.ShapeDtypeStruct((d, m), BF16),
        compiler_params=_params("parallel"),
        name="rmsnorm_t",
    )(x, gain.reshape(1, d))


def _sort_desc(v):
    v = list(v)
    n = len(v)
    k = 2
    while k <= n:
        j = k // 2
        while j >= 1:
            for i in range(n):
                l = i ^ j
                if l > i:
                    hi, lo = jnp.maximum(v[i], v[l]), jnp.minimum(v[i], v[l])
                    v[i], v[l] = (hi, lo) if (i & k) == 0 else (lo, hi)
            j //= 2
        k *= 2
    return v


def _merge_top(a, b):
    n = len(a)
    v = [jnp.maximum(a[i], b[n - 1 - i]) for i in range(n)]
    j = n // 2
    while j >= 1:
        for i in range(n):
            l = i ^ j
            if l > i:
                v[i], v[l] = jnp.maximum(v[i], v[l]), jnp.minimum(v[i], v[l])
        j //= 2
    return v


def _top_sorted(vals, n):
    vals = list(vals)
    while len(vals) % n:
        vals.append(jnp.full_like(vals[0], -jnp.inf))
    acc = _sort_desc(vals[:n])
    for g in range(1, len(vals) // n):
        acc = _merge_top(acc, _sort_desc(vals[g * n:(g + 1) * n]))
    return acc


_PEER_PAIRS = [(i, j) for i in range(PEER_TOPK) for j in range(PEER_TOPK) if (i + 1) * (j + 1) <= PEER_TOPK]


def _peer_select_kernel(wf_ref, ht_ref, e1_ref, n1_ref, r2_ref, e2_ref, sub_ref):
    nk, k = PEER_NKEYS, PEER_TOPK
    tt = ht_ref.shape[1]
    ht = ht_ref[...]
    for c in range(2):
        sub_ref[c] = jnp.dot(wf_ref[c], ht, preferred_element_type=F32)
    row = lax.broadcasted_iota(jnp.int32, (SUBLANES, LANES), 0)

    def head_row(v, h):
        return jnp.sum(jnp.where(row == h, v, 0.0), axis=0, keepdims=True)

    def lane_group(lg, carry):
        lanes = pl.ds(pl.multiple_of(lg * LANES, LANES), LANES)

        def top_of_head(h, packed, c):
            base = pl.multiple_of(h * nk, nk)
            slabs = [sub_ref[c, pl.ds(base + SUBLANES * j, SUBLANES), lanes] for j in range(nk // SUBLANES)]
            top = _sort_desc(slabs)
            for sh in (4, 2, 1):
                top = _merge_top(top, [pltpu.roll(t, sh, 0) for t in top])
            return tuple(jnp.where(row == h, top[i], packed[i]) for i in range(k))

        zero = tuple(jnp.zeros((SUBLANES, LANES), F32) for _ in range(k))
        a = lax.fori_loop(0, PEER_HEADS, functools.partial(top_of_head, c=0), zero)
        b = lax.fori_loop(0, PEER_HEADS, functools.partial(top_of_head, c=1), zero)
        best = _top_sorted([a[i] + b[j] for i, j in _PEER_PAIRS], k)
        thr, vmax = best[k - 1], best[0]
        z = jnp.zeros((SUBLANES, LANES), F32)
        for i in range(k):
            z = z + jnp.exp(best[i] - vmax)
        inv_z = 1.0 / z

        def emit(h, carry):
            base = pl.multiple_of(h * nk, nk)
            s1 = sub_ref[0, pl.ds(base, nk), lanes]
            s2 = sub_ref[1, pl.ds(base, nk), lanes]
            thr_h = head_row(thr, h)
            n1 = jnp.zeros_like(s1)
            r2 = jnp.zeros_like(s2)
            for j in range(k):
                bj = head_row(b[j], h)
                n1 = n1 + jnp.where(s1 + bj >= thr_h, 1.0, 0.0)
                r2 = r2 + jnp.where(bj > s2, 1.0, 0.0)
            e1_ref[h, :, lanes] = jnp.exp(s1 - head_row(a[0], h)) * head_row(inv_z, h)
            n1_ref[h, :, lanes] = n1
            r2_ref[h, :, lanes] = r2.astype(BF16)
            e2_ref[h, :, lanes] = jnp.exp(s2 - head_row(b[0], h)).astype(BF16)
            return carry

        lax.fori_loop(0, PEER_HEADS, emit, 0)
        return carry

    lax.fori_loop(0, tt // LANES, lane_group, 0)


def peer_select(wf, ht, tt=256):
    d, s = ht.shape
    nrow = PEER_HEADS * PEER_NKEYS
    shape = (PEER_HEADS, PEER_NKEYS, s)
    ospec = pl.BlockSpec((PEER_HEADS, PEER_NKEYS, tt), lambda i: (0, 0, i))
    return pl.pallas_call(
        _peer_select_kernel,
        grid=(s // tt,),
        in_specs=[pl.BlockSpec((2, nrow, d), lambda i: (0, 0, 0)), pl.BlockSpec((d, tt), lambda i: (0, i))],
        out_specs=[ospec] * 4,
        out_shape=[jax.ShapeDtypeStruct(shape, F32)] * 2 + [jax.ShapeDtypeStruct(shape, BF16)] * 2,
        scratch_shapes=[pltpu.VMEM((2, nrow, tt), F32)],
        compiler_params=_params("parallel"),
        name="peer_select",
    )(wf, ht)


def _gelu_tanh(x):
    return 0.5 * x * (1.0 + jnp.tanh(math.sqrt(2.0 / math.pi) * (x + 0.044715 * (x * x * x))))


def _peer_dense_kernel(u_ref, ht_ref, vt_ref, e1_ref, n1_ref, r2_ref, e2_ref, o_ref, act_ref, ga_ref):
    nk = PEER_NKEYS
    j = pl.program_id(1)
    eb = u_ref.shape[0]

    @pl.when(j == 0)
    def _():
        o_ref[...] = jnp.zeros_like(o_ref)

    act_ref[...] = jnp.dot(u_ref[...], ht_ref[...], preferred_element_type=F32)
    for ii in range(eb // nk):
        i1 = j * (eb // nk) + ii
        act = _gelu_tanh(act_ref[pl.ds(ii * nk, nk), :])
        gate = None
        for h in range(PEER_HEADS):
            n1 = n1_ref[h, pl.ds(i1, 1), :].astype(BF16)
            e1 = e1_ref[h, pl.ds(i1, 1), :].astype(BF16)
            g = jnp.where(r2_ref[h] < n1, e2_ref[h] * e1, jnp.zeros((), BF16))
            gate = g if gate is None else gate + g
        ga_ref[pl.ds(ii * nk, nk), :] = (gate.astype(F32) * act).astype(BF16)
    o_ref[...] += jnp.dot(vt_ref[...], ga_ref[...], preferred_element_type=F32)


def peer_dense(u, ht, vt, e1, n1, r2, e2, tt=512, eb=1024):
    d, s = ht.shape
    ne = u.shape[0]
    gspec = pl.BlockSpec((PEER_HEADS, PEER_NKEYS, tt), lambda i, j: (0, 0, i))
    return pl.pallas_call(
        _peer_dense_kernel,
        grid=(s // tt, ne // eb),
        in_specs=[
            pl.BlockSpec((eb, d), lambda i, j: (j, 0)),
            pl.BlockSpec((d, tt), lambda i, j: (0, i)),
            pl.BlockSpec((d, eb), lambda i, j: (0, j)),
            gspec, gspec, gspec, gspec,
        ],
        out_specs=pl.BlockSpec((d, tt), lambda i, j: (0, i)),
        out_shape=jax.ShapeDtypeStruct((d, s), F32),
        scratch_shapes=[pltpu.VMEM((eb, tt), F32), pltpu.VMEM((eb, tt), BF16)],
        compiler_params=_params("parallel", "arbitrary"),
        name="peer_dense",
    )(u, ht, vt, e1, n1, r2, e2)


def peer(x, ln_g, w_q, keys, u_bf, vt_bf, tm=512, tt=256):
    wf = peer_fold(w_q, keys).reshape(2, PEER_HEADS * PEER_NKEYS, D_MODEL).astype(BF16)
    ht = rmsnorm_t(x, ln_g, tm=tm)
    return peer_dense(u_bf, ht, vt_bf, *peer_select(wf, ht, tt=tt), tt=tm)


def _mla_kernel(qi_ref, ki_ref, qn_ref, qr_ref, qrot_ref, cq_ref, sq_ref, kn_ref, kr_ref, krot_ref, ck_ref, sk_ref,
                v_ref, o_ref, q1_ref, q2_ref, m_ref, l_ref, acc_ref, *, scale):
    t = pl.program_id(1)
    qi, ki = qi_ref[t], ki_ref[t]
    tq, tk = qn_ref.shape[0], kn_ref.shape[0]
    ratio = tq // tk

    @pl.when(ki == 0)
    def _():
        q1_ref[...] = (qn_ref[...] * scale).astype(BF16)
        q2_ref[...] = ((qr_ref[0] * cq_ref[...] + qrot_ref[0] * sq_ref[...]) * scale).astype(BF16)
        m_ref[...] = jnp.full_like(m_ref, -jnp.inf)
        l_ref[...] = jnp.zeros_like(l_ref)
        acc_ref[...] = jnp.zeros_like(acc_ref)

    def step(masked):
        kr = kr_ref[...] * ck_ref[...] + krot_ref[...] * sk_ref[...]
        s = _bdot_nt(q1_ref[...], kn_ref[...]) + _bdot_nt(q2_ref[...], kr)
        if masked:
            qpos = qi * tq + lax.broadcasted_iota(jnp.int32, (tq, tk), 0)
            kpos = ki * tk + lax.broadcasted_iota(jnp.int32, (tq, tk), 1)
            s = jnp.where(kpos <= qpos, s, -jnp.inf)
        m_old = m_ref[...]
        m_new = jnp.maximum(m_old, jnp.max(s, axis=-1, keepdims=True))
        alpha = jnp.exp(m_old - m_new)
        p = jnp.exp(s - m_new)
        l_ref[...] = alpha * l_ref[...] + jnp.sum(p, axis=-1, keepdims=True)
        acc_ref[...] = alpha * acc_ref[...] + _bdot(p, v_ref[...])
        m_ref[...] = m_new

    pl.when(ki < qi * ratio)(functools.partial(step, False))
    pl.when(ki >= qi * ratio)(functools.partial(step, True))

    @pl.when(ki == (qi + 1) * ratio - 1)
    def _():
        o_ref[...] = acc_ref[...] / l_ref[...]


def mla_attention(qfull, kv, qr, qrot, kr, krot, cos, sin, tq=1024, tk=512):
    s = qfull.shape[0]
    tq, tk = min(tq, s), min(tk, s)
    ratio = tq // tk
    hh, dn, dr = MLA_HEADS, MLA_NOPE, MLA_ROPE
    pairs = [(qi, ki) for qi in range(s // tq) for ki in range((qi + 1) * ratio)]
    qi_tab = jnp.array([pr[0] for pr in pairs], jnp.int32)
    ki_tab = jnp.array([pr[1] for pr in pairs], jnp.int32)
    qmap = lambda h, t, qt, kt: (qt[t], 0)
    kmap = lambda h, t, qt, kt: (kt[t], 0)
    grid_spec = pltpu.PrefetchScalarGridSpec(
        num_scalar_prefetch=2,
        grid=(hh, len(pairs)),
        in_specs=[
            pl.BlockSpec((tq, dn), lambda h, t, qt, kt: (qt[t], h)),
            pl.BlockSpec((1, tq, dr), lambda h, t, qt, kt: (h, qt[t], 0)),
            pl.BlockSpec((1, tq, dr), lambda h, t, qt, kt: (h, qt[t], 0)),
            pl.BlockSpec((tq, dr), qmap),
            pl.BlockSpec((tq, dr), qmap),
            pl.BlockSpec((tk, dn), lambda h, t, qt, kt: (kt[t], h)),
            pl.BlockSpec((tk, dr), kmap),
            pl.BlockSpec((tk, dr), kmap),
            pl.BlockSpec((tk, dr), kmap),
            pl.BlockSpec((tk, dr), kmap),
            pl.BlockSpec((tk, MLA_V), lambda h, t, qt, kt: (kt[t], hh + h)),
        ],
        out_specs=pl.BlockSpec((tq, MLA_V), lambda h, t, qt, kt: (qt[t], h)),
        scratch_shapes=[pltpu.VMEM((tq, dn), BF16), pltpu.VMEM((tq, dr), BF16), pltpu.VMEM((tq, 1), F32),
                        pltpu.VMEM((tq, 1), F32), pltpu.VMEM((tq, MLA_V), F32)],
    )
    return pl.pallas_call(
        functools.partial(_mla_kernel, scale=(MLA_NOPE + MLA_ROPE) ** -0.5),
        grid_spec=grid_spec,
        out_shape=jax.ShapeDtypeStruct((s, hh * MLA_V), F32),
        compiler_params=_params("parallel", "arbitrary"),
        name="mla_attention",
    )(qi_tab, ki_tab, qfull, qr, qrot, cos, sin, kv, kr, krot, cos, sin, kv)


def _causal_conv(cur_ref, prev_ref, w_ref, first, bias=None):
    prev = jnp.where(first, 0.0, prev_ref[...])
    xe = jnp.concatenate([prev, cur_ref[...]], axis=0)
    w = w_ref[...]
    acc = w[CONV_WIDTH - 1:CONV_WIDTH] * xe[SUBLANES:]
    for j in range(CONV_WIDTH - 1):
        acc = acc + w[j:j + 1] * pltpu.roll(xe, CONV_WIDTH - 1 - j, 0)[SUBLANES:]
    return acc if bias is None else acc + bias


def _unit_lower_inverse(low, c):
    n = low.shape[0]
    r = lax.broadcasted_iota(jnp.int32, (n, n), 0)
    q = lax.broadcasted_iota(jnp.int32, (n, n), 1)
    eye = (r == q).astype(F32)
    d16 = jnp.where((r // 16) == (q // 16), low, 0.0)
    p = -d16
    x = eye + p
    for _ in range(3):
        p = _bdot(p, p)
        x = x + _bdot(x, p)
    prev = d16
    size = 32
    while size <= c:
        cur = jnp.where((r // size) == (q // size), low, 0.0) if size < c else low
        x = x - _bdot(_bdot(x, cur - prev), x)
        prev = cur
        size *= 2
    return x


def _gdn_kernel(q_ref, qp_ref, k_ref, kp_ref, v_ref, vp_ref, z_ref, wq_ref, wk_ref, wv_ref,
                ac_ref, ar_ref, bc_ref, alog_ref, dtb_ref, ng_ref, o_ref, st_ref):
    c = GDN_CHUNK
    tt = q_ref.shape[0]
    first = pl.program_id(1) == 0

    @pl.when(first)
    def _():
        st_ref[...] = jnp.zeros_like(st_ref)

    def l2n(x):
        return x * lax.rsqrt(jnp.sum(x * x, axis=-1, keepdims=True) + 1e-6)

    q = l2n(_silu(_causal_conv(q_ref, qp_ref, wq_ref, first))) * (GDN_DK ** -0.5)
    k = l2n(_silu(_causal_conv(k_ref, kp_ref, wk_ref, first)))
    v = _silu(_causal_conv(v_ref, vp_ref, wv_ref, first))
    neg_a = -jnp.exp(alog_ref[0, :, 0:1])
    dtb = dtb_ref[0, :, 0:1]
    nb = 2 * c
    r = lax.broadcasted_iota(jnp.int32, (nb, nb), 0)
    cc = lax.broadcasted_iota(jnp.int32, (nb, nb), 1)
    same = (r // c) == (cc // c)
    incl, strict = same & (r >= cc), same & (r > cc)
    incl_t = same & (r <= cc)
    top = lax.broadcasted_iota(jnp.int32, (nb, 1), 0) < c
    state = st_ref[...]
    outs = []
    for n in range(tt // nb):
        sl = slice(n * nb, (n + 1) * nb)
        g_col = neg_a * _softplus(ac_ref[0, n] + dtb)
        g_row = neg_a * _softplus(ar_ref[0, n] + dtb)
        beta = _sigmoid(bc_ref[0, n])
        gc = jnp.sum(jnp.where(incl, g_row, 0.0), axis=1, keepdims=True)
        gr = jnp.sum(jnp.where(incl_t, g_col, 0.0), axis=0, keepdims=True)
        decay = jnp.where(incl, jnp.exp(jnp.where(incl, gc - gr, 0.0)), 0.0)
        qb, kb, vb = q[sl], k[sl], v[sl]
        low = jnp.where(strict, beta * _bdot_nt(kb, kb) * decay, 0.0)
        eg = jnp.exp(gc)
        rhs = jnp.concatenate([beta * vb, (beta * eg) * kb], axis=1)
        sol = _bdot(_unit_lower_inverse(low, c), rhs)
        w1, w2 = sol[:, :GDN_DV], sol[:, GDN_DV:]
        a_qk = _bdot_nt(qb, kb) * decay
        g_last = jnp.where(top, gc[c - 1:c], gc[nb - 1:nb])
        k_end = kb * jnp.exp(g_last - gc)
        q_dec = qb * eg
        us, o_state = [], []
        for j in range(2):
            cs = slice(j * c, (j + 1) * c)
            u = w1[cs] - _bdot(w2[cs], state)
            o_state.append(_bdot(q_dec[cs], state))
            state = jnp.exp(gc[(j + 1) * c - 1:(j + 1) * c]) * state + _bdot(k_end[cs].T, u)
            us.append(u)
        outs.append(jnp.concatenate(o_state, axis=0) + _bdot(a_qk, jnp.concatenate(us, axis=0)))
    st_ref[...] = state
    o = jnp.concatenate(outs, axis=0)
    o_ref[...] = _rms(o, ng_ref[...]) * _silu(z_ref[...])


def gated_delta_net(proj, col0, a_raw, b_raw, conv_w, a_log, dt_bias, norm_g, tt=256):
    s = proj.shape[0]
    tt = min(tt, s)
    hh, c = GDN_HEADS, GDN_CHUNK
    b0 = col0 // LANES
    nblk = GDN_HEADS * GDN_DK // LANES

    def cur(g):
        return pl.BlockSpec((tt, LANES), lambda h, i: (i, b0 + g * nblk + h))

    def prev(g):
        return pl.BlockSpec((SUBLANES, LANES), lambda h, i: (jnp.maximum(i * (tt // SUBLANES) - 1, 0), b0 + g * nblk + h))

    def wspec(g):
        return pl.BlockSpec((CONV_WIDTH, LANES), lambda h, i: (0, g * nblk + h))

    a_t, b_t = a_raw.T, b_raw.T
    c = 2 * c
    colspec = pl.BlockSpec((1, tt // c, c, 1), lambda h, i: (h, i, 0, 0))
    rowspec = pl.BlockSpec((1, tt // c, 1, c), lambda h, i: (h, i, 0, 0))
    hspec = pl.BlockSpec((1, 1, LANES), lambda h, i: (h, 0, 0))
    bcast = lambda p: jnp.broadcast_to(p.astype(F32)[:, None, None], (hh, 1, LANES))
    return pl.pallas_call(
        _gdn_kernel,
        grid=(hh, s // tt),
        in_specs=[cur(0), prev(0), cur(1), prev(1), cur(2), prev(2), cur(3), wspec(0), wspec(1), wspec(2),
                  colspec, rowspec, colspec, hspec, hspec, pl.BlockSpec((1, GDN_DV), lambda h, i: (0, 0))],
        out_specs=pl.BlockSpec((tt, GDN_DV), lambda h, i: (i, h)),
        out_shape=jax.ShapeDtypeStruct((s, hh * GDN_DV), F32),
        scratch_shapes=[pltpu.VMEM((GDN_DK, GDN_DV), F32)],
        compiler_params=_params("parallel", "arbitrary"),
        name="gated_delta_net",
    )(proj, proj, proj, proj, proj, proj, proj, conv_w, conv_w, conv_w,
      a_t.reshape(hh, s // c, c, 1), a_t.reshape(hh, s // c, 1, c), b_t.reshape(hh, s // c, c, 1),
      bcast(a_log), bcast(dt_bias), norm_g.reshape(1, GDN_DV).astype(F32))


def _rot_half_cols(w, half):
    return jnp.concatenate([-w[..., half:], w[..., :half]], axis=-1)


def _pad_cols(w, n):
    return jnp.pad(w, ((0, 0), (0, n - w.shape[1])))


AB_QKVZ = MLA_Q_RANK + MLA_KV_RANK + 2 * MLA_ROPE
AB_GATES = AB_QKVZ + 4 * GDN_HEADS * GDN_DK
AB_PAD = AB_GATES + LANES


def prep_ab(w_in, w_uq, w_ukv, w_out):
    rq, rkv, rr = MLA_Q_RANK, MLA_KV_RANK, MLA_ROPE
    w_kr = w_in[:, rq + rkv:rq + rkv + rr]
    w_main = jnp.concatenate([w_in[:, :rq + rkv], w_kr, _rot_half_cols(w_kr, rr // 2),
                              w_in[:, rq + rkv + rr:]], axis=1)
    w_main = _pad_cols(w_main, AB_PAD).astype(BF16)
    uq = w_uq.reshape(rq, MLA_HEADS, MLA_NOPE + MLA_ROPE)
    uq_r = uq[..., MLA_NOPE:]
    uq2 = jnp.concatenate([uq[..., :MLA_NOPE].reshape(rq, -1), uq_r.reshape(rq, -1),
                           _rot_half_cols(uq_r, rr // 2).reshape(rq, -1)], axis=1).astype(BF16)
    ukv = w_ukv.reshape(rkv, MLA_HEADS, MLA_NOPE + MLA_V)
    ukv2 = jnp.concatenate([ukv[..., :MLA_NOPE].reshape(rkv, -1), ukv[..., MLA_NOPE:].reshape(rkv, -1)], axis=1).astype(BF16)
    return w_main, uq2, ukv2, w_out.astype(BF16)


def mixer_ab(x, ln, w_main, uq2, ukv2, w_out, q_norm, kv_norm, conv_w, a_log, dt_bias, gdn_norm, cos2, sin2,
             tm=512, t_attn=512, t_gdn=256):
    s = x.shape[0]
    rq, rkv, rr = MLA_Q_RANK, MLA_KV_RANK, MLA_ROPE
    proj = norm_matmul(x, w_main, gain=ln, tm=tm)
    qfull = norm_matmul(proj[:, :rq], uq2, gain=q_norm, tm=tm)
    kv = norm_matmul(proj[:, rq:rq + rkv], ukv2, gain=kv_norm, tm=tm, out_dtype=BF16)
    nn = MLA_HEADS * MLA_NOPE
    heads = lambda a: a.reshape(s, MLA_HEADS, rr).transpose(1, 0, 2)
    o_a = mla_attention(qfull, kv, heads(qfull[:, nn:nn + MLA_HEADS * rr]), heads(qfull[:, nn + MLA_HEADS * rr:]),
                        proj[:, rq + rkv:rq + rkv + rr], proj[:, rq + rkv + rr:AB_QKVZ], cos2, sin2, tq=2 * t_attn, tk=t_attn)
    o_b = gated_delta_net(proj, AB_QKVZ, proj[:, AB_GATES:AB_GATES + GDN_HEADS],
                          proj[:, AB_GATES + GDN_HEADS:AB_GATES + 2 * GDN_HEADS], conv_w, a_log, dt_bias, gdn_norm, tt=t_gdn)
    return norm_matmul(jnp.concatenate([o_a, o_b], axis=1), w_out, residual=x, tm=tm)


CD_R, CD_K, CD_V, CD_Z, CD_DT, CD_X = (i * 1024 for i in range(6))
CD_B = 6144
CD_C = CD_B + SSD_GROUPS * SSD_STATE
CD_WA = CD_C + SSD_GROUPS * SSD_STATE
CD_G = CD_WA + LANES
CD_PAD = 7168
GROUP_W = SSD_INNER // SSD_GROUPS


def _lower_ones(c):
    r = lax.broadcasted_iota(jnp.int32, (c, c), 0)
    q = lax.broadcasted_iota(jnp.int32, (c, c), 1)
    return r >= q, r > q


def _ssd_kernel(x_ref, xp_ref, b_ref, bp_ref, c_ref, cp_ref, z_ref, dt_ref, wx_ref, wb_ref, wc_ref,
                bx_ref, bb_ref, bc_ref, dtb_ref, alog_ref, dskip_ref, ng_ref, o_ref, st_ref):
    c = SSD_CHUNK
    tt = x_ref.shape[0]
    first = pl.program_id(1) == 0

    @pl.when(first)
    def _():
        st_ref[...] = jnp.zeros_like(st_ref)

    xs_all = _silu(_causal_conv(x_ref, xp_ref, wx_ref, first, bx_ref[...]))
    bm_all = _silu(_causal_conv(b_ref, bp_ref, wb_ref, first, bb_ref[...]))
    cm_all = _silu(_causal_conv(c_ref, cp_ref, wc_ref, first, bc_ref[...]))
    dt_all = _softplus(dt_ref[...] + dtb_ref[...])
    a_all = -jnp.exp(alog_ref[...]) * dt_all
    incl, _ = _lower_ones(c)
    tri = incl.astype(F32)
    left = lax.broadcasted_iota(jnp.int32, (c, LANES), 1) < SSD_HEADDIM
    npair = GROUP_W // LANES
    outs = []
    for n in range(tt // c):
        sl = slice(n * c, (n + 1) * c)
        xs, bm, cm, dt = xs_all[sl], bm_all[sl], cm_all[sl], dt_all[sl]
        acs = _bdot(tri, a_all[sl])
        xdt = xs * dt
        cb = _bdot_nt(cm, bm)
        bm_t = bm.T
        ys = []
        for p in range(npair):
            ls = slice(p * LANES, (p + 1) * LANES)
            acs_p = acs[:, ls]
            acs_t = acs_p.T
            xp = xdt[:, ls]
            yd = []
            for hd in range(2):
                col = acs_p[:, hd * SSD_HEADDIM:hd * SSD_HEADDIM + 1]
                row = acs_t[hd * SSD_HEADDIM:hd * SSD_HEADDIM + 1, :]
                lmat = jnp.where(incl, jnp.exp(jnp.where(incl, col - row, 0.0)), 0.0)
                yd.append(_bdot(cb * lmat, xp))
            last = acs_p[c - 1:c]
            prev_t = st_ref[p]
            y_off = _bdot(cm, prev_t) * jnp.exp(acs_p)
            st_ref[p] = jnp.exp(last) * prev_t + _bdot(bm_t, xp * jnp.exp(last - acs_p))
            ys.append(jnp.where(left, yd[0], yd[1]) + y_off)
        outs.append(jnp.concatenate(ys, axis=1) + xs * dskip_ref[...])
    y = jnp.concatenate(outs, axis=0) * _silu(z_ref[...])
    o_ref[...] = _rms(y, ng_ref[...])


def mamba2_ssd(proj, conv_w, conv_b, dt_bias, a_log, d_skip, norm_g, tt=256):
    s = proj.shape[0]
    tt = min(tt, s)
    gw, ns = GROUP_W, SSD_STATE
    per = lambda v: jnp.repeat(v.astype(F32), SSD_HEADDIM).reshape(1, SSD_INNER)

    def cur(col, w):
        return pl.BlockSpec((tt, w), lambda g, i: (i, col // w + g))

    def prev(col, w):
        return pl.BlockSpec((SUBLANES, w), lambda g, i: (jnp.maximum(i * (tt // SUBLANES) - 1, 0), col // w + g))

    def par(rows, col, w):
        return pl.BlockSpec((rows, w), lambda g, i: (0, col // w + g))

    cb = conv_b.reshape(1, -1).astype(F32)
    return pl.pallas_call(
        _ssd_kernel,
        grid=(SSD_GROUPS, s // tt),
        in_specs=[cur(CD_X, gw), prev(CD_X, gw), cur(CD_B, ns), prev(CD_B, ns), cur(CD_C, ns), prev(CD_C, ns),
                  cur(CD_Z, gw), cur(CD_DT, gw),
                  par(CONV_WIDTH, 0, gw), par(CONV_WIDTH, SSD_INNER, ns), par(CONV_WIDTH, SSD_INNER + SSD_GROUPS * ns, ns),
                  par(1, 0, gw), par(1, SSD_INNER, ns), par(1, SSD_INNER + SSD_GROUPS * ns, ns),
                  par(1, 0, gw), par(1, 0, gw), par(1, 0, gw), par(1, 0, gw)],
        out_specs=pl.BlockSpec((tt, gw), lambda g, i: (i, g)),
        out_shape=jax.ShapeDtypeStruct((s, SSD_INNER), F32),
        scratch_shapes=[pltpu.VMEM((gw // LANES, ns, LANES), F32)],
        compiler_params=_params("parallel", "arbitrary"),
        name="mamba2_ssd",
    )(proj, proj, proj, proj, proj, proj, proj, proj, conv_w, conv_w, conv_w, cb, cb, cb,
      per(dt_bias), per(a_log), per(d_skip), norm_g.reshape(1, SSD_INNER).astype(F32))


def _pair_ones():
    r = lax.broadcasted_iota(jnp.int32, (LANES, LANES), 0)
    q = lax.broadcasted_iota(jnp.int32, (LANES, LANES), 1)
    return (r // RWKV_HEAD) == (q // RWKV_HEAD)


def _head_sums(x, ones):
    return jnp.concatenate([_hdot(x[:, i * LANES:(i + 1) * LANES], ones) for i in range(x.shape[1] // LANES)], axis=1)


def _rwkv_prep_kernel(r_ref, rp_ref, k_ref, kp_ref, v_ref, vp_ref, wa_ref, wap_ref, g0_ref, g0p_ref, g1_ref, g1p_ref,
                      mur_ref, muk_ref, muv_ref, muwa_ref, mug0_ref, mug1_ref, w0_ref, w2_ref, a0_ref, a2_ref, g2_ref,
                      kk_ref, ka_ref, rk_ref,
                      ro_ref, lw_ref, ko_ref, vo_ref, po_ref, qo_ref, go_ref, bo_ref):
    first = pl.program_id(0) == 0

    def mix(cur_ref, prev_ref, mu_ref):
        cur = cur_ref[...]
        prev = jnp.where(first, 0.0, prev_ref[...])
        shifted = pltpu.roll(jnp.concatenate([prev, cur], axis=0), 1, 0)[SUBLANES:]
        return cur + (shifted - cur) * mu_ref[...]

    r = mix(r_ref, rp_ref, mur_ref)
    k = mix(k_ref, kp_ref, muk_ref)
    v = mix(v_ref, vp_ref, muv_ref)
    wa = mix(wa_ref, wap_ref, muwa_ref)
    g0 = mix(g0_ref, g0p_ref, mug0_ref)
    g1 = mix(g1_ref, g1p_ref, mug1_ref)
    log_w = -math.exp(-0.5) * _sigmoid(w0_ref[...] + _bdot(jnp.tanh(wa), w2_ref[...]))
    a = _sigmoid(a0_ref[...] + _bdot(wa, a2_ref[...]))
    gate = _bdot(_sigmoid(g0), g2_ref[0:LANES, :]) + _bdot(_sigmoid(g1), g2_ref[LANES:2 * LANES, :])
    ones = _pair_ones().astype(F32)
    kx = k * kk_ref[...]
    kk = kx * lax.rsqrt(_head_sums(kx * kx, ones) + 1e-6)
    k_mod = k * (1.0 + (a - 1.0) * ka_ref[...])
    ro_ref[...] = r
    lw_ref[...] = log_w
    ko_ref[...] = k_mod
    vo_ref[...] = v
    po_ref[...] = -kk * a
    qo_ref[...] = kk
    go_ref[...] = gate
    bo_ref[...] = _head_sums(r * k_mod * rk_ref[...], ones) * v


def rwkv_prep(proj, mu, w0, w2, a0, a2, g2, k_k, k_a, r_k, tt=256):
    s = proj.shape[0]
    tt = min(tt, s)
    ri = RWKV_INNER
    row = lambda v: v.reshape(1, -1).astype(F32)
    mu_r, mu_k, mu_v = (row(mu[i * ri:(i + 1) * ri]) for i in range(3))
    mu_wa = row(mu[3 * ri:3 * ri + LANES])
    mu_g = row(jnp.pad(mu[3 * ri + LANES:], (0, 2 * LANES - RWKV_G_LORA)))
    zeros = jnp.zeros((RWKV_W_LORA, ri), F32)
    w2p = jnp.concatenate([w2, zeros], axis=0).astype(BF16)
    a2p = jnp.concatenate([zeros, a2], axis=0).astype(BF16)
    g2p = jnp.pad(g2, ((0, 2 * LANES - RWKV_G_LORA), (0, 0))).astype(BF16)

    def cur(col, w):
        return pl.BlockSpec((tt, w), lambda i: (i, col // w))

    def prev(col, w):
        return pl.BlockSpec((SUBLANES, w), lambda i: (jnp.maximum(i * (tt // SUBLANES) - 1, 0), col // w))

    full = lambda a: pl.BlockSpec(a.shape, lambda i: (0, 0))
    params = [mu_r, mu_k, mu_v, mu_wa, mu_g[:, :LANES], mu_g[:, LANES:], row(w0), w2p, row(a0), a2p, g2p,
              row(k_k), row(k_a), row(r_k)]
    out = jax.ShapeDtypeStruct((s, ri), F32)
    return pl.pallas_call(
        _rwkv_prep_kernel,
        grid=(s // tt,),
        in_specs=[cur(CD_R, ri), prev(CD_R, ri), cur(CD_K, ri), prev(CD_K, ri), cur(CD_V, ri), prev(CD_V, ri),
                  cur(CD_WA, LANES), prev(CD_WA, LANES), cur(CD_G, LANES), prev(CD_G, LANES),
                  cur(CD_G + LANES, LANES), prev(CD_G + LANES, LANES)] + [full(a) for a in params],
        out_specs=[pl.BlockSpec((tt, ri), lambda i: (i, 0))] * 8,
        out_shape=[out] * 8,
        compiler_params=_params("arbitrary"),
        name="rwkv_prep",
    )(*([proj] * 12), *params)


RWKV_MY_CHUNK = 64


def _rwkv_scan_kernel(r_ref, lw_ref, k_ref, v_ref, p_ref, q_ref, g_ref, b_ref, lnw_ref, lnb_ref, o_ref, st_ref):
    c = RWKV_MY_CHUNK
    tt = r_ref.shape[0]

    @pl.when(pl.program_id(1) == 0)
    def _():
        st_ref[...] = jnp.zeros_like(st_ref)

    tri = _lower_ones(c)[0].astype(F32)
    left = lax.broadcasted_iota(jnp.int32, (c, LANES), 1) < RWKV_HEAD
    pair = _pair_ones()
    ones = pair.astype(F32)
    r_id = lax.broadcasted_iota(jnp.int32, (LANES, LANES), 0)
    c_id = lax.broadcasted_iota(jnp.int32, (LANES, LANES), 1)
    eye = r_id == c_id
    top = r_id < c
    strict = pair & ((r_id % c) > (c_id % c))
    incl = pair & ((r_id % c) >= (c_id % c))
    stack2 = lambda a: jnp.concatenate([a, a], axis=0)
    by_head = lambda a: jnp.concatenate([jnp.where(left, a, 0.0), jnp.where(left, 0.0, a)], axis=0)
    unstack = lambda a: jnp.where(left, a[:c], a[c:])
    state = st_ref[...]
    outs = []
    for n in range(tt // c):
        sl = slice(n * c, (n + 1) * c)
        r, w, k, v, p, q = (ref[sl, :] for ref in (r_ref, lw_ref, k_ref, v_ref, p_ref, q_ref))
        lw = _bdot(tri, w)
        lam_in, inv_lam = jnp.exp(lw), jnp.exp(-lw)
        q_bar, r_bar, p_t, k_t = q * jnp.exp(lw - w), r * lam_in, p * inv_lam, k * inv_lam
        pk = jnp.concatenate([p_t, k_t], axis=0)
        mq = _bdot_nt(by_head(q_bar), pk)
        mr = _bdot_nt(by_head(r_bar), pk)
        mq_sw, mr_sw = pltpu.roll(mq, c, 1), pltpu.roll(mr, c, 1)
        m_qp = jnp.where(strict, jnp.where(top, mq, mq_sw), 0.0)
        m_qk = jnp.where(strict, jnp.where(top, mq_sw, mq), 0.0)
        m_rp = jnp.where(incl, jnp.where(top, mr, mr_sw), 0.0)
        m_rk = jnp.where(incl, jnp.where(top, mr_sw, mr), 0.0)
        vv = stack2(v)
        sol = _bdot(_unit_lower_inverse(-m_qp, c), jnp.concatenate([stack2(q_bar), _bdot(m_qk, vv)], axis=1))
        ws, wv = unstack(sol[:, :LANES]), unstack(sol[:, LANES:])
        y_loc = unstack(_bdot(m_rk, vv))
        lam_end = lam_in[c - 1:c]
        lam_col = jnp.sum(jnp.where(eye, lam_end, 0.0), axis=1, keepdims=True)
        pk_end_t = (pk * lam_end).T
        u = _bdot(ws, state) + wv
        outs.append(_bdot(r_bar, state) + unstack(_bdot(m_rp, stack2(u))) + y_loc)
        upd = _bdot(pk_end_t, jnp.concatenate([u, v], axis=0))
        state = lam_col * state + jnp.where(pair, upd, 0.0)
    st_ref[...] = state
    y = jnp.concatenate(outs, axis=0)
    mean = _hdot(y, ones) * (1.0 / RWKV_HEAD)
    yc = y - mean
    var = _hdot(yc * yc, ones) * (1.0 / RWKV_HEAD)
    y = yc * lax.rsqrt(var + RWKV_GN_EPS) * lnw_ref[...] + lnb_ref[...]
    o_ref[...] = (y + b_ref[...]) * g_ref[...]


def rwkv_scan(r, lw, k, v, p, q, gate, bonus, ln_w, ln_b, tt=256):
    s = r.shape[0]
    tt = min(tt, s)
    spec = pl.BlockSpec((tt, LANES), lambda h, i: (i, h))
    pspec = pl.BlockSpec((1, LANES), lambda h, i: (0, h))
    return pl.pallas_call(
        _rwkv_scan_kernel,
        grid=(RWKV_INNER // LANES, s // tt),
        in_specs=[spec] * 8 + [pspec] * 2,
        out_specs=spec,
        out_shape=jax.ShapeDtypeStruct((s, RWKV_INNER), F32),
        scratch_shapes=[pltpu.VMEM((LANES, LANES), F32)],
        compiler_params=_params("parallel", "arbitrary"),
        name="rwkv_scan",
    )(r, lw, k, v, p, q, gate, bonus, ln_w.reshape(1, -1).astype(F32), ln_b.reshape(1, -1).astype(F32))


def prep_cd(w_in, w_out):
    si, ri = SSD_INNER, RWKV_INNER
    z, xbc, dt, rw = w_in[:, :si], w_in[:, si:2 * si + 512], w_in[:, 2 * si + 512:2 * si + 528], w_in[:, 2 * si + 528:]
    dt_exp = jnp.repeat(dt, SSD_HEADDIM, axis=1)
    cols = [rw[:, :3 * ri], z, dt_exp, xbc, rw[:, 3 * ri:]]
    return _pad_cols(jnp.concatenate(cols, axis=1), CD_PAD).astype(BF16), w_out.astype(BF16)


def mixer_cd(x, ln, w_main, w_out, ssd_conv_w, ssd_conv_b, ssd_dt_bias, ssd_a_log, ssd_d, ssd_norm,
             mu, w0, w2, a0, a2, g2, k_k, k_a, r_k, ln_w, ln_b, tm=512, tt=256):
    proj = norm_matmul(x, w_main, gain=ln, tm=tm)
    o_c = mamba2_ssd(proj, ssd_conv_w, ssd_conv_b, ssd_dt_bias, ssd_a_log, ssd_d, ssd_norm, tt=tt)
    o_d = rwkv_scan(*rwkv_prep(proj, mu, w0, w2, a0, a2, g2, k_k, k_a, r_k.reshape(-1), tt=tt), ln_w, ln_b, tt=tt)
    return norm_matmul(jnp.concatenate([o_c, o_d], axis=1), w_out, residual=x, tm=tm)


def _rope_tables(s):
    inv = 1.0 / (ROPE_THETA ** (jnp.arange(0, MLA_ROPE, 2, dtype=F32) / MLA_ROPE))
    ang = jnp.arange(s, dtype=F32)[:, None] * inv[None, :]
    cos, sin = jnp.cos(ang), jnp.sin(ang)
    return jnp.concatenate([cos, cos], axis=1), jnp.concatenate([sin, sin], axis=1)


def kernel(x, p, ln_mix, ln_ffn, ab_w_in, mla_q_norm, mla_w_uq, mla_kv_norm, mla_w_ukv, gdn_conv_w, gdn_a_log, gdn_dt_bias, gdn_norm, ab_w_out, cd_w_in, ssd_conv_w, ssd_conv_b, ssd_dt_bias, ssd_a_log, ssd_d, ssd_norm, rwkv_mu, rwkv_w0, rwkv_w2, rwkv_a0, rwkv_a2, rwkv_g2, rwkv_k_k, rwkv_k_a, rwkv_r_k, rwkv_ln_w, rwkv_ln_b, cd_w_out, peer_w_q, peer_keys, peer_u, peer_v, ple_w_proj, ple_norm, ple_w_gate, final_norm):
    assert x.shape[0] == 1
    s = x.shape[1]
    tm = min(512, s)
    tt = min(256, s)
    cos2, sin2 = _rope_tables(s)
    xs = x[0]
    for i in range(DEPTH):
        j = i // 2
        if i % 2 == 0:
            wts = prep_ab(ab_w_in[j], mla_w_uq[j], mla_w_ukv[j], ab_w_out[j])
            xs = mixer_ab(xs, ln_mix[i], *wts, mla_q_norm[j], mla_kv_norm[j], gdn_conv_w[j], gdn_a_log[j],
                          gdn_dt_bias[j], gdn_norm[j], cos2, sin2, tm=tm, t_attn=tm, t_gdn=tt)
        else:
            wts = prep_cd(cd_w_in[j], cd_w_out[j])
            xs = mixer_cd(xs, ln_mix[i], *wts, ssd_conv_w[j], ssd_conv_b[j], ssd_dt_bias[j], ssd_a_log[j], ssd_d[j],
                          ssd_norm[j], rwkv_mu[j], rwkv_w0[j], rwkv_w2[j], rwkv_a0[j], rwkv_a2[j], rwkv_g2[j],
                          rwkv_k_k[j], rwkv_k_a[j], rwkv_r_k[j], rwkv_ln_w[j], rwkv_ln_b[j], tm=tm, tt=tt)
        xs = xs + peer(xs, ln_ffn[i], peer_w_q[i], peer_keys[i], peer_u[i].astype(BF16), peer_v[i].T.astype(BF16), tm=tm, tt=tt).T
        xs = ple_update(xs, p[i, 0], ple_norm[i], ple_w_gate[i].astype(BF16), ple_w_proj[i].astype(BF16), tm=tm)
    return rmsnorm(xs, final_norm, tm=tm)[None]
```

```python
import functools
import math

import jax
import jax.numpy as jnp
from jax import lax
from jax.experimental import pallas as pl
from jax.experimental.pallas import tpu as pltpu

F32 = jnp.float32
BF16 = jnp.bfloat16
HIGHEST = lax.Precision.HIGHEST

D_MODEL = 2048
DEPTH = 4
PLE_DIM = 256
RMS_EPS = 1e-6
MLA_HEADS = 8
MLA_Q_RANK = 512
MLA_KV_RANK = 256
MLA_NOPE = 128
MLA_ROPE = 64
MLA_V = 128
ROPE_THETA = 10000.0
GDN_HEADS = 8
GDN_DK = 128
GDN_DV = 128
GDN_CHUNK = 64
SSD_HEADS = 16
SSD_HEADDIM = 64
SSD_GROUPS = 2
SSD_STATE = 128
SSD_CHUNK = 128
SSD_INNER = SSD_HEADS * SSD_HEADDIM
RWKV_HEADS = 16
RWKV_HEAD = 64
RWKV_INNER = RWKV_HEADS * RWKV_HEAD
RWKV_W_LORA = 64
RWKV_A_LORA = 64
RWKV_G_LORA = 160
RWKV_GN_EPS = 64e-5
CONV_WIDTH = 4
PEER_HEADS = 8
PEER_NKEYS = 128
PEER_EXPERTS = PEER_NKEYS * PEER_NKEYS
PEER_QDIM = 256
PEER_TOPK = 16

LANES = 128
SUBLANES = 8
VMEM_LIMIT = 56 * 1024 * 1024


def _params(*sem):
    return pltpu.CompilerParams(dimension_semantics=sem, vmem_limit_bytes=VMEM_LIMIT)


def _bdot(a, b):
    return jnp.dot(a.astype(BF16), b.astype(BF16), preferred_element_type=F32)


def _bdot_nt(a, b):
    return lax.dot_general(a.astype(BF16), b.astype(BF16), (((1,), (1,)), ((), ())), preferred_element_type=F32)


def _hdot(a, b):
    return jnp.dot(a, b, preferred_element_type=F32, precision=HIGHEST)


def _hdot_nt(a, b):
    return lax.dot_general(a, b, (((1,), (1,)), ((), ())), preferred_element_type=F32, precision=HIGHEST)


def _rms(x, gain):
    return x * lax.rsqrt(jnp.mean(x * x, axis=-1, keepdims=True) + RMS_EPS) * gain


def _sigmoid(x):
    return 1.0 / (1.0 + jnp.exp(-x))


def _silu(x):
    return x * _sigmoid(x)


def _softplus(x):
    return jnp.maximum(x, 0.0) + jnp.log(1.0 + jnp.exp(-jnp.abs(x)))


def _nm_kernel(*refs, has_norm, has_res):
    it = iter(refs)
    x_ref = next(it)
    g_ref = next(it) if has_norm else None
    w_ref = next(it)
    r_ref = next(it) if has_res else None
    o_ref = next(it)
    xn_ref = next(it)

    @pl.when(pl.program_id(1) == 0)
    def _():
        x = x_ref[...].astype(F32)
        if has_norm:
            x = _rms(x, g_ref[...])
        xn_ref[...] = x.astype(BF16)

    acc = jnp.dot(xn_ref[...], w_ref[...], preferred_element_type=F32)
    if has_res:
        acc = acc + r_ref[...]
    o_ref[...] = acc.astype(o_ref.dtype)


def norm_matmul(x, w, gain=None, residual=None, tm=512, tn=512, out_dtype=F32, x_col=0):
    m = x.shape[0]
    k, n = w.shape
    tn = min(tn, n)
    assert m % tm == 0 and n % tn == 0 and x_col % k == 0
    in_specs = [pl.BlockSpec((tm, k), lambda i, j: (i, x_col // k))]
    args = [x]
    if gain is not None:
        in_specs.append(pl.BlockSpec((1, k), lambda i, j: (0, 0)))
        args.append(gain.reshape(1, k).astype(F32))
    in_specs.append(pl.BlockSpec((k, tn), lambda i, j: (0, j)))
    args.append(w)
    if residual is not None:
        in_specs.append(pl.BlockSpec((tm, tn), lambda i, j: (i, j)))
        args.append(residual)
    return pl.pallas_call(
        functools.partial(_nm_kernel, has_norm=gain is not None, has_res=residual is not None),
        grid=(m // tm, n // tn),
        in_specs=in_specs,
        out_specs=pl.BlockSpec((tm, tn), lambda i, j: (i, j)),
        out_shape=jax.ShapeDtypeStruct((m, n), out_dtype),
        scratch_shapes=[pltpu.VMEM((tm, k), BF16)],
        compiler_params=_params("parallel", "arbitrary"),
        name="norm_matmul",
    )(*args)


def _out_proj_kernel(a_ref, b_ref, wa_ref, wb_ref, r_ref, o_ref):
    o_ref[...] = r_ref[...] + _bdot(a_ref[...], wa_ref[...]) + _bdot(b_ref[...], wb_ref[...])


def out_proj(a, b, w, residual, tm=512, tn=512):
    m, ka = a.shape
    n = w.shape[1]
    assert b.shape[1] == ka and w.shape[0] == 2 * ka
    return pl.pallas_call(
        _out_proj_kernel,
        grid=(m // tm, n // tn),
        in_specs=[
            pl.BlockSpec((tm, ka), lambda i, j: (i, 0)),
            pl.BlockSpec((tm, ka), lambda i, j: (i, 0)),
            pl.BlockSpec((ka, tn), lambda i, j: (0, j)),
            pl.BlockSpec((ka, tn), lambda i, j: (1, j)),
            pl.BlockSpec((tm, tn), lambda i, j: (i, j)),
        ],
        out_specs=pl.BlockSpec((tm, tn), lambda i, j: (i, j)),
        out_shape=jax.ShapeDtypeStruct((m, n), F32),
        compiler_params=_params("parallel", "arbitrary"),
        name="out_proj",
    )(a, b, w, w, residual)


def _ple_kernel(x_ref, yt_ref, g_ref, wg_ref, p_ref, wp_ref, o_ref, xn_ref, xs_ref):
    j = pl.program_id(1)
    tn = o_ref.shape[1]

    @pl.when(j == 0)
    def _():
        x = x_ref[...] + yt_ref[...].T
        xn_ref[...] = _rms(x, g_ref[...]).astype(BF16)
        for jj in range(xs_ref.shape[0]):
            xs_ref[jj] = x[:, jj * tn:(jj + 1) * tn]

    gate = _sigmoid(jnp.dot(xn_ref[...], wg_ref[...], preferred_element_type=F32))
    emb = _bdot(p_ref[...], wp_ref[...])
    o_ref[...] = xs_ref[j] + gate * emb


def ple_update(x, yt, p_i, norm_g, w_gate, w_proj, tm=512, tn=512):
    m, d = x.shape
    pd = p_i.shape[1]
    return pl.pallas_call(
        _ple_kernel,
        grid=(m // tm, d // tn),
        in_specs=[
            pl.BlockSpec((tm, d), lambda i, j: (i, 0)),
            pl.BlockSpec((d, tm), lambda i, j: (0, i)),
            pl.BlockSpec((1, d), lambda i, j: (0, 0)),
            pl.BlockSpec((d, tn), lambda i, j: (0, j)),
            pl.BlockSpec((tm, pd), lambda i, j: (i, 0)),
            pl.BlockSpec((pd, tn), lambda i, j: (0, j)),
        ],
        out_specs=pl.BlockSpec((tm, tn), lambda i, j: (i, j)),
        out_shape=jax.ShapeDtypeStruct((m, d), F32),
        scratch_shapes=[pltpu.VMEM((tm, d), BF16), pltpu.VMEM((d // tn, tm, tn), F32)],
        compiler_params=_params("parallel", "arbitrary"),
        name="ple_update",
    )(x, yt, norm_g.reshape(1, d), w_gate, p_i, w_proj)


def _rmsnorm_kernel(x_ref, g_ref, o_ref):
    o_ref[...] = _rms(x_ref[...], g_ref[...])


def rmsnorm(x, gain, tm=512):
    m, d = x.shape
    return pl.pallas_call(
        _rmsnorm_kernel,
        grid=(m // tm,),
        in_specs=[pl.BlockSpec((tm, d), lambda i: (i, 0)), pl.BlockSpec((1, d), lambda i: (0, 0))],
        out_specs=pl.BlockSpec((tm, d), lambda i: (i, 0)),
        out_shape=jax.ShapeDtypeStruct((m, d), F32),
        compiler_params=_params("parallel"),
        name="rmsnorm",
    )(x, gain.reshape(1, d))


def _peer_fold_kernel(keys_ref, wq_ref, o_ref):
    o_ref[0, 0] = _hdot_nt(keys_ref[0, 0], wq_ref[...])


def peer_fold(w_q, keys):
    hk = PEER_QDIM // 2
    return pl.pallas_call(
        _peer_fold_kernel,
        grid=(2, PEER_HEADS),
        in_specs=[
            pl.BlockSpec((1, 1, PEER_NKEYS, hk), lambda c, h: (h, c, 0, 0)),
            pl.BlockSpec((D_MODEL, hk), lambda c, h: (0, h * 2 + c)),
        ],
        out_specs=pl.BlockSpec((1, 1, PEER_NKEYS, D_MODEL), lambda c, h: (c, h, 0, 0)),
        out_shape=jax.ShapeDtypeStruct((2, PEER_HEADS, PEER_NKEYS, D_MODEL), F32),
        compiler_params=_params("parallel", "parallel"),
        name="peer_fold",
    )(keys, w_q)


def _rmsnorm_t_kernel(x_ref, g_ref, o_ref):
    o_ref[...] = _rms(x_ref[...], g_ref[...]).T.astype(BF16)


def rmsnorm_t(x, gain, tm=512):
    m, d = x.shape
    return pl.pallas_call(
        _rmsnorm_t_kernel,
        grid=(m // tm,),
        in_specs=[pl.BlockSpec((tm, d), lambda i: (i, 0)), pl.BlockSpec((1, d), lambda i: (0, 0))],
        out_specs=pl.BlockSpec((d, tm), lambda i: (0, i)),
        out_shape=jax.ShapeDtypeStruct((d, m), BF16),
        compiler_params=_params("parallel"),
        name="rmsnorm_t",
    )(x, gain.reshape(1, d))


def _sort_desc(v):
    v = list(v)
    n = len(v)
    k = 2
    while k <= n:
        j = k // 2
        while j >= 1:
            for i in range(n):
                l = i ^ j
                if l > i:
                    hi, lo = jnp.maximum(v[i], v[l]), jnp.minimum(v[i], v[l])
                    v[i], v[l] = (hi, lo) if (i & k) == 0 else (lo, hi)
            j //= 2
        k *= 2
    return v


def _merge_top(a, b):
    n = len(a)
    v = [jnp.maximum(a[i], b[n - 1 - i]) for i in range(n)]
    j = n // 2
    while j >= 1:
        for i in range(n):
            l = i ^ j
            if l > i:
                v[i], v[l] = jnp.maximum(v[i], v[l]), jnp.minimum(v[i], v[l])
        j //= 2
    return v


def _top_sorted(vals, n):
    vals = list(vals)
    while len(vals) % n:
        vals.append(jnp.full_like(vals[0], -jnp.inf))
    acc = _sort_desc(vals[:n])
    for g in range(1, len(vals) // n):
        acc = _merge_top(acc, _sort_desc(vals[g * n:(g + 1) * n]))
    return acc


_PEER_PAIRS = [(i, j) for i in range(PEER_TOPK) for j in range(PEER_TOPK) if (i + 1) * (j + 1) <= PEER_TOPK]


def _peer_select_kernel(wf_ref, ht_ref, e1_ref, n1_ref, r2_ref, e2_ref, sub_ref):
    nk, k = PEER_NKEYS, PEER_TOPK
    tt = ht_ref.shape[1]
    ht = ht_ref[...]
    for c in range(2):
        sub_ref[c] = jnp.dot(wf_ref[c], ht, preferred_element_type=F32)
    row = lax.broadcasted_iota(jnp.int32, (SUBLANES, LANES), 0)

    def head_row(v, h):
        return jnp.sum(jnp.where(row == h, v, 0.0), axis=0, keepdims=True)

    def lane_group(lg, carry):
        lanes = pl.ds(pl.multiple_of(lg * LANES, LANES), LANES)

        def top_of_head(h, packed, c):
            base = pl.multiple_of(h * nk, nk)
            slabs = [sub_ref[c, pl.ds(base + SUBLANES * j, SUBLANES), lanes] for j in range(nk // SUBLANES)]
            top = _sort_desc(slabs)
            for sh in (4, 2, 1):
                top = _merge_top(top, [pltpu.roll(t, sh, 0) for t in top])
            return tuple(jnp.where(row == h, top[i], packed[i]) for i in range(k))

        zero = tuple(jnp.zeros((SUBLANES, LANES), F32) for _ in range(k))
        a = lax.fori_loop(0, PEER_HEADS, functools.partial(top_of_head, c=0), zero)
        b = lax.fori_loop(0, PEER_HEADS, functools.partial(top_of_head, c=1), zero)
        best = _top_sorted([a[i] + b[j] for i, j in _PEER_PAIRS], k)
        thr, vmax = best[k - 1], best[0]
        z = jnp.zeros((SUBLANES, LANES), F32)
        for i in range(k):
            z = z + jnp.exp(best[i] - vmax)
        inv_z = 1.0 / z

        def emit(h, carry):
            base = pl.multiple_of(h * nk, nk)
            s1 = sub_ref[0, pl.ds(base, nk), lanes]
            s2 = sub_ref[1, pl.ds(base, nk), lanes]
            thr_h = head_row(thr, h)
            n1 = jnp.zeros_like(s1)
            r2 = jnp.zeros_like(s2)
            for j in range(k):
                bj = head_row(b[j], h)
                n1 = n1 + jnp.where(s1 + bj >= thr_h, 1.0, 0.0)
                r2 = r2 + jnp.where(bj > s2, 1.0, 0.0)
            e1_ref[h, :, lanes] = jnp.exp(s1 - head_row(a[0], h)) * head_row(inv_z, h)
            n1_ref[h, :, lanes] = n1
            r2_ref[h, :, lanes] = r2.astype(BF16)
            e2_ref[h, :, lanes] = jnp.exp(s2 - head_row(b[0], h)).astype(BF16)
            return carry

        lax.fori_loop(0, PEER_HEADS, emit, 0)
        return carry

    lax.fori_loop(0, tt // LANES, lane_group, 0)


def peer_select(wf, ht, tt=256):
    d, s = ht.shape
    nrow = PEER_HEADS * PEER_NKEYS
    shape = (PEER_HEADS, PEER_NKEYS, s)
    ospec = pl.BlockSpec((PEER_HEADS, PEER_NKEYS, tt), lambda i: (0, 0, i))
    return pl.pallas_call(
        _peer_select_kernel,
        grid=(s // tt,),
        in_specs=[pl.BlockSpec((2, nrow, d), lambda i: (0, 0, 0)), pl.BlockSpec((d, tt), lambda i: (0, i))],
        out_specs=[ospec] * 4,
        out_shape=[jax.ShapeDtypeStruct(shape, F32)] * 2 + [jax.ShapeDtypeStruct(shape, BF16)] * 2,
        scratch_shapes=[pltpu.VMEM((2, nrow, tt), F32)],
        compiler_params=_params("parallel"),
        name="peer_select",
    )(wf, ht)


def _gelu_tanh(x):
    return 0.5 * x * (1.0 + jnp.tanh(math.sqrt(2.0 / math.pi) * (x + 0.044715 * (x * x * x))))


def _peer_dense_kernel(u_ref, ht_ref, vt_ref, e1_ref, n1_ref, r2_ref, e2_ref, o_ref, act0_ref, act1_ref, ga_ref):
    nk = PEER_NKEYS
    j = pl.program_id(1)
    eb = u_ref.shape[0]
    nblk = eb // nk

    @pl.when(j == 0)
    def _():
        o_ref[...] = jnp.zeros_like(o_ref)
        act1_ref[...] = jnp.zeros_like(act1_ref)

    done = jnp.maximum(j - 1, 0)

    def step(prev_ref, next_ref):
        for ii in range(nblk):
            i1 = done * nblk + ii
            act = _gelu_tanh(prev_ref[pl.ds(ii * nk, nk), :])
            gate = None
            for h in range(PEER_HEADS):
                n1 = n1_ref[h, pl.ds(i1, 1), :].astype(BF16)
                e1 = e1_ref[h, pl.ds(i1, 1), :].astype(BF16)
                g = jnp.where(r2_ref[h] < n1, e2_ref[h] * e1, jnp.zeros((), BF16))
                gate = g if gate is None else gate + g
            ga_ref[pl.ds(ii * nk, nk), :] = (gate.astype(F32) * act).astype(BF16)
        next_ref[...] = jnp.dot(u_ref[...], ht_ref[...], preferred_element_type=F32)
        o_ref[...] += jnp.dot(vt_ref[...], ga_ref[...], preferred_element_type=F32)

    pl.when(j % 2 == 0)(functools.partial(step, act1_ref, act0_ref))
    pl.when(j % 2 == 1)(functools.partial(step, act0_ref, act1_ref))


def peer_dense(u, ht, vt, e1, n1, r2, e2, tt=512, eb=1024):
    d, s = ht.shape
    ne = u.shape[0]
    last = ne // eb - 1
    gspec = pl.BlockSpec((PEER_HEADS, PEER_NKEYS, tt), lambda i, j: (0, 0, i))
    return pl.pallas_call(
        _peer_dense_kernel,
        grid=(s // tt, ne // eb + 1),
        in_specs=[
            pl.BlockSpec((eb, d), lambda i, j: (jnp.minimum(j, last), 0)),
            pl.BlockSpec((d, tt), lambda i, j: (0, i)),
            pl.BlockSpec((d, eb), lambda i, j: (0, jnp.maximum(j - 1, 0))),
            gspec, gspec, gspec, gspec,
        ],
        out_specs=pl.BlockSpec((d, tt), lambda i, j: (0, i)),
        out_shape=jax.ShapeDtypeStruct((d, s), F32),
        scratch_shapes=[pltpu.VMEM((eb, tt), F32), pltpu.VMEM((eb, tt), F32), pltpu.VMEM((eb, tt), BF16)],
        compiler_params=_params("parallel", "arbitrary"),
        name="peer_dense",
    )(u, ht, vt, e1, n1, r2, e2)


def peer(x, ln_g, w_q, keys, u_bf, vt_bf, tm=512, tt=256):
    wf = peer_fold(w_q, keys).reshape(2, PEER_HEADS * PEER_NKEYS, D_MODEL).astype(BF16)
    ht = rmsnorm_t(x, ln_g, tm=tm)
    return peer_dense(u_bf, ht, vt_bf, *peer_select(wf, ht, tt=tt), tt=tm)


def _mla_kernel(qi_ref, ki_ref, qn_ref, qr_ref, qrot_ref, cq_ref, sq_ref, kn_ref, kr_ref, krot_ref, ck_ref, sk_ref,
                vt_ref, o_ref, q1_ref, q2_ref, m_ref, l_ref, acc_ref, *, scale):
    t = pl.program_id(1)
    qi, ki = qi_ref[t], ki_ref[t]
    tq, tk = qn_ref.shape[1], kn_ref.shape[0]
    ratio = tq // tk

    @pl.when(ki == 0)
    def _():
        q1_ref[...] = (qn_ref[...] * scale).astype(BF16)
        q2_ref[...] = ((qr_ref[...] * cq_ref[...] + qrot_ref[...] * sq_ref[...]) * scale).astype(BF16)
        m_ref[...] = jnp.full_like(m_ref, -jnp.inf)
        l_ref[...] = jnp.zeros_like(l_ref)
        acc_ref[...] = jnp.zeros_like(acc_ref)

    def step(masked):
        kr = (kr_ref[...] * ck_ref[...] + krot_ref[...] * sk_ref[...]).astype(BF16)
        s = (jnp.dot(kn_ref[...], q1_ref[...], preferred_element_type=F32)
             + jnp.dot(kr, q2_ref[...], preferred_element_type=F32))
        if masked:
            kpos = ki * tk + lax.broadcasted_iota(jnp.int32, (tk, tq), 0)
            qpos = qi * tq + lax.broadcasted_iota(jnp.int32, (tk, tq), 1)
            s = jnp.where(kpos <= qpos, s, -jnp.inf)
        m_old = m_ref[...]
        m_new = jnp.maximum(m_old, jnp.max(s, axis=0, keepdims=True))
        alpha = jnp.exp(m_old - m_new)
        p = jnp.exp(s - m_new)
        l_ref[...] = alpha * l_ref[...] + jnp.sum(p, axis=0, keepdims=True)
        acc_ref[...] = alpha * acc_ref[...] + jnp.dot(vt_ref[...], p.astype(BF16), preferred_element_type=F32)
        m_ref[...] = m_new

    pl.when(ki < qi * ratio)(functools.partial(step, False))
    pl.when(ki >= qi * ratio)(functools.partial(step, True))

    @pl.when(ki == (qi + 1) * ratio - 1)
    def _():
        o_ref[...] = acc_ref[...] / l_ref[...]


def mla_attention(qt, kv, vt, kr, krot, cos, sin, cos_t, sin_t, tq=1024, tk=512):
    s = kv.shape[0]
    tq, tk = min(tq, s), min(tk, s)
    ratio = tq // tk
    hh, dn, dr = MLA_HEADS, MLA_NOPE, MLA_ROPE
    pairs = [(qi, ki) for qi in range(s // tq) for ki in range((qi + 1) * ratio)]
    qi_tab = jnp.array([pr[0] for pr in pairs], jnp.int32)
    ki_tab = jnp.array([pr[1] for pr in pairs], jnp.int32)
    r0 = hh * dn // dr
    qmap = lambda h, t, qt_, kt_: (0, qt_[t])
    kmap = lambda h, t, qt_, kt_: (kt_[t], 0)
    grid_spec = pltpu.PrefetchScalarGridSpec(
        num_scalar_prefetch=2,
        grid=(hh, len(pairs)),
        in_specs=[
            pl.BlockSpec((dn, tq), lambda h, t, qt_, kt_: (h, qt_[t])),
            pl.BlockSpec((dr, tq), lambda h, t, qt_, kt_: (r0 + h, qt_[t])),
            pl.BlockSpec((dr, tq), lambda h, t, qt_, kt_: (r0 + hh + h, qt_[t])),
            pl.BlockSpec((dr, tq), qmap),
            pl.BlockSpec((dr, tq), qmap),
            pl.BlockSpec((tk, dn), lambda h, t, qt_, kt_: (kt_[t], h)),
            pl.BlockSpec((tk, dr), kmap),
            pl.BlockSpec((tk, dr), kmap),
            pl.BlockSpec((tk, dr), kmap),
            pl.BlockSpec((tk, dr), kmap),
            pl.BlockSpec((MLA_V, tk), lambda h, t, qt_, kt_: (h, kt_[t])),
        ],
        out_specs=pl.BlockSpec((MLA_V, tq), lambda h, t, qt_, kt_: (h, qt_[t])),
        scratch_shapes=[pltpu.VMEM((dn, tq), BF16), pltpu.VMEM((dr, tq), BF16), pltpu.VMEM((1, tq), F32),
                        pltpu.VMEM((1, tq), F32), pltpu.VMEM((MLA_V, tq), F32)],
    )
    return pl.pallas_call(
        functools.partial(_mla_kernel, scale=(MLA_NOPE + MLA_ROPE) ** -0.5),
        grid_spec=grid_spec,
        out_shape=jax.ShapeDtypeStruct((hh * MLA_V, s), F32),
        compiler_params=_params("parallel", "arbitrary"),
        name="mla_attention",
    )(qi_tab, ki_tab, qt, qt, qt, cos_t, sin_t, kv, kr, krot, cos, sin, vt)


def _causal_conv(cur_ref, prev_ref, w_ref, first, bias=None):
    prev = jnp.where(first, 0.0, prev_ref[...])
    xe = jnp.concatenate([prev, cur_ref[...]], axis=0)
    w = w_ref[...]
    acc = w[CONV_WIDTH - 1:CONV_WIDTH] * xe[SUBLANES:]
    for j in range(CONV_WIDTH - 1):
        acc = acc + w[j:j + 1] * pltpu.roll(xe, CONV_WIDTH - 1 - j, 0)[SUBLANES:]
    return acc if bias is None else acc + bias


def _unit_lower_inverses(lows, c):
    n = lows[0].shape[0]
    r = lax.broadcasted_iota(jnp.int32, (n, n), 0)
    q = lax.broadcasted_iota(jnp.int32, (n, n), 1)
    eye = (r == q).astype(F32)
    prev = [jnp.where((r // 16) == (q // 16), low, 0.0) for low in lows]
    ps = [-d for d in prev]
    xs = [eye + p for p in ps]
    for _ in range(3):
        ps = [_bdot(p, p) for p in ps]
        xs = [x + _bdot(x, p) for x, p in zip(xs, ps)]
    size = 32
    while size <= c:
        cur = [jnp.where((r // size) == (q // size), low, 0.0) for low in lows] if size < c else lows
        ts = [_bdot(x, cu - pr) for x, cu, pr in zip(xs, cur, prev)]
        xs = [x - _bdot(t, x) for x, t in zip(xs, ts)]
        prev = cur
        size *= 2
    return xs


def _gdn_kernel(q_ref, qp_ref, k_ref, kp_ref, v_ref, vp_ref, z_ref, wq_ref, wk_ref, wv_ref,
                ac_ref, ar_ref, bc_ref, alog_ref, dtb_ref, ng_ref, o_ref, st_ref):
    c = GDN_CHUNK
    tt = q_ref.shape[0]
    first = pl.program_id(1) == 0

    @pl.when(first)
    def _():
        st_ref[...] = jnp.zeros_like(st_ref)

    def l2n(x):
        return x * lax.rsqrt(jnp.sum(x * x, axis=-1, keepdims=True) + 1e-6)

    q = l2n(_silu(_causal_conv(q_ref, qp_ref, wq_ref, first))) * (GDN_DK ** -0.5)
    k = l2n(_silu(_causal_conv(k_ref, kp_ref, wk_ref, first)))
    v = _silu(_causal_conv(v_ref, vp_ref, wv_ref, first))
    neg_a = -jnp.exp(alog_ref[0, :, 0:1])
    dtb = dtb_ref[0, :, 0:1]
    nb = 2 * c
    r = lax.broadcasted_iota(jnp.int32, (nb, nb), 0)
    cc = lax.broadcasted_iota(jnp.int32, (nb, nb), 1)
    same = (r // c) == (cc // c)
    incl, strict = same & (r >= cc), same & (r > cc)
    incl_t = same & (r <= cc)
    top = lax.broadcasted_iota(jnp.int32, (nb, 1), 0) < c
    blocks = range(tt // nb)
    rows = [slice(n * nb, (n + 1) * nb) for n in blocks]
    g_col = [neg_a * _softplus(ac_ref[0, n] + dtb) for n in blocks]
    g_row = [neg_a * _softplus(ar_ref[0, n] + dtb) for n in blocks]
    beta = [_sigmoid(bc_ref[0, n]) for n in blocks]
    gc = [jnp.sum(jnp.where(incl, g, 0.0), axis=1, keepdims=True) for g in g_row]
    gr = [jnp.sum(jnp.where(incl_t, g, 0.0), axis=0, keepdims=True) for g in g_col]
    decay = [jnp.where(incl, jnp.exp(jnp.where(incl, a - b, 0.0)), 0.0) for a, b in zip(gc, gr)]
    kk = [_bdot_nt(k[sl], k[sl]) for sl in rows]
    qk = [_bdot_nt(q[sl], k[sl]) for sl in rows]
    inv = _unit_lower_inverses([jnp.where(strict, b * m * d, 0.0) for b, m, d in zip(beta, kk, decay)], c)
    eg = [jnp.exp(g) for g in gc]
    sol = [_bdot(x, jnp.concatenate([b * v[sl], (b * e) * k[sl]], axis=1)) for x, b, e, sl in zip(inv, beta, eg, rows)]
    a_qk = [m * d for m, d in zip(qk, decay)]
    k_end = [k[sl] * jnp.exp(jnp.where(top, g[c - 1:c], g[nb - 1:nb]) - g) for g, sl in zip(gc, rows)]
    k_end_t = [[ke[:c].T, ke[c:].T] for ke in k_end]
    q_dec = [q[sl] * e for e, sl in zip(eg, rows)]
    state = st_ref[...]
    us, o_state = [], []
    for n in blocks:
        for j in range(2):
            cs = slice(j * c, (j + 1) * c)
            u = sol[n][cs, :GDN_DV] - _bdot(sol[n][cs, GDN_DV:], state)
            o_state.append(_bdot(q_dec[n][cs], state))
            state = jnp.exp(gc[n][(j + 1) * c - 1:(j + 1) * c]) * state + _bdot(k_end_t[n][j], u)
            us.append(u)
    st_ref[...] = state
    o_loc = [_bdot(a, jnp.concatenate(us[2 * n:2 * n + 2], axis=0)) for n, a in zip(blocks, a_qk)]
    o = jnp.concatenate(o_state, axis=0) + jnp.concatenate(o_loc, axis=0)
    o_ref[...] = _rms(o, ng_ref[...]) * _silu(z_ref[...])


def gated_delta_net(proj, col0, a_raw, b_raw, conv_w, a_log, dt_bias, norm_g, tt=256):
    s = proj.shape[0]
    tt = min(tt, s)
    hh, c = GDN_HEADS, GDN_CHUNK
    b0 = col0 // LANES
    nblk = GDN_HEADS * GDN_DK // LANES

    def cur(g):
        return pl.BlockSpec((tt, LANES), lambda h, i: (i, b0 + g * nblk + h))

    def prev(g):
        return pl.BlockSpec((SUBLANES, LANES), lambda h, i: (jnp.maximum(i * (tt // SUBLANES) - 1, 0), b0 + g * nblk + h))

    def wspec(g):
        return pl.BlockSpec((CONV_WIDTH, LANES), lambda h, i: (0, g * nblk + h))

    a_t, b_t = a_raw.T, b_raw.T
    c = 2 * c
    colspec = pl.BlockSpec((1, tt // c, c, 1), lambda h, i: (h, i, 0, 0))
    rowspec = pl.BlockSpec((1, tt // c, 1, c), lambda h, i: (h, i, 0, 0))
    hspec = pl.BlockSpec((1, 1, LANES), lambda h, i: (h, 0, 0))
    bcast = lambda p: jnp.broadcast_to(p.astype(F32)[:, None, None], (hh, 1, LANES))
    return pl.pallas_call(
        _gdn_kernel,
        grid=(hh, s // tt),
        in_specs=[cur(0), prev(0), cur(1), prev(1), cur(2), prev(2), cur(3), wspec(0), wspec(1), wspec(2),
                  colspec, rowspec, colspec, hspec, hspec, pl.BlockSpec((1, GDN_DV), lambda h, i: (0, 0))],
        out_specs=pl.BlockSpec((tt, GDN_DV), lambda h, i: (i, h)),
        out_shape=jax.ShapeDtypeStruct((s, hh * GDN_DV), F32),
        scratch_shapes=[pltpu.VMEM((GDN_DK, GDN_DV), F32)],
        compiler_params=_params("parallel", "arbitrary"),
        name="gated_delta_net",
    )(proj, proj, proj, proj, proj, proj, proj, conv_w, conv_w, conv_w,
      a_t.reshape(hh, s // c, c, 1), a_t.reshape(hh, s // c, 1, c), b_t.reshape(hh, s // c, c, 1),
      bcast(a_log), bcast(dt_bias), norm_g.reshape(1, GDN_DV).astype(F32))


def _rot_half_cols(w, half):
    return jnp.concatenate([-w[..., half:], w[..., :half]], axis=-1)


def _pad_cols(w, n):
    return jnp.pad(w, ((0, 0), (0, n - w.shape[1])))


AB_QKVZ = MLA_Q_RANK + MLA_KV_RANK + 2 * MLA_ROPE
AB_GATES = AB_QKVZ + 4 * GDN_HEADS * GDN_DK
AB_PAD = AB_GATES + LANES


def prep_ab(w_in, w_uq, w_ukv, w_out):
    rq, rkv, rr = MLA_Q_RANK, MLA_KV_RANK, MLA_ROPE
    w_kr = w_in[:, rq + rkv:rq + rkv + rr]
    w_main = jnp.concatenate([w_in[:, :rq + rkv], w_kr, _rot_half_cols(w_kr, rr // 2),
                              w_in[:, rq + rkv + rr:]], axis=1)
    w_main = _pad_cols(w_main, AB_PAD).astype(BF16)
    uq = w_uq.reshape(rq, MLA_HEADS, MLA_NOPE + MLA_ROPE)
    uq_r = uq[..., MLA_NOPE:]
    uq2 = jnp.concatenate([uq[..., :MLA_NOPE].reshape(rq, -1), uq_r.reshape(rq, -1),
                           _rot_half_cols(uq_r, rr // 2).reshape(rq, -1)], axis=1).astype(BF16)
    ukv = w_ukv.reshape(rkv, MLA_HEADS, MLA_NOPE + MLA_V)
    ukv2 = jnp.concatenate([ukv[..., :MLA_NOPE].reshape(rkv, -1), ukv[..., MLA_NOPE:].reshape(rkv, -1)], axis=1).astype(BF16)
    return w_main, uq2, ukv2, w_out.astype(BF16)


def mixer_ab(x, ln, w_main, uq2, ukv2, w_out, q_norm, kv_norm, conv_w, a_log, dt_bias, gdn_norm, cos2, sin2,
             tm=512, t_attn=512, t_gdn=256):
    s = x.shape[0]
    rq, rkv, rr = MLA_Q_RANK, MLA_KV_RANK, MLA_ROPE
    proj = norm_matmul(x, w_main, gain=ln, tm=tm)
    qfull = norm_matmul(proj, uq2, gain=q_norm, tm=tm)
    kv = norm_matmul(proj, ukv2, gain=kv_norm, tm=tm, out_dtype=BF16, x_col=rq)
    nn = MLA_HEADS * MLA_NOPE
    o_a = mla_attention(qfull.T, kv, kv[:, nn:].T, proj[:, rq + rkv:rq + rkv + rr], proj[:, rq + rkv + rr:AB_QKVZ],
                        cos2, sin2, cos2.T, sin2.T, tq=2 * t_attn, tk=t_attn).T
    o_b = gated_delta_net(proj, AB_QKVZ, proj[:, AB_GATES:AB_GATES + GDN_HEADS],
                          proj[:, AB_GATES + GDN_HEADS:AB_GATES + 2 * GDN_HEADS], conv_w, a_log, dt_bias, gdn_norm, tt=t_gdn)
    return out_proj(o_a, o_b, w_out, x, tm=tm)


CD_R, CD_K, CD_V, CD_Z, CD_DT, CD_X = (i * 1024 for i in range(6))
CD_B = 6144
CD_C = CD_B + SSD_GROUPS * SSD_STATE
CD_WA = CD_C + SSD_GROUPS * SSD_STATE
CD_G = CD_WA + LANES
CD_PAD = 7168
GROUP_W = SSD_INNER // SSD_GROUPS


def _lower_ones(c):
    r = lax.broadcasted_iota(jnp.int32, (c, c), 0)
    q = lax.broadcasted_iota(jnp.int32, (c, c), 1)
    return r >= q, r > q


def _ssd_kernel(x_ref, xp_ref, b_ref, bp_ref, c_ref, cp_ref, z_ref, dt_ref, wx_ref, wb_ref, wc_ref,
                bx_ref, bb_ref, bc_ref, dtb_ref, alog_ref, dskip_ref, ng_ref, o_ref, st_ref):
    c = SSD_CHUNK
    tt = x_ref.shape[0]
    first = pl.program_id(1) == 0

    @pl.when(first)
    def _():
        st_ref[...] = jnp.zeros_like(st_ref)

    xs_all = _silu(_causal_conv(x_ref, xp_ref, wx_ref, first, bx_ref[...]))
    bm_all = _silu(_causal_conv(b_ref, bp_ref, wb_ref, first, bb_ref[...]))
    cm_all = _silu(_causal_conv(c_ref, cp_ref, wc_ref, first, bc_ref[...]))
    dt_all = _softplus(dt_ref[...] + dtb_ref[...])
    a_all = -jnp.exp(alog_ref[...]) * dt_all
    incl, _ = _lower_ones(c)
    tri = incl.astype(F32)
    left = lax.broadcasted_iota(jnp.int32, (c, LANES), 1) < SSD_HEADDIM
    npair = GROUP_W // LANES
    outs = []
    for n in range(tt // c):
        sl = slice(n * c, (n + 1) * c)
        xs, bm, cm, dt = xs_all[sl], bm_all[sl], cm_all[sl], dt_all[sl]
        acs = _bdot(tri, a_all[sl])
        xdt = xs * dt
        cb = _bdot_nt(cm, bm)
        bm_t = bm.T
        ys = []
        for p in range(npair):
            ls = slice(p * LANES, (p + 1) * LANES)
            acs_p = acs[:, ls]
            acs_t = acs_p.T
            xp = xdt[:, ls]
            yd = []
            for hd in range(2):
                col = acs_p[:, hd * SSD_HEADDIM:hd * SSD_HEADDIM + 1]
                row = acs_t[hd * SSD_HEADDIM:hd * SSD_HEADDIM + 1, :]
                lmat = jnp.where(incl, jnp.exp(jnp.where(incl, col - row, 0.0)), 0.0)
                yd.append(_bdot(cb * lmat, xp))
            last = acs_p[c - 1:c]
            prev_t = st_ref[p]
            y_off = _bdot(cm, prev_t) * jnp.exp(acs_p)
            st_ref[p] = jnp.exp(last) * prev_t + _bdot(bm_t, xp * jnp.exp(last - acs_p))
            ys.append(jnp.where(left, yd[0], yd[1]) + y_off)
        outs.append(jnp.concatenate(ys, axis=1) + xs * dskip_ref[...])
    y = jnp.concatenate(outs, axis=0) * _silu(z_ref[...])
    o_ref[...] = _rms(y, ng_ref[...])


def mamba2_ssd(proj, conv_w, conv_b, dt_bias, a_log, d_skip, norm_g, tt=256):
    s = proj.shape[0]
    tt = min(tt, s)
    gw, ns = GROUP_W, SSD_STATE
    per = lambda v: jnp.repeat(v.astype(F32), SSD_HEADDIM).reshape(1, SSD_INNER)

    def cur(col, w):
        return pl.BlockSpec((tt, w), lambda g, i: (i, col // w + g))

    def prev(col, w):
        return pl.BlockSpec((SUBLANES, w), lambda g, i: (jnp.maximum(i * (tt // SUBLANES) - 1, 0), col // w + g))

    def par(rows, col, w):
        return pl.BlockSpec((rows, w), lambda g, i: (0, col // w + g))

    cb = conv_b.reshape(1, -1).astype(F32)
    return pl.pallas_call(
        _ssd_kernel,
        grid=(SSD_GROUPS, s // tt),
        in_specs=[cur(CD_X, gw), prev(CD_X, gw), cur(CD_B, ns), prev(CD_B, ns), cur(CD_C, ns), prev(CD_C, ns),
                  cur(CD_Z, gw), cur(CD_DT, gw),
                  par(CONV_WIDTH, 0, gw), par(CONV_WIDTH, SSD_INNER, ns), par(CONV_WIDTH, SSD_INNER + SSD_GROUPS * ns, ns),
                  par(1, 0, gw), par(1, SSD_INNER, ns), par(1, SSD_INNER + SSD_GROUPS * ns, ns),
                  par(1, 0, gw), par(1, 0, gw), par(1, 0, gw), par(1, 0, gw)],
        out_specs=pl.BlockSpec((tt, gw), lambda g, i: (i, g)),
        out_shape=jax.ShapeDtypeStruct((s, SSD_INNER), F32),
        scratch_shapes=[pltpu.VMEM((gw // LANES, ns, LANES), F32)],
        compiler_params=_params("parallel", "arbitrary"),
        name="mamba2_ssd",
    )(proj, proj, proj, proj, proj, proj, proj, proj, conv_w, conv_w, conv_w, cb, cb, cb,
      per(dt_bias), per(a_log), per(d_skip), norm_g.reshape(1, SSD_INNER).astype(F32))


def _pair_ones():
    r = lax.broadcasted_iota(jnp.int32, (LANES, LANES), 0)
    q = lax.broadcasted_iota(jnp.int32, (LANES, LANES), 1)
    return (r // RWKV_HEAD) == (q // RWKV_HEAD)


def _head_sums(x, ones):
    return jnp.concatenate([_hdot(x[:, i * LANES:(i + 1) * LANES], ones) for i in range(x.shape[1] // LANES)], axis=1)


def _rwkv_prep_kernel(r_ref, rp_ref, k_ref, kp_ref, v_ref, vp_ref, wa_ref, wap_ref, g0_ref, g0p_ref, g1_ref, g1p_ref,
                      mur_ref, muk_ref, muv_ref, muwa_ref, mug0_ref, mug1_ref, w0_ref, w2_ref, a0_ref, a2_ref, g2_ref,
                      kk_ref, ka_ref, rk_ref,
                      ro_ref, lw_ref, ko_ref, vo_ref, po_ref, qo_ref, go_ref, bo_ref):
    first = pl.program_id(0) == 0

    def mix(cur_ref, prev_ref, mu_ref):
        cur = cur_ref[...]
        prev = jnp.where(first, 0.0, prev_ref[...])
        shifted = pltpu.roll(jnp.concatenate([prev, cur], axis=0), 1, 0)[SUBLANES:]
        return cur + (shifted - cur) * mu_ref[...]

    r = mix(r_ref, rp_ref, mur_ref)
    k = mix(k_ref, kp_ref, muk_ref)
    v = mix(v_ref, vp_ref, muv_ref)
    wa = mix(wa_ref, wap_ref, muwa_ref)
    g0 = mix(g0_ref, g0p_ref, mug0_ref)
    g1 = mix(g1_ref, g1p_ref, mug1_ref)
    log_w = -math.exp(-0.5) * _sigmoid(w0_ref[...] + _bdot(jnp.tanh(wa), w2_ref[...]))
    a = _sigmoid(a0_ref[...] + _bdot(wa, a2_ref[...]))
    gate = _bdot(_sigmoid(g0), g2_ref[0:LANES, :]) + _bdot(_sigmoid(g1), g2_ref[LANES:2 * LANES, :])
    ones = _pair_ones().astype(F32)
    kx = k * kk_ref[...]
    kk = kx * lax.rsqrt(_head_sums(kx * kx, ones) + 1e-6)
    k_mod = k * (1.0 + (a - 1.0) * ka_ref[...])
    ro_ref[...] = r
    lw_ref[...] = log_w
    ko_ref[...] = k_mod
    vo_ref[...] = v
    po_ref[...] = -kk * a
    qo_ref[...] = kk
    go_ref[...] = gate
    bo_ref[...] = _head_sums(r * k_mod * rk_ref[...], ones) * v


def rwkv_prep(proj, mu, w0, w2, a0, a2, g2, k_k, k_a, r_k, tt=256):
    s = proj.shape[0]
    tt = min(tt, s)
    ri = RWKV_INNER
    row = lambda v: v.reshape(1, -1).astype(F32)
    mu_r, mu_k, mu_v = (row(mu[i * ri:(i + 1) * ri]) for i in range(3))
    mu_wa = row(mu[3 * ri:3 * ri + LANES])
    mu_g = row(jnp.pad(mu[3 * ri + LANES:], (0, 2 * LANES - RWKV_G_LORA)))
    zeros = jnp.zeros((RWKV_W_LORA, ri), F32)
    w2p = jnp.concatenate([w2, zeros], axis=0).astype(BF16)
    a2p = jnp.concatenate([zeros, a2], axis=0).astype(BF16)
    g2p = jnp.pad(g2, ((0, 2 * LANES - RWKV_G_LORA), (0, 0))).astype(BF16)

    def cur(col, w):
        return pl.BlockSpec((tt, w), lambda i: (i, col // w))

    def prev(col, w):
        return pl.BlockSpec((SUBLANES, w), lambda i: (jnp.maximum(i * (tt // SUBLANES) - 1, 0), col // w))

    full = lambda a: pl.BlockSpec(a.shape, lambda i: (0, 0))
    params = [mu_r, mu_k, mu_v, mu_wa, mu_g[:, :LANES], mu_g[:, LANES:], row(w0), w2p, row(a0), a2p, g2p,
              row(k_k), row(k_a), row(r_k)]
    out = jax.ShapeDtypeStruct((s, ri), F32)
    return pl.pallas_call(
        _rwkv_prep_kernel,
        grid=(s // tt,),
        in_specs=[cur(CD_R, ri), prev(CD_R, ri), cur(CD_K, ri), prev(CD_K, ri), cur(CD_V, ri), prev(CD_V, ri),
                  cur(CD_WA, LANES), prev(CD_WA, LANES), cur(CD_G, LANES), prev(CD_G, LANES),
                  cur(CD_G + LANES, LANES), prev(CD_G + LANES, LANES)] + [full(a) for a in params],
        out_specs=[pl.BlockSpec((tt, ri), lambda i: (i, 0))] * 8,
        out_shape=[out] * 8,
        compiler_params=_params("arbitrary"),
        name="rwkv_prep",
    )(*([proj] * 12), *params)


RWKV_MY_CHUNK = 64


def _rwkv_scan_kernel(r_ref, lw_ref, k_ref, v_ref, p_ref, q_ref, g_ref, b_ref, lnw_ref, lnb_ref, o_ref, st_ref):
    c = RWKV_MY_CHUNK
    tt = r_ref.shape[0]

    @pl.when(pl.program_id(1) == 0)
    def _():
        st_ref[...] = jnp.zeros_like(st_ref)

    tri = _lower_ones(c)[0].astype(F32)
    left = lax.broadcasted_iota(jnp.int32, (c, LANES), 1) < RWKV_HEAD
    pair = _pair_ones()
    ones = pair.astype(F32)
    r_id = lax.broadcasted_iota(jnp.int32, (LANES, LANES), 0)
    c_id = lax.broadcasted_iota(jnp.int32, (LANES, LANES), 1)
    eye = r_id == c_id
    top = r_id < c
    strict = pair & ((r_id % c) > (c_id % c))
    incl = pair & ((r_id % c) >= (c_id % c))
    stack2 = lambda a: jnp.concatenate([a, a], axis=0)
    by_head = lambda a: jnp.concatenate([jnp.where(left, a, 0.0), jnp.where(left, 0.0, a)], axis=0)
    unstack = lambda a: jnp.where(left, a[:c], a[c:])
    chunks = range(tt // c)
    rows = [slice(n * c, (n + 1) * c) for n in chunks]
    w = [lw_ref[sl, :] for sl in rows]
    v = [v_ref[sl, :] for sl in rows]
    lw = [_bdot(tri, x) for x in w]
    lam_in = [jnp.exp(x) for x in lw]
    inv_lam = [jnp.exp(-x) for x in lw]
    q_bar = [q_ref[sl, :] * jnp.exp(a - b) for sl, a, b in zip(rows, lw, w)]
    r_bar = [r_ref[sl, :] * x for sl, x in zip(rows, lam_in)]
    pk = [jnp.concatenate([p_ref[sl, :] * x, k_ref[sl, :] * x], axis=0) for sl, x in zip(rows, inv_lam)]
    mq = [_bdot_nt(by_head(a), b) for a, b in zip(q_bar, pk)]
    mr = [_bdot_nt(by_head(a), b) for a, b in zip(r_bar, pk)]
    mq_sw = [pltpu.roll(x, c, 1) for x in mq]
    mr_sw = [pltpu.roll(x, c, 1) for x in mr]
    m_qp = [jnp.where(strict, jnp.where(top, a, b), 0.0) for a, b in zip(mq, mq_sw)]
    m_qk = [jnp.where(strict, jnp.where(top, b, a), 0.0) for a, b in zip(mq, mq_sw)]
    m_rp = [jnp.where(incl, jnp.where(top, a, b), 0.0) for a, b in zip(mr, mr_sw)]
    m_rk = [jnp.where(incl, jnp.where(top, b, a), 0.0) for a, b in zip(mr, mr_sw)]
    vv = [stack2(x) for x in v]
    inv = _unit_lower_inverses([-x for x in m_qp], c)
    qkv = [_bdot(a, b) for a, b in zip(m_qk, vv)]
    sol = [_bdot(x, jnp.concatenate([stack2(a), b], axis=1)) for x, a, b in zip(inv, q_bar, qkv)]
    ws = [unstack(x[:, :LANES]) for x in sol]
    wv = [unstack(x[:, LANES:]) for x in sol]
    y_loc = [unstack(_bdot(a, b)) for a, b in zip(m_rk, vv)]
    lam_end = [x[c - 1:c] for x in lam_in]
    lam_col = [jnp.sum(jnp.where(eye, x, 0.0), axis=1, keepdims=True) for x in lam_end]
    pk_end_t = [(a * b).T for a, b in zip(pk, lam_end)]
    state = st_ref[...]
    us, y_state = [], []
    for n in chunks:
        u = _bdot(ws[n], state) + wv[n]
        y_state.append(_bdot(r_bar[n], state))
        upd = _bdot(pk_end_t[n], jnp.concatenate([u, v[n]], axis=0))
        state = lam_col[n] * state + jnp.where(pair, upd, 0.0)
        us.append(u)
    outs = [a + unstack(_bdot(b, stack2(u))) + d for a, b, u, d in zip(y_state, m_rp, us, y_loc)]
    st_ref[...] = state
    y = jnp.concatenate(outs, axis=0)
    mean = _hdot(y, ones) * (1.0 / RWKV_HEAD)
    yc = y - mean
    var = _hdot(yc * yc, ones) * (1.0 / RWKV_HEAD)
    y = yc * lax.rsqrt(var + RWKV_GN_EPS) * lnw_ref[...] + lnb_ref[...]
    o_ref[...] = (y + b_ref[...]) * g_ref[...]


def rwkv_scan(r, lw, k, v, p, q, gate, bonus, ln_w, ln_b, tt=256):
    s = r.shape[0]
    tt = min(tt, s)
    spec = pl.BlockSpec((tt, LANES), lambda h, i: (i, h))
    pspec = pl.BlockSpec((1, LANES), lambda h, i: (0, h))
    return pl.pallas_call(
        _rwkv_scan_kernel,
        grid=(RWKV_INNER // LANES, s // tt),
        in_specs=[spec] * 8 + [pspec] * 2,
        out_specs=spec,
        out_shape=jax.ShapeDtypeStruct((s, RWKV_INNER), F32),
        scratch_shapes=[pltpu.VMEM((LANES, LANES), F32)],
        compiler_params=_params("parallel", "arbitrary"),
        name="rwkv_scan",
    )(r, lw, k, v, p, q, gate, bonus, ln_w.reshape(1, -1).astype(F32), ln_b.reshape(1, -1).astype(F32))


def prep_cd(w_in, w_out):
    si, ri = SSD_INNER, RWKV_INNER
    z, xbc, dt, rw = w_in[:, :si], w_in[:, si:2 * si + 512], w_in[:, 2 * si + 512:2 * si + 528], w_in[:, 2 * si + 528:]
    dt_exp = jnp.repeat(dt, SSD_HEADDIM, axis=1)
    cols = [rw[:, :3 * ri], z, dt_exp, xbc, rw[:, 3 * ri:]]
    return _pad_cols(jnp.concatenate(cols, axis=1), CD_PAD).astype(BF16), w_out.astype(BF16)


def mixer_cd(x, ln, w_main, w_out, ssd_conv_w, ssd_conv_b, ssd_dt_bias, ssd_a_log, ssd_d, ssd_norm,
             mu, w0, w2, a0, a2, g2, k_k, k_a, r_k, ln_w, ln_b, tm=512, tt=256):
    proj = norm_matmul(x, w_main, gain=ln, tm=tm)
    o_c = mamba2_ssd(proj, ssd_conv_w, ssd_conv_b, ssd_dt_bias, ssd_a_log, ssd_d, ssd_norm, tt=tt)
    o_d = rwkv_scan(*rwkv_prep(proj, mu, w0, w2, a0, a2, g2, k_k, k_a, r_k.reshape(-1), tt=tt), ln_w, ln_b, tt=2 * tt)
    return out_proj(o_c, o_d, w_out, x, tm=tm)


def _rope_tables(s):
    inv = 1.0 / (ROPE_THETA ** (jnp.arange(0, MLA_ROPE, 2, dtype=F32) / MLA_ROPE))
    ang = jnp.arange(s, dtype=F32)[:, None] * inv[None, :]
    cos, sin = jnp.cos(ang), jnp.sin(ang)
    return jnp.concatenate([cos, cos], axis=1), jnp.concatenate([sin, sin], axis=1)


def kernel(x, p, ln_mix, ln_ffn, ab_w_in, mla_q_norm, mla_w_uq, mla_kv_norm, mla_w_ukv, gdn_conv_w, gdn_a_log, gdn_dt_bias, gdn_norm, ab_w_out, cd_w_in, ssd_conv_w, ssd_conv_b, ssd_dt_bias, ssd_a_log, ssd_d, ssd_norm, rwkv_mu, rwkv_w0, rwkv_w2, rwkv_a0, rwkv_a2, rwkv_g2, rwkv_k_k, rwkv_k_a, rwkv_r_k, rwkv_ln_w, rwkv_ln_b, cd_w_out, peer_w_q, peer_keys, peer_u, peer_v, ple_w_proj, ple_norm, ple_w_gate, final_norm):
    assert x.shape[0] == 1
    s = x.shape[1]
    tm = min(512, s)
    tt = min(256, s)
    cos2, sin2 = _rope_tables(s)
    xs = x[0]
    for i in range(DEPTH):
        j = i // 2
        if i % 2 == 0:
            wts = prep_ab(ab_w_in[j], mla_w_uq[j], mla_w_ukv[j], ab_w_out[j])
            xs = mixer_ab(xs, ln_mix[i], *wts, mla_q_norm[j], mla_kv_norm[j], gdn_conv_w[j], gdn_a_log[j],
                          gdn_dt_bias[j], gdn_norm[j], cos2, sin2, tm=tm, t_attn=tm, t_gdn=tm)
        else:
            wts = prep_cd(cd_w_in[j], cd_w_out[j])
            xs = mixer_cd(xs, ln_mix[i], *wts, ssd_conv_w[j], ssd_conv_b[j], ssd_dt_bias[j], ssd_a_log[j], ssd_d[j],
                          ssd_norm[j], rwkv_mu[j], rwkv_w0[j], rwkv_w2[j], rwkv_a0[j], rwkv_a2[j], rwkv_g2[j],
                          rwkv_k_k[j], rwkv_k_a[j], rwkv_r_k[j], rwkv_ln_w[j], rwkv_ln_b[j], tm=tm, tt=tt)
        yt = peer(xs, ln_ffn[i], peer_w_q[i], peer_keys[i], peer_u[i].astype(BF16), peer_v[i].T.astype(BF16), tm=tm, tt=tt)
        xs = ple_update(xs, yt, p[i, 0], ple_norm[i], ple_w_gate[i].astype(BF16), ple_w_proj[i].astype(BF16), tm=tm)
    return rmsnorm(xs, final_norm, tm=tm)[None]
```

```python
import functools
import math

import jax
import jax.numpy as jnp
from jax import lax
from jax.experimental import pallas as pl
from jax.experimental.pallas import tpu as pltpu

F32 = jnp.float32
BF16 = jnp.bfloat16
HIGHEST = lax.Precision.HIGHEST

D_MODEL = 2048
DEPTH = 4
PLE_DIM = 256
RMS_EPS = 1e-6
MLA_HEADS = 8
MLA_Q_RANK = 512
MLA_KV_RANK = 256
MLA_NOPE = 128
MLA_ROPE = 64
MLA_V = 128
ROPE_THETA = 10000.0
GDN_HEADS = 8
GDN_DK = 128
GDN_DV = 128
GDN_CHUNK = 64
SSD_HEADS = 16
SSD_HEADDIM = 64
SSD_GROUPS = 2
SSD_STATE = 128
SSD_CHUNK = 128
SSD_INNER = SSD_HEADS * SSD_HEADDIM
RWKV_HEADS = 16
RWKV_HEAD = 64
RWKV_INNER = RWKV_HEADS * RWKV_HEAD
RWKV_W_LORA = 64
RWKV_A_LORA = 64
RWKV_G_LORA = 160
RWKV_GN_EPS = 64e-5
CONV_WIDTH = 4
PEER_HEADS = 8
PEER_NKEYS = 128
PEER_EXPERTS = PEER_NKEYS * PEER_NKEYS
PEER_QDIM = 256
PEER_TOPK = 16

LANES = 128
SUBLANES = 8
VMEM_LIMIT = 56 * 1024 * 1024


def _params(*sem):
    return pltpu.CompilerParams(dimension_semantics=sem, vmem_limit_bytes=VMEM_LIMIT)


def _bdot(a, b):
    return jnp.dot(a.astype(BF16), b.astype(BF16), preferred_element_type=F32)


def _bdot_nt(a, b):
    return lax.dot_general(a.astype(BF16), b.astype(BF16), (((1,), (1,)), ((), ())), preferred_element_type=F32)


def _hdot(a, b):
    return jnp.dot(a, b, preferred_element_type=F32, precision=HIGHEST)


def _hdot_nt(a, b):
    return lax.dot_general(a, b, (((1,), (1,)), ((), ())), preferred_element_type=F32, precision=HIGHEST)


def _rms(x, gain):
    return x * lax.rsqrt(jnp.mean(x * x, axis=-1, keepdims=True) + RMS_EPS) * gain


def _sigmoid(x):
    return 1.0 / (1.0 + jnp.exp(-x))


def _silu(x):
    return x * _sigmoid(x)


def _softplus(x):
    return jnp.maximum(x, 0.0) + jnp.log(1.0 + jnp.exp(-jnp.abs(x)))


def _nm_kernel(*refs, has_norm, has_res):
    it = iter(refs)
    x_ref = next(it)
    g_ref = next(it) if has_norm else None
    w_ref = next(it)
    r_ref = next(it) if has_res else None
    o_ref = next(it)
    xn_ref = next(it)

    @pl.when(pl.program_id(1) == 0)
    def _():
        x = x_ref[...].astype(F32)
        if has_norm:
            x = _rms(x, g_ref[...])
        xn_ref[...] = x.astype(BF16)

    acc = jnp.dot(xn_ref[...], w_ref[...], preferred_element_type=F32)
    if has_res:
        acc = acc + r_ref[...]
    o_ref[...] = acc.astype(o_ref.dtype)


def norm_matmul(x, w, gain=None, residual=None, tm=512, tn=512, out_dtype=F32, x_col=0):
    m = x.shape[0]
    k, n = w.shape
    tn = min(tn, n)
    assert m % tm == 0 and n % tn == 0 and x_col % k == 0
    in_specs = [pl.BlockSpec((tm, k), lambda i, j: (i, x_col // k))]
    args = [x]
    if gain is not None:
        in_specs.append(pl.BlockSpec((1, k), lambda i, j: (0, 0)))
        args.append(gain.reshape(1, k).astype(F32))
    in_specs.append(pl.BlockSpec((k, tn), lambda i, j: (0, j)))
    args.append(w)
    if residual is not None:
        in_specs.append(pl.BlockSpec((tm, tn), lambda i, j: (i, j)))
        args.append(residual)
    return pl.pallas_call(
        functools.partial(_nm_kernel, has_norm=gain is not None, has_res=residual is not None),
        grid=(m // tm, n // tn),
        in_specs=in_specs,
        out_specs=pl.BlockSpec((tm, tn), lambda i, j: (i, j)),
        out_shape=jax.ShapeDtypeStruct((m, n), out_dtype),
        scratch_shapes=[pltpu.VMEM((tm, k), BF16)],
        compiler_params=_params("parallel", "arbitrary"),
        name="norm_matmul",
    )(*args)


def _out_proj_kernel(a_ref, b_ref, wa_ref, wb_ref, r_ref, o_ref):
    o_ref[...] = r_ref[...] + _bdot(a_ref[...], wa_ref[...]) + _bdot(b_ref[...], wb_ref[...])


def out_proj(a, b, w, residual, tm=512, tn=512):
    m, ka = a.shape
    n = w.shape[1]
    assert b.shape[1] == ka and w.shape[0] == 2 * ka
    return pl.pallas_call(
        _out_proj_kernel,
        grid=(m // tm, n // tn),
        in_specs=[
            pl.BlockSpec((tm, ka), lambda i, j: (i, 0)),
            pl.BlockSpec((tm, ka), lambda i, j: (i, 0)),
            pl.BlockSpec((ka, tn), lambda i, j: (0, j)),
            pl.BlockSpec((ka, tn), lambda i, j: (1, j)),
            pl.BlockSpec((tm, tn), lambda i, j: (i, j)),
        ],
        out_specs=pl.BlockSpec((tm, tn), lambda i, j: (i, j)),
        out_shape=jax.ShapeDtypeStruct((m, n), F32),
        compiler_params=_params("parallel", "arbitrary"),
        name="out_proj",
    )(a, b, w, w, residual)


def _ple_kernel(x_ref, yt_ref, g_ref, wg_ref, p_ref, wp_ref, o_ref, xn_ref, xs_ref):
    j = pl.program_id(1)
    tn = o_ref.shape[1]

    @pl.when(j == 0)
    def _():
        x = x_ref[...] + yt_ref[...].T
        xn_ref[...] = _rms(x, g_ref[...]).astype(BF16)
        for jj in range(xs_ref.shape[0]):
            xs_ref[jj] = x[:, jj * tn:(jj + 1) * tn]

    gate = _sigmoid(jnp.dot(xn_ref[...], wg_ref[...], preferred_element_type=F32))
    emb = _bdot(p_ref[...], wp_ref[...])
    o_ref[...] = xs_ref[j] + gate * emb


def ple_update(x, yt, p_i, norm_g, w_gate, w_proj, tm=512, tn=512):
    m, d = x.shape
    pd = p_i.shape[1]
    return pl.pallas_call(
        _ple_kernel,
        grid=(m // tm, d // tn),
        in_specs=[
            pl.BlockSpec((tm, d), lambda i, j: (i, 0)),
            pl.BlockSpec((d, tm), lambda i, j: (0, i)),
            pl.BlockSpec((1, d), lambda i, j: (0, 0)),
            pl.BlockSpec((d, tn), lambda i, j: (0, j)),
            pl.BlockSpec((tm, pd), lambda i, j: (i, 0)),
            pl.BlockSpec((pd, tn), lambda i, j: (0, j)),
        ],
        out_specs=pl.BlockSpec((tm, tn), lambda i, j: (i, j)),
        out_shape=jax.ShapeDtypeStruct((m, d), F32),
        scratch_shapes=[pltpu.VMEM((tm, d), BF16), pltpu.VMEM((d // tn, tm, tn), F32)],
        compiler_params=_params("parallel", "arbitrary"),
        name="ple_update",
    )(x, yt, norm_g.reshape(1, d), w_gate, p_i, w_proj)


def _rmsnorm_kernel(x_ref, g_ref, o_ref):
    o_ref[...] = _rms(x_ref[...], g_ref[...])


def rmsnorm(x, gain, tm=512):
    m, d = x.shape
    return pl.pallas_call(
        _rmsnorm_kernel,
        grid=(m // tm,),
        in_specs=[pl.BlockSpec((tm, d), lambda i: (i, 0)), pl.BlockSpec((1, d), lambda i: (0, 0))],
        out_specs=pl.BlockSpec((tm, d), lambda i: (i, 0)),
        out_shape=jax.ShapeDtypeStruct((m, d), F32),
        compiler_params=_params("parallel"),
        name="rmsnorm",
    )(x, gain.reshape(1, d))


def _peer_fold_kernel(keys_ref, wq_ref, o_ref):
    o_ref[0, 0] = _hdot_nt(keys_ref[0, 0], wq_ref[...])


def peer_fold(w_q, keys):
    hk = PEER_QDIM // 2
    return pl.pallas_call(
        _peer_fold_kernel,
        grid=(2, PEER_HEADS),
        in_specs=[
            pl.BlockSpec((1, 1, PEER_NKEYS, hk), lambda c, h: (h, c, 0, 0)),
            pl.BlockSpec((D_MODEL, hk), lambda c, h: (0, h * 2 + c)),
        ],
        out_specs=pl.BlockSpec((1, 1, PEER_NKEYS, D_MODEL), lambda c, h: (c, h, 0, 0)),
        out_shape=jax.ShapeDtypeStruct((2, PEER_HEADS, PEER_NKEYS, D_MODEL), F32),
        compiler_params=_params("parallel", "parallel"),
        name="peer_fold",
    )(keys, w_q)


def _rmsnorm_t_kernel(x_ref, g_ref, o_ref):
    o_ref[...] = _rms(x_ref[...], g_ref[...]).T.astype(BF16)


def rmsnorm_t(x, gain, tm=512):
    m, d = x.shape
    return pl.pallas_call(
        _rmsnorm_t_kernel,
        grid=(m // tm,),
        in_specs=[pl.BlockSpec((tm, d), lambda i: (i, 0)), pl.BlockSpec((1, d), lambda i: (0, 0))],
        out_specs=pl.BlockSpec((d, tm), lambda i: (0, i)),
        out_shape=jax.ShapeDtypeStruct((d, m), BF16),
        compiler_params=_params("parallel"),
        name="rmsnorm_t",
    )(x, gain.reshape(1, d))


def _sort_desc(v):
    v = list(v)
    n = len(v)
    k = 2
    while k <= n:
        j = k // 2
        while j >= 1:
            for i in range(n):
                l = i ^ j
                if l > i:
                    hi, lo = jnp.maximum(v[i], v[l]), jnp.minimum(v[i], v[l])
                    v[i], v[l] = (hi, lo) if (i & k) == 0 else (lo, hi)
            j //= 2
        k *= 2
    return v


def _merge_top(a, b):
    n = len(a)
    v = [jnp.maximum(a[i], b[n - 1 - i]) for i in range(n)]
    j = n // 2
    while j >= 1:
        for i in range(n):
            l = i ^ j
            if l > i:
                v[i], v[l] = jnp.maximum(v[i], v[l]), jnp.minimum(v[i], v[l])
        j //= 2
    return v


def _top_sorted(vals, n):
    vals = list(vals)
    while len(vals) % n:
        vals.append(jnp.full_like(vals[0], -jnp.inf))
    acc = _sort_desc(vals[:n])
    for g in range(1, len(vals) // n):
        acc = _merge_top(acc, _sort_desc(vals[g * n:(g + 1) * n]))
    return acc


def _count_leading(pred, b):
    t0 = pred(b[15])
    t1 = pred(b[7])
    t2 = pred(jnp.where(t1, b[11], b[3]))
    t3 = pred(jnp.where(t1, jnp.where(t2, b[13], b[9]), jnp.where(t2, b[5], b[1])))
    hi = jnp.where(t2, jnp.where(t3, b[14], b[12]), jnp.where(t3, b[10], b[8]))
    lo = jnp.where(t2, jnp.where(t3, b[6], b[4]), jnp.where(t3, b[2], b[0]))
    t4 = pred(jnp.where(t1, hi, lo))
    cnt = (jnp.where(t1, 8.0, 0.0) + jnp.where(t2, 4.0, 0.0)) + (jnp.where(t3, 2.0, 0.0) + jnp.where(t4, 1.0, 0.0))
    return jnp.where(t0, 16.0, cnt)


_PEER_PAIRS = [(i, j) for i in range(PEER_TOPK) for j in range(PEER_TOPK) if (i + 1) * (j + 1) <= PEER_TOPK]


def _peer_select_kernel(wf_ref, ht_ref, e1_ref, n1_ref, r2_ref, e2_ref, sub_ref):
    nk, k = PEER_NKEYS, PEER_TOPK
    tt = ht_ref.shape[1]
    ht = ht_ref[...]
    for c in range(2):
        sub_ref[c] = jnp.dot(wf_ref[c], ht, preferred_element_type=F32)
    row = lax.broadcasted_iota(jnp.int32, (SUBLANES, LANES), 0)

    def head_row(v, h):
        return jnp.sum(jnp.where(row == h, v, 0.0), axis=0, keepdims=True)

    def lane_group(lg, carry):
        lanes = pl.ds(pl.multiple_of(lg * LANES, LANES), LANES)

        def top_of_head(h, packed, c):
            base = pl.multiple_of(h * nk, nk)
            slabs = [sub_ref[c, pl.ds(base + SUBLANES * j, SUBLANES), lanes] for j in range(nk // SUBLANES)]
            top = _sort_desc(slabs)
            for sh in (4, 2, 1):
                top = _merge_top(top, [pltpu.roll(t, sh, 0) for t in top])
            return tuple(jnp.where(row == h, top[i], packed[i]) for i in range(k))

        zero = tuple(jnp.zeros((SUBLANES, LANES), F32) for _ in range(k))
        a = lax.fori_loop(0, PEER_HEADS, functools.partial(top_of_head, c=0), zero)
        b = lax.fori_loop(0, PEER_HEADS, functools.partial(top_of_head, c=1), zero)
        best = _top_sorted([a[i] + b[j] for i, j in _PEER_PAIRS], k)
        thr, vmax = best[k - 1], best[0]
        z = jnp.zeros((SUBLANES, LANES), F32)
        for i in range(k):
            z = z + jnp.exp(best[i] - vmax)
        inv_z = 1.0 / z

        def emit(h, carry):
            base = pl.multiple_of(h * nk, nk)
            s1 = sub_ref[0, pl.ds(base, nk), lanes]
            s2 = sub_ref[1, pl.ds(base, nk), lanes]
            thr_h = head_row(thr, h)
            bh = [head_row(b[j], h) for j in range(k)]
            n1 = _count_leading(lambda x: s1 + x >= thr_h, bh)
            r2 = _count_leading(lambda x: x > s2, bh)
            e1_ref[h, :, lanes] = jnp.exp(s1 - head_row(a[0], h)) * head_row(inv_z, h)
            n1_ref[h, :, lanes] = n1
            r2_ref[h, :, lanes] = r2.astype(BF16)
            e2_ref[h, :, lanes] = jnp.exp(s2 - head_row(b[0], h)).astype(BF16)
            return carry

        lax.fori_loop(0, PEER_HEADS, emit, 0)
        return carry

    lax.fori_loop(0, tt // LANES, lane_group, 0)


def peer_select(wf, ht, tt=256):
    d, s = ht.shape
    nrow = PEER_HEADS * PEER_NKEYS
    shape = (PEER_HEADS, PEER_NKEYS, s)
    ospec = pl.BlockSpec((PEER_HEADS, PEER_NKEYS, tt), lambda i: (0, 0, i))
    return pl.pallas_call(
        _peer_select_kernel,
        grid=(s // tt,),
        in_specs=[pl.BlockSpec((2, nrow, d), lambda i: (0, 0, 0)), pl.BlockSpec((d, tt), lambda i: (0, i))],
        out_specs=[ospec] * 4,
        out_shape=[jax.ShapeDtypeStruct(shape, F32)] * 2 + [jax.ShapeDtypeStruct(shape, BF16)] * 2,
        scratch_shapes=[pltpu.VMEM((2, nrow, tt), F32)],
        compiler_params=_params("parallel"),
        name="peer_select",
    )(wf, ht)


def _gelu_tanh(x):
    return 0.5 * x * (1.0 + jnp.tanh(math.sqrt(2.0 / math.pi) * (x + 0.044715 * (x * x * x))))


def _peer_dense_kernel(u_ref, ht_ref, vt_ref, e1_ref, n1_ref, r2_ref, e2_ref, o_ref, act0_ref, act1_ref, ga_ref):
    nk = PEER_NKEYS
    j = pl.program_id(1)
    eb = u_ref.shape[0]
    nblk = eb // nk

    @pl.when(j == 0)
    def _():
        o_ref[...] = jnp.zeros_like(o_ref)
        act1_ref[...] = jnp.zeros_like(act1_ref)

    done = jnp.maximum(j - 1, 0)

    def step(prev_ref, next_ref):
        for ii in range(nblk):
            i1 = done * nblk + ii
            act = _gelu_tanh(prev_ref[pl.ds(ii * nk, nk), :])
            gate = None
            for h in range(PEER_HEADS):
                n1 = n1_ref[h, pl.ds(i1, 1), :].astype(BF16)
                e1 = e1_ref[h, pl.ds(i1, 1), :].astype(BF16)
                g = jnp.where(r2_ref[h] < n1, e2_ref[h] * e1, jnp.zeros((), BF16))
                gate = g if gate is None else gate + g
            ga_ref[pl.ds(ii * nk, nk), :] = (gate.astype(F32) * act).astype(BF16)
        next_ref[...] = jnp.dot(u_ref[...], ht_ref[...], preferred_element_type=F32)
        o_ref[...] += jnp.dot(vt_ref[...], ga_ref[...], preferred_element_type=F32)

    pl.when(j % 2 == 0)(functools.partial(step, act1_ref, act0_ref))
    pl.when(j % 2 == 1)(functools.partial(step, act0_ref, act1_ref))


PEER_EB = 1024


def peer_dense(u_all, ht, vt_all, layer, e1, n1, r2, e2, tt=512):
    d, s = ht.shape
    ne, eb = u_all.shape[1], PEER_EB
    last = ne // eb - 1
    gspec = pl.BlockSpec((PEER_HEADS, PEER_NKEYS, tt), lambda i, j: (0, 0, i))
    return pl.pallas_call(
        _peer_dense_kernel,
        grid=(s // tt, ne // eb + 1),
        in_specs=[
            pl.BlockSpec((None, eb, d), lambda i, j: (layer, jnp.minimum(j, last), 0)),
            pl.BlockSpec((d, tt), lambda i, j: (0, i)),
            pl.BlockSpec((None, None, d, eb), lambda i, j: (layer, jnp.maximum(j - 1, 0), 0, 0)),
            gspec, gspec, gspec, gspec,
        ],
        out_specs=pl.BlockSpec((d, tt), lambda i, j: (0, i)),
        out_shape=jax.ShapeDtypeStruct((d, s), F32),
        scratch_shapes=[pltpu.VMEM((eb, tt), F32), pltpu.VMEM((eb, tt), F32), pltpu.VMEM((eb, tt), BF16)],
        compiler_params=_params("parallel", "arbitrary"),
        name="peer_dense",
    )(u_all, ht, vt_all, e1, n1, r2, e2)


def prep_peer_tables(peer_u, peer_v):
    nl, ne, d = peer_v.shape
    vt_all = peer_v.astype(BF16).reshape(nl, ne // PEER_EB, PEER_EB, d).transpose(0, 1, 3, 2)
    return peer_u.astype(BF16), vt_all


def peer(x, ln_g, w_q, keys, u_all, vt_all, layer, tm=512, tt=256):
    wf = peer_fold(w_q, keys).reshape(2, PEER_HEADS * PEER_NKEYS, D_MODEL).astype(BF16)
    ht = rmsnorm_t(x, ln_g, tm=tm)
    return peer_dense(u_all, ht, vt_all, layer, *peer_select(wf, ht, tt=tt), tt=tm)


def _mla_kernel(qi_ref, ki_ref, qn_ref, qr_ref, qrot_ref, cq_ref, sq_ref, kn_ref, kr_ref, krot_ref, ck_ref, sk_ref,
                vt_ref, o_ref, q1_ref, q2_ref, m_ref, l_ref, acc_ref, *, scale):
    t = pl.program_id(1)
    qi, ki = qi_ref[t], ki_ref[t]
    tq, tk = qn_ref.shape[1], kn_ref.shape[0]
    ratio = tq // tk

    @pl.when(ki == 0)
    def _():
        q1_ref[...] = (qn_ref[...] * scale).astype(BF16)
        q2_ref[...] = ((qr_ref[...] * cq_ref[...] + qrot_ref[...] * sq_ref[...]) * scale).astype(BF16)
        m_ref[...] = jnp.full_like(m_ref, -jnp.inf)
        l_ref[...] = jnp.zeros_like(l_ref)
        acc_ref[...] = jnp.zeros_like(acc_ref)

    def step(masked):
        kr = (kr_ref[...] * ck_ref[...] + krot_ref[...] * sk_ref[...]).astype(BF16)
        s = (jnp.dot(kn_ref[...], q1_ref[...], preferred_element_type=F32)
             + jnp.dot(kr, q2_ref[...], preferred_element_type=F32))
        if masked:
            kpos = ki * tk + lax.broadcasted_iota(jnp.int32, (tk, tq), 0)
            qpos = qi * tq + lax.broadcasted_iota(jnp.int32, (tk, tq), 1)
            s = jnp.where(kpos <= qpos, s, -jnp.inf)
        m_old = m_ref[...]
        m_new = jnp.maximum(m_old, jnp.max(s, axis=0, keepdims=True))
        alpha = jnp.exp(m_old - m_new)
        p = jnp.exp(s - m_new)
        l_ref[...] = alpha * l_ref[...] + jnp.sum(p, axis=0, keepdims=True)
        acc_ref[...] = alpha * acc_ref[...] + jnp.dot(vt_ref[...], p.astype(BF16), preferred_element_type=F32)
        m_ref[...] = m_new

    pl.when(ki < qi * ratio)(functools.partial(step, False))
    pl.when(ki >= qi * ratio)(functools.partial(step, True))

    @pl.when(ki == (qi + 1) * ratio - 1)
    def _():
        o_ref[...] = acc_ref[...] / l_ref[...]


def mla_attention(qt, kv, vt, kr, krot, cos, sin, cos_t, sin_t, tq=1024, tk=512):
    s = kv.shape[0]
    tq, tk = min(tq, s), min(tk, s)
    ratio = tq // tk
    hh, dn, dr = MLA_HEADS, MLA_NOPE, MLA_ROPE
    pairs = [(qi, ki) for qi in range(s // tq) for ki in range((qi + 1) * ratio)]
    qi_tab = jnp.array([pr[0] for pr in pairs], jnp.int32)
    ki_tab = jnp.array([pr[1] for pr in pairs], jnp.int32)
    r0 = hh * dn // dr
    qmap = lambda h, t, qt_, kt_: (0, qt_[t])
    kmap = lambda h, t, qt_, kt_: (kt_[t], 0)
    grid_spec = pltpu.PrefetchScalarGridSpec(
        num_scalar_prefetch=2,
        grid=(hh, len(pairs)),
        in_specs=[
            pl.BlockSpec((dn, tq), lambda h, t, qt_, kt_: (h, qt_[t])),
            pl.BlockSpec((dr, tq), lambda h, t, qt_, kt_: (r0 + h, qt_[t])),
            pl.BlockSpec((dr, tq), lambda h, t, qt_, kt_: (r0 + hh + h, qt_[t])),
            pl.BlockSpec((dr, tq), qmap),
            pl.BlockSpec((dr, tq), qmap),
            pl.BlockSpec((tk, dn), lambda h, t, qt_, kt_: (kt_[t], h)),
            pl.BlockSpec((tk, dr), kmap),
            pl.BlockSpec((tk, dr), kmap),
            pl.BlockSpec((tk, dr), kmap),
            pl.BlockSpec((tk, dr), kmap),
            pl.BlockSpec((MLA_V, tk), lambda h, t, qt_, kt_: (h, kt_[t])),
        ],
        out_specs=pl.BlockSpec((MLA_V, tq), lambda h, t, qt_, kt_: (h, qt_[t])),
        scratch_shapes=[pltpu.VMEM((dn, tq), BF16), pltpu.VMEM((dr, tq), BF16), pltpu.VMEM((1, tq), F32),
                        pltpu.VMEM((1, tq), F32), pltpu.VMEM((MLA_V, tq), F32)],
    )
    return pl.pallas_call(
        functools.partial(_mla_kernel, scale=(MLA_NOPE + MLA_ROPE) ** -0.5),
        grid_spec=grid_spec,
        out_shape=jax.ShapeDtypeStruct((hh * MLA_V, s), F32),
        compiler_params=_params("parallel", "arbitrary"),
        name="mla_attention",
    )(qi_tab, ki_tab, qt, qt, qt, cos_t, sin_t, kv, kr, krot, cos, sin, vt)


def _causal_conv(cur_ref, prev_ref, w_ref, first, bias=None):
    prev = jnp.where(first, 0.0, prev_ref[...])
    xe = jnp.concatenate([prev, cur_ref[...]], axis=0)
    w = w_ref[...]
    acc = w[CONV_WIDTH - 1:CONV_WIDTH] * xe[SUBLANES:]
    for j in range(CONV_WIDTH - 1):
        acc = acc + w[j:j + 1] * pltpu.roll(xe, CONV_WIDTH - 1 - j, 0)[SUBLANES:]
    return acc if bias is None else acc + bias


def _unit_lower_inverses(lows, c):
    n = lows[0].shape[0]
    r = lax.broadcasted_iota(jnp.int32, (n, n), 0)
    q = lax.broadcasted_iota(jnp.int32, (n, n), 1)
    eye = (r == q).astype(F32)
    prev = [jnp.where((r // 16) == (q // 16), low, 0.0) for low in lows]
    ps = [-d for d in prev]
    xs = [eye + p for p in ps]
    for _ in range(3):
        ps = [_bdot(p, p) for p in ps]
        xs = [x + _bdot(x, p) for x, p in zip(xs, ps)]
    size = 32
    while size <= c:
        cur = [jnp.where((r // size) == (q // size), low, 0.0) for low in lows] if size < c else lows
        ts = [_bdot(x, cu - pr) for x, cu, pr in zip(xs, cur, prev)]
        xs = [x - _bdot(t, x) for x, t in zip(xs, ts)]
        prev = cur
        size *= 2
    return xs


def _gdn_kernel(q_ref, qp_ref, k_ref, kp_ref, v_ref, vp_ref, z_ref, wq_ref, wk_ref, wv_ref,
                ar_ref, br_ref, alog_ref, dtb_ref, ng_ref, o_ref, st_ref):
    c = GDN_CHUNK
    tt = q_ref.shape[0]
    first = pl.program_id(1) == 0

    @pl.when(first)
    def _():
        st_ref[...] = jnp.zeros_like(st_ref)

    def l2n(x):
        return x * lax.rsqrt(jnp.sum(x * x, axis=-1, keepdims=True) + 1e-6)

    q = l2n(_silu(_causal_conv(q_ref, qp_ref, wq_ref, first))) * (GDN_DK ** -0.5)
    k = l2n(_silu(_causal_conv(k_ref, kp_ref, wk_ref, first)))
    v = _silu(_causal_conv(v_ref, vp_ref, wv_ref, first))
    neg_a = -jnp.exp(alog_ref[0, :, 0:1])
    dtb = dtb_ref[0, :, 0:1]
    nb = 2 * c
    r = lax.broadcasted_iota(jnp.int32, (nb, nb), 0)
    cc = lax.broadcasted_iota(jnp.int32, (nb, nb), 1)
    same = (r // c) == (cc // c)
    incl, strict = same & (r >= cc), same & (r > cc)
    incl_t = same & (r <= cc)
    top = lax.broadcasted_iota(jnp.int32, (nb, 1), 0) < c
    blocks = range(tt // nb)
    rows = [slice(n * nb, (n + 1) * nb) for n in blocks]
    as_col = lambda row: jnp.sum(jnp.where(r == cc, row, 0.0), axis=1, keepdims=True)
    g_row = [neg_a * _softplus(ar_ref[0, n] + dtb) for n in blocks]
    g_col = [as_col(g) for g in g_row]
    beta = [as_col(_sigmoid(br_ref[0, n])) for n in blocks]
    gc = [jnp.sum(jnp.where(incl, g, 0.0), axis=1, keepdims=True) for g in g_row]
    gr = [jnp.sum(jnp.where(incl_t, g, 0.0), axis=0, keepdims=True) for g in g_col]
    decay = [jnp.where(incl, jnp.exp(jnp.where(incl, a - b, 0.0)), 0.0) for a, b in zip(gc, gr)]
    kk = [_bdot_nt(k[sl], k[sl]) for sl in rows]
    qk = [_bdot_nt(q[sl], k[sl]) for sl in rows]
    inv = _unit_lower_inverses([jnp.where(strict, b * m * d, 0.0) for b, m, d in zip(beta, kk, decay)], c)
    eg = [jnp.exp(g) for g in gc]
    sol = [_bdot(x, jnp.concatenate([b * v[sl], (b * e) * k[sl]], axis=1)) for x, b, e, sl in zip(inv, beta, eg, rows)]
    a_qk = [m * d for m, d in zip(qk, decay)]
    k_end = [k[sl] * jnp.exp(jnp.where(top, g[c - 1:c], g[nb - 1:nb]) - g) for g, sl in zip(gc, rows)]
    k_end_t = [[ke[:c].T, ke[c:].T] for ke in k_end]
    q_dec = [q[sl] * e for e, sl in zip(eg, rows)]
    state = st_ref[...]
    us, o_state = [], []
    for n in blocks:
        for j in range(2):
            cs = slice(j * c, (j + 1) * c)
            u = sol[n][cs, :GDN_DV] - _bdot(sol[n][cs, GDN_DV:], state)
            o_state.append(_bdot(q_dec[n][cs], state))
            state = jnp.exp(gc[n][(j + 1) * c - 1:(j + 1) * c]) * state + _bdot(k_end_t[n][j], u)
            us.append(u)
    st_ref[...] = state
    o_loc = [_bdot(a, jnp.concatenate(us[2 * n:2 * n + 2], axis=0)) for n, a in zip(blocks, a_qk)]
    o = jnp.concatenate(o_state, axis=0) + jnp.concatenate(o_loc, axis=0)
    o_ref[...] = _rms(o, ng_ref[...]) * _silu(z_ref[...])


def gated_delta_net(proj, col0, a_raw, b_raw, conv_w, a_log, dt_bias, norm_g, tt=256):
    s = proj.shape[0]
    tt = min(tt, s)
    hh, c = GDN_HEADS, GDN_CHUNK
    b0 = col0 // LANES
    nblk = GDN_HEADS * GDN_DK // LANES

    def cur(g):
        return pl.BlockSpec((tt, LANES), lambda h, i: (i, b0 + g * nblk + h))

    def prev(g):
        return pl.BlockSpec((SUBLANES, LANES), lambda h, i: (jnp.maximum(i * (tt // SUBLANES) - 1, 0), b0 + g * nblk + h))

    def wspec(g):
        return pl.BlockSpec((CONV_WIDTH, LANES), lambda h, i: (0, g * nblk + h))

    a_t, b_t = a_raw.T, b_raw.T
    c = 2 * c
    rowspec = pl.BlockSpec((1, tt // c, 1, c), lambda h, i: (h, i, 0, 0))
    hspec = pl.BlockSpec((1, 1, LANES), lambda h, i: (h, 0, 0))
    bcast = lambda p: jnp.broadcast_to(p.astype(F32)[:, None, None], (hh, 1, LANES))
    return pl.pallas_call(
        _gdn_kernel,
        grid=(hh, s // tt),
        in_specs=[cur(0), prev(0), cur(1), prev(1), cur(2), prev(2), cur(3), wspec(0), wspec(1), wspec(2),
                  rowspec, rowspec, hspec, hspec, pl.BlockSpec((1, GDN_DV), lambda h, i: (0, 0))],
        out_specs=pl.BlockSpec((tt, GDN_DV), lambda h, i: (i, h)),
        out_shape=jax.ShapeDtypeStruct((s, hh * GDN_DV), F32),
        scratch_shapes=[pltpu.VMEM((GDN_DK, GDN_DV), F32)],
        compiler_params=_params("parallel", "arbitrary"),
        name="gated_delta_net",
    )(proj, proj, proj, proj, proj, proj, proj, conv_w, conv_w, conv_w,
      a_t.reshape(hh, s // c, 1, c), b_t.reshape(hh, s // c, 1, c),
      bcast(a_log), bcast(dt_bias), norm_g.reshape(1, GDN_DV).astype(F32))


def _rot_half_cols(w, half):
    return jnp.concatenate([-w[..., half:], w[..., :half]], axis=-1)


def _pad_cols(w, n):
    return jnp.pad(w, ((0, 0), (0, n - w.shape[1])))


AB_QKVZ = MLA_Q_RANK + MLA_KV_RANK + 2 * MLA_ROPE
AB_GATES = AB_QKVZ + 4 * GDN_HEADS * GDN_DK
AB_PAD = AB_GATES + LANES


def prep_ab(w_in, w_uq, w_ukv, w_out):
    rq, rkv, rr = MLA_Q_RANK, MLA_KV_RANK, MLA_ROPE
    w_kr = w_in[:, rq + rkv:rq + rkv + rr]
    w_main = jnp.concatenate([w_in[:, :rq + rkv], w_kr, _rot_half_cols(w_kr, rr // 2),
                              w_in[:, rq + rkv + rr:]], axis=1)
    w_main = _pad_cols(w_main, AB_PAD).astype(BF16)
    uq = w_uq.reshape(rq, MLA_HEADS, MLA_NOPE + MLA_ROPE)
    uq_r = uq[..., MLA_NOPE:]
    uq2 = jnp.concatenate([uq[..., :MLA_NOPE].reshape(rq, -1), uq_r.reshape(rq, -1),
                           _rot_half_cols(uq_r, rr // 2).reshape(rq, -1)], axis=1).astype(BF16)
    ukv = w_ukv.reshape(rkv, MLA_HEADS, MLA_NOPE + MLA_V)
    ukv2 = jnp.concatenate([ukv[..., :MLA_NOPE].reshape(rkv, -1), ukv[..., MLA_NOPE:].reshape(rkv, -1)], axis=1).astype(BF16)
    return w_main, uq2, ukv2, w_out.astype(BF16)


def mixer_ab(x, ln, w_main, uq2, ukv2, w_out, q_norm, kv_norm, conv_w, a_log, dt_bias, gdn_norm, cos2, sin2,
             tm=512, t_attn=512, t_gdn=256):
    s = x.shape[0]
    rq, rkv, rr = MLA_Q_RANK, MLA_KV_RANK, MLA_ROPE
    proj = norm_matmul(x, w_main, gain=ln, tm=min(2 * tm, x.shape[0]))
    qfull = norm_matmul(proj, uq2, gain=q_norm, tm=tm)
    kv = norm_matmul(proj, ukv2, gain=kv_norm, tm=tm, out_dtype=BF16, x_col=rq)
    nn = MLA_HEADS * MLA_NOPE
    o_a = mla_attention(qfull.T, kv, kv[:, nn:].T, proj[:, rq + rkv:rq + rkv + rr], proj[:, rq + rkv + rr:AB_QKVZ],
                        cos2, sin2, cos2.T, sin2.T, tq=2 * t_attn, tk=t_attn).T
    o_b = gated_delta_net(proj, AB_QKVZ, proj[:, AB_GATES:AB_GATES + GDN_HEADS],
                          proj[:, AB_GATES + GDN_HEADS:AB_GATES + 2 * GDN_HEADS], conv_w, a_log, dt_bias, gdn_norm, tt=t_gdn)
    return out_proj(o_a, o_b, w_out, x, tm=min(2 * tm, x.shape[0]))


CD_R, CD_K, CD_V, CD_Z, CD_DT, CD_X = (i * 1024 for i in range(6))
CD_B = 6144
CD_C = CD_B + SSD_GROUPS * SSD_STATE
CD_WA = CD_C + SSD_GROUPS * SSD_STATE
CD_G = CD_WA + LANES
CD_PAD = 7168
GROUP_W = SSD_INNER // SSD_GROUPS


def _lower_ones(c):
    r = lax.broadcasted_iota(jnp.int32, (c, c), 0)
    q = lax.broadcasted_iota(jnp.int32, (c, c), 1)
    return r >= q, r > q


def _ssd_kernel(x_ref, xp_ref, b_ref, bp_ref, c_ref, cp_ref, z_ref, dt_ref, wx_ref, wb_ref, wc_ref,
                bx_ref, bb_ref, bc_ref, dtb_ref, alog_ref, dskip_ref, ng_ref, o_ref, st_ref):
    c = SSD_CHUNK
    tt = x_ref.shape[0]
    first = pl.program_id(1) == 0

    @pl.when(first)
    def _():
        st_ref[...] = jnp.zeros_like(st_ref)

    xs_all = _silu(_causal_conv(x_ref, xp_ref, wx_ref, first, bx_ref[...]))
    bm_all = _silu(_causal_conv(b_ref, bp_ref, wb_ref, first, bb_ref[...]))
    cm_all = _silu(_causal_conv(c_ref, cp_ref, wc_ref, first, bc_ref[...]))
    dt_all = _softplus(dt_ref[...] + dtb_ref[...])
    a_all = -jnp.exp(alog_ref[...]) * dt_all
    incl, _ = _lower_ones(c)
    tri = incl.astype(F32)
    left = lax.broadcasted_iota(jnp.int32, (c, LANES), 1) < SSD_HEADDIM
    npair = GROUP_W // LANES
    outs = []
    for n in range(tt // c):
        sl = slice(n * c, (n + 1) * c)
        xs, bm, cm, dt = xs_all[sl], bm_all[sl], cm_all[sl], dt_all[sl]
        acs = _bdot(tri, a_all[sl])
        xdt = xs * dt
        cb = _bdot_nt(cm, bm)
        bm_t = bm.T
        ys = []
        for p in range(npair):
            ls = slice(p * LANES, (p + 1) * LANES)
            acs_p = acs[:, ls]
            acs_t = acs_p.T
            xp = xdt[:, ls]
            yd = []
            for hd in range(2):
                col = acs_p[:, hd * SSD_HEADDIM:hd * SSD_HEADDIM + 1]
                row = acs_t[hd * SSD_HEADDIM:hd * SSD_HEADDIM + 1, :]
                lmat = jnp.where(incl, jnp.exp(jnp.where(incl, col - row, 0.0)), 0.0)
                yd.append(_bdot(cb * lmat, xp))
            last = acs_p[c - 1:c]
            prev_t = st_ref[p]
            y_off = _bdot(cm, prev_t) * jnp.exp(acs_p)
            st_ref[p] = jnp.exp(last) * prev_t + _bdot(bm_t, xp * jnp.exp(last - acs_p))
            ys.append(jnp.where(left, yd[0], yd[1]) + y_off)
        outs.append(jnp.concatenate(ys, axis=1) + xs * dskip_ref[...])
    y = jnp.concatenate(outs, axis=0) * _silu(z_ref[...])
    o_ref[...] = _rms(y, ng_ref[...])


def mamba2_ssd(proj, conv_w, conv_b, dt_bias, a_log, d_skip, norm_g, tt=256):
    s = proj.shape[0]
    tt = min(tt, s)
    gw, ns = GROUP_W, SSD_STATE
    per = lambda v: jnp.repeat(v.astype(F32), SSD_HEADDIM).reshape(1, SSD_INNER)

    def cur(col, w):
        return pl.BlockSpec((tt, w), lambda g, i: (i, col // w + g))

    def prev(col, w):
        return pl.BlockSpec((SUBLANES, w), lambda g, i: (jnp.maximum(i * (tt // SUBLANES) - 1, 0), col // w + g))

    def par(rows, col, w):
        return pl.BlockSpec((rows, w), lambda g, i: (0, col // w + g))

    cb = conv_b.reshape(1, -1).astype(F32)
    return pl.pallas_call(
        _ssd_kernel,
        grid=(SSD_GROUPS, s // tt),
        in_specs=[cur(CD_X, gw), prev(CD_X, gw), cur(CD_B, ns), prev(CD_B, ns), cur(CD_C, ns), prev(CD_C, ns),
                  cur(CD_Z, gw), cur(CD_DT, gw),
                  par(CONV_WIDTH, 0, gw), par(CONV_WIDTH, SSD_INNER, ns), par(CONV_WIDTH, SSD_INNER + SSD_GROUPS * ns, ns),
                  par(1, 0, gw), par(1, SSD_INNER, ns), par(1, SSD_INNER + SSD_GROUPS * ns, ns),
                  par(1, 0, gw), par(1, 0, gw), par(1, 0, gw), par(1, 0, gw)],
        out_specs=pl.BlockSpec((tt, gw), lambda g, i: (i, g)),
        out_shape=jax.ShapeDtypeStruct((s, SSD_INNER), F32),
        scratch_shapes=[pltpu.VMEM((gw // LANES, ns, LANES), F32)],
        compiler_params=_params("parallel", "arbitrary"),
        name="mamba2_ssd",
    )(proj, proj, proj, proj, proj, proj, proj, proj, conv_w, conv_w, conv_w, cb, cb, cb,
      per(dt_bias), per(a_log), per(d_skip), norm_g.reshape(1, SSD_INNER).astype(F32))


def _pair_ones():
    r = lax.broadcasted_iota(jnp.int32, (LANES, LANES), 0)
    q = lax.broadcasted_iota(jnp.int32, (LANES, LANES), 1)
    return (r // RWKV_HEAD) == (q // RWKV_HEAD)


def _head_sums(x, ones):
    return jnp.concatenate([_hdot(x[:, i * LANES:(i + 1) * LANES], ones) for i in range(x.shape[1] // LANES)], axis=1)


def _rwkv_prep_kernel(r_ref, rp_ref, k_ref, kp_ref, v_ref, vp_ref, wa_ref, wap_ref, g0_ref, g0p_ref, g1_ref, g1p_ref,
                      mur_ref, muk_ref, muv_ref, muwa_ref, mug0_ref, mug1_ref, w0_ref, w2_ref, a0_ref, a2_ref, g2_ref,
                      kk_ref, ka_ref, rk_ref,
                      ro_ref, lw_ref, ko_ref, vo_ref, po_ref, qo_ref, go_ref, bo_ref):
    first = pl.program_id(0) == 0

    def mix(cur_ref, prev_ref, mu_ref):
        cur = cur_ref[...]
        prev = jnp.where(first, 0.0, prev_ref[...])
        shifted = pltpu.roll(jnp.concatenate([prev, cur], axis=0), 1, 0)[SUBLANES:]
        return cur + (shifted - cur) * mu_ref[...]

    r = mix(r_ref, rp_ref, mur_ref)
    k = mix(k_ref, kp_ref, muk_ref)
    v = mix(v_ref, vp_ref, muv_ref)
    wa = mix(wa_ref, wap_ref, muwa_ref)
    g0 = mix(g0_ref, g0p_ref, mug0_ref)
    g1 = mix(g1_ref, g1p_ref, mug1_ref)
    log_w = -math.exp(-0.5) * _sigmoid(w0_ref[...] + _bdot(jnp.tanh(wa), w2_ref[...]))
    a = _sigmoid(a0_ref[...] + _bdot(wa, a2_ref[...]))
    gate = _bdot(_sigmoid(g0), g2_ref[0:LANES, :]) + _bdot(_sigmoid(g1), g2_ref[LANES:2 * LANES, :])
    ones = _pair_ones().astype(F32)
    kx = k * kk_ref[...]
    kk = kx * lax.rsqrt(_head_sums(kx * kx, ones) + 1e-6)
    k_mod = k * (1.0 + (a - 1.0) * ka_ref[...])
    ro_ref[...] = r
    lw_ref[...] = log_w
    ko_ref[...] = k_mod
    vo_ref[...] = v
    po_ref[...] = -kk * a
    qo_ref[...] = kk
    go_ref[...] = gate
    bo_ref[...] = _head_sums(r * k_mod * rk_ref[...], ones) * v


def rwkv_prep(proj, mu, w0, w2, a0, a2, g2, k_k, k_a, r_k, tt=256):
    s = proj.shape[0]
    tt = min(tt, s)
    ri = RWKV_INNER
    row = lambda v: v.reshape(1, -1).astype(F32)
    mu_r, mu_k, mu_v = (row(mu[i * ri:(i + 1) * ri]) for i in range(3))
    mu_wa = row(mu[3 * ri:3 * ri + LANES])
    mu_g = row(jnp.pad(mu[3 * ri + LANES:], (0, 2 * LANES - RWKV_G_LORA)))
    zeros = jnp.zeros((RWKV_W_LORA, ri), F32)
    w2p = jnp.concatenate([w2, zeros], axis=0).astype(BF16)
    a2p = jnp.concatenate([zeros, a2], axis=0).astype(BF16)
    g2p = jnp.pad(g2, ((0, 2 * LANES - RWKV_G_LORA), (0, 0))).astype(BF16)

    def cur(col, w):
        return pl.BlockSpec((tt, w), lambda i: (i, col // w))

    def prev(col, w):
        return pl.BlockSpec((SUBLANES, w), lambda i: (jnp.maximum(i * (tt // SUBLANES) - 1, 0), col // w))

    full = lambda a: pl.BlockSpec(a.shape, lambda i: (0, 0))
    params = [mu_r, mu_k, mu_v, mu_wa, mu_g[:, :LANES], mu_g[:, LANES:], row(w0), w2p, row(a0), a2p, g2p,
              row(k_k), row(k_a), row(r_k)]
    out = jax.ShapeDtypeStruct((s, ri), F32)
    return pl.pallas_call(
        _rwkv_prep_kernel,
        grid=(s // tt,),
        in_specs=[cur(CD_R, ri), prev(CD_R, ri), cur(CD_K, ri), prev(CD_K, ri), cur(CD_V, ri), prev(CD_V, ri),
                  cur(CD_WA, LANES), prev(CD_WA, LANES), cur(CD_G, LANES), prev(CD_G, LANES),
                  cur(CD_G + LANES, LANES), prev(CD_G + LANES, LANES)] + [full(a) for a in params],
        out_specs=[pl.BlockSpec((tt, ri), lambda i: (i, 0))] * 8,
        out_shape=[out] * 8,
        compiler_params=_params("arbitrary"),
        name="rwkv_prep",
    )(*([proj] * 12), *params)


RWKV_MY_CHUNK = 64


def _rwkv_scan_kernel(r_ref, lw_ref, k_ref, v_ref, p_ref, q_ref, g_ref, b_ref, lnw_ref, lnb_ref, o_ref, st_ref):
    c = RWKV_MY_CHUNK
    tt = r_ref.shape[0]

    @pl.when(pl.program_id(1) == 0)
    def _():
        st_ref[...] = jnp.zeros_like(st_ref)

    tri = _lower_ones(c)[0].astype(F32)
    left = lax.broadcasted_iota(jnp.int32, (c, LANES), 1) < RWKV_HEAD
    pair = _pair_ones()
    ones = pair.astype(F32)
    r_id = lax.broadcasted_iota(jnp.int32, (LANES, LANES), 0)
    c_id = lax.broadcasted_iota(jnp.int32, (LANES, LANES), 1)
    eye = r_id == c_id
    top = r_id < c
    strict = pair & ((r_id % c) > (c_id % c))
    incl = pair & ((r_id % c) >= (c_id % c))
    stack2 = lambda a: jnp.concatenate([a, a], axis=0)
    by_head = lambda a: jnp.concatenate([jnp.where(left, a, 0.0), jnp.where(left, 0.0, a)], axis=0)
    unstack = lambda a: jnp.where(left, a[:c], a[c:])
    chunks = range(tt // c)
    rows = [slice(n * c, (n + 1) * c) for n in chunks]
    w = [lw_ref[sl, :] for sl in rows]
    v = [v_ref[sl, :] for sl in rows]
    lw = [_bdot(tri, x) for x in w]
    lam_in = [jnp.exp(x) for x in lw]
    inv_lam = [jnp.exp(-x) for x in lw]
    q_bar = [q_ref[sl, :] * jnp.exp(a - b) for sl, a, b in zip(rows, lw, w)]
    r_bar = [r_ref[sl, :] * x for sl, x in zip(rows, lam_in)]
    pk = [jnp.concatenate([p_ref[sl, :] * x, k_ref[sl, :] * x], axis=0) for sl, x in zip(rows, inv_lam)]
    mq = [_bdot_nt(by_head(a), b) for a, b in zip(q_bar, pk)]
    mr = [_bdot_nt(by_head(a), b) for a, b in zip(r_bar, pk)]
    mq_sw = [pltpu.roll(x, c, 1) for x in mq]
    mr_sw = [pltpu.roll(x, c, 1) for x in mr]
    m_qp = [jnp.where(strict, jnp.where(top, a, b), 0.0) for a, b in zip(mq, mq_sw)]
    m_qk = [jnp.where(strict, jnp.where(top, b, a), 0.0) for a, b in zip(mq, mq_sw)]
    m_rp = [jnp.where(incl, jnp.where(top, a, b), 0.0) for a, b in zip(mr, mr_sw)]
    m_rk = [jnp.where(incl, jnp.where(top, b, a), 0.0) for a, b in zip(mr, mr_sw)]
    vv = [stack2(x) for x in v]
    inv = _unit_lower_inverses([-x for x in m_qp], c)
    qkv = [_bdot(a, b) for a, b in zip(m_qk, vv)]
    sol = [_bdot(x, jnp.concatenate([stack2(a), b], axis=1)) for x, a, b in zip(inv, q_bar, qkv)]
    ws = [unstack(x[:, :LANES]) for x in sol]
    wv = [unstack(x[:, LANES:]) for x in sol]
    y_loc = [unstack(_bdot(a, b)) for a, b in zip(m_rk, vv)]
    lam_end = [x[c - 1:c] for x in lam_in]
    lam_col = [jnp.sum(jnp.where(eye, x, 0.0), axis=1, keepdims=True) for x in lam_end]
    pk_end_t = [(a * b).T for a, b in zip(pk, lam_end)]
    state = st_ref[...]
    us, y_state = [], []
    for n in chunks:
        u = _bdot(ws[n], state) + wv[n]
        y_state.append(_bdot(r_bar[n], state))
        upd = _bdot(pk_end_t[n], jnp.concatenate([u, v[n]], axis=0))
        state = lam_col[n] * state + jnp.where(pair, upd, 0.0)
        us.append(u)
    outs = [a + unstack(_bdot(b, stack2(u))) + d for a, b, u, d in zip(y_state, m_rp, us, y_loc)]
    st_ref[...] = state
    y = jnp.concatenate(outs, axis=0)
    mean = _hdot(y, ones) * (1.0 / RWKV_HEAD)
    yc = y - mean
    var = _hdot(yc * yc, ones) * (1.0 / RWKV_HEAD)
    y = yc * lax.rsqrt(var + RWKV_GN_EPS) * lnw_ref[...] + lnb_ref[...]
    o_ref[...] = (y + b_ref[...]) * g_ref[...]


def rwkv_scan(r, lw, k, v, p, q, gate, bonus, ln_w, ln_b, tt=256):
    s = r.shape[0]
    tt = min(tt, s)
    spec = pl.BlockSpec((tt, LANES), lambda h, i: (i, h))
    pspec = pl.BlockSpec((1, LANES), lambda h, i: (0, h))
    return pl.pallas_call(
        _rwkv_scan_kernel,
        grid=(RWKV_INNER // LANES, s // tt),
        in_specs=[spec] * 8 + [pspec] * 2,
        out_specs=spec,
        out_shape=jax.ShapeDtypeStruct((s, RWKV_INNER), F32),
        scratch_shapes=[pltpu.VMEM((LANES, LANES), F32)],
        compiler_params=_params("parallel", "arbitrary"),
        name="rwkv_scan",
    )(r, lw, k, v, p, q, gate, bonus, ln_w.reshape(1, -1).astype(F32), ln_b.reshape(1, -1).astype(F32))


def prep_cd(w_in, w_out):
    si, ri = SSD_INNER, RWKV_INNER
    z, xbc, dt, rw = w_in[:, :si], w_in[:, si:2 * si + 512], w_in[:, 2 * si + 512:2 * si + 528], w_in[:, 2 * si + 528:]
    dt_exp = jnp.repeat(dt, SSD_HEADDIM, axis=1)
    cols = [rw[:, :3 * ri], z, dt_exp, xbc, rw[:, 3 * ri:]]
    return _pad_cols(jnp.concatenate(cols, axis=1), CD_PAD).astype(BF16), w_out.astype(BF16)


def mixer_cd(x, ln, w_main, w_out, ssd_conv_w, ssd_conv_b, ssd_dt_bias, ssd_a_log, ssd_d, ssd_norm,
             mu, w0, w2, a0, a2, g2, k_k, k_a, r_k, ln_w, ln_b, tm=512, tt=256):
    proj = norm_matmul(x, w_main, gain=ln, tm=min(2 * tm, x.shape[0]))
    o_c = mamba2_ssd(proj, ssd_conv_w, ssd_conv_b, ssd_dt_bias, ssd_a_log, ssd_d, ssd_norm, tt=tt)
    o_d = rwkv_scan(*rwkv_prep(proj, mu, w0, w2, a0, a2, g2, k_k, k_a, r_k.reshape(-1), tt=tt), ln_w, ln_b, tt=2 * tt)
    return out_proj(o_c, o_d, w_out, x, tm=min(2 * tm, x.shape[0]))


def _rope_tables(s):
    inv = 1.0 / (ROPE_THETA ** (jnp.arange(0, MLA_ROPE, 2, dtype=F32) / MLA_ROPE))
    ang = jnp.arange(s, dtype=F32)[:, None] * inv[None, :]
    cos, sin = jnp.cos(ang), jnp.sin(ang)
    return jnp.concatenate([cos, cos], axis=1), jnp.concatenate([sin, sin], axis=1)


def kernel(x, p, ln_mix, ln_ffn, ab_w_in, mla_q_norm, mla_w_uq, mla_kv_norm, mla_w_ukv, gdn_conv_w, gdn_a_log, gdn_dt_bias, gdn_norm, ab_w_out, cd_w_in, ssd_conv_w, ssd_conv_b, ssd_dt_bias, ssd_a_log, ssd_d, ssd_norm, rwkv_mu, rwkv_w0, rwkv_w2, rwkv_a0, rwkv_a2, rwkv_g2, rwkv_k_k, rwkv_k_a, rwkv_r_k, rwkv_ln_w, rwkv_ln_b, cd_w_out, peer_w_q, peer_keys, peer_u, peer_v, ple_w_proj, ple_norm, ple_w_gate, final_norm):
    assert x.shape[0] == 1
    s = x.shape[1]
    tm = min(512, s)
    tt = min(256, s)
    cos2, sin2 = _rope_tables(s)
    u_all, vt_all = prep_peer_tables(peer_u, peer_v)
    xs = x[0]
    for i in range(DEPTH):
        j = i // 2
        if i % 2 == 0:
            wts = prep_ab(ab_w_in[j], mla_w_uq[j], mla_w_ukv[j], ab_w_out[j])
            xs = mixer_ab(xs, ln_mix[i], *wts, mla_q_norm[j], mla_kv_norm[j], gdn_conv_w[j], gdn_a_log[j],
                          gdn_dt_bias[j], gdn_norm[j], cos2, sin2, tm=tm, t_attn=tm, t_gdn=tm)
        else:
            wts = prep_cd(cd_w_in[j], cd_w_out[j])
            xs = mixer_cd(xs, ln_mix[i], *wts, ssd_conv_w[j], ssd_conv_b[j], ssd_dt_bias[j], ssd_a_log[j], ssd_d[j],
                          ssd_norm[j], rwkv_mu[j], rwkv_w0[j], rwkv_w2[j], rwkv_a0[j], rwkv_a2[j], rwkv_g2[j],
                          rwkv_k_k[j], rwkv_k_a[j], rwkv_r_k[j], rwkv_ln_w[j], rwkv_ln_b[j], tm=tm, tt=tt)
        yt = peer(xs, ln_ffn[i], peer_w_q[i], peer_keys[i], u_all, vt_all, i, tm=tm, tt=tt)
        xs = ple_update(xs, yt, p[i, 0], ple_norm[i], ple_w_gate[i].astype(BF16), ple_w_proj[i].astype(BF16), tm=tm)
    return rmsnorm(xs, final_norm, tm=tm)[None]
```

```python
import functools
import math

import jax
import jax.numpy as jnp
from jax import lax
from jax.experimental import pallas as pl
from jax.experimental.pallas import tpu as pltpu

F32 = jnp.float32
BF16 = jnp.bfloat16
HIGHEST = lax.Precision.HIGHEST

D_MODEL = 2048
DEPTH = 4
PLE_DIM = 256
RMS_EPS = 1e-6
MLA_HEADS = 8
MLA_Q_RANK = 512
MLA_KV_RANK = 256
MLA_NOPE = 128
MLA_ROPE = 64
MLA_V = 128
ROPE_THETA = 10000.0
GDN_HEADS = 8
GDN_DK = 128
GDN_DV = 128
GDN_CHUNK = 64
SSD_HEADS = 16
SSD_HEADDIM = 64
SSD_GROUPS = 2
SSD_STATE = 128
SSD_CHUNK = 128
SSD_INNER = SSD_HEADS * SSD_HEADDIM
RWKV_HEADS = 16
RWKV_HEAD = 64
RWKV_INNER = RWKV_HEADS * RWKV_HEAD
RWKV_W_LORA = 64
RWKV_A_LORA = 64
RWKV_G_LORA = 160
RWKV_GN_EPS = 64e-5
CONV_WIDTH = 4
PEER_HEADS = 8
PEER_NKEYS = 128
PEER_EXPERTS = PEER_NKEYS * PEER_NKEYS
PEER_QDIM = 256
PEER_TOPK = 16

LANES = 128
SUBLANES = 8
VMEM_LIMIT = 56 * 1024 * 1024


def _params(*sem):
    return pltpu.CompilerParams(dimension_semantics=sem, vmem_limit_bytes=VMEM_LIMIT)


def _bdot(a, b):
    return jnp.dot(a.astype(BF16), b.astype(BF16), preferred_element_type=F32)


def _bdot_nt(a, b):
    return lax.dot_general(a.astype(BF16), b.astype(BF16), (((1,), (1,)), ((), ())), preferred_element_type=F32)


def _hdot(a, b):
    return jnp.dot(a, b, preferred_element_type=F32, precision=HIGHEST)


def _hdot_nt(a, b):
    return lax.dot_general(a, b, (((1,), (1,)), ((), ())), preferred_element_type=F32, precision=HIGHEST)


def _rms(x, gain):
    return x * lax.rsqrt(jnp.mean(x * x, axis=-1, keepdims=True) + RMS_EPS) * gain


def _sigmoid(x):
    return 1.0 / (1.0 + jnp.exp(-x))


def _silu(x):
    return x * _sigmoid(x)


def _softplus(x):
    return jnp.maximum(x, 0.0) + jnp.log(1.0 + jnp.exp(-jnp.abs(x)))


def _nm_kernel(*refs, has_norm, has_res):
    it = iter(refs)
    x_ref = next(it)
    g_ref = next(it) if has_norm else None
    w_ref = next(it)
    r_ref = next(it) if has_res else None
    o_ref = next(it)
    xn_ref = next(it)

    @pl.when(pl.program_id(1) == 0)
    def _():
        x = x_ref[...].astype(F32)
        if has_norm:
            x = _rms(x, g_ref[...])
        xn_ref[...] = x.astype(BF16)

    acc = jnp.dot(xn_ref[...], w_ref[...], preferred_element_type=F32)
    if has_res:
        acc = acc + r_ref[...]
    o_ref[...] = acc.astype(o_ref.dtype)


def norm_matmul(x, w, gain=None, residual=None, tm=512, tn=512, out_dtype=F32, x_col=0):
    m = x.shape[0]
    k, n = w.shape
    tn = min(tn, n)
    assert m % tm == 0 and n % tn == 0 and x_col % k == 0
    in_specs = [pl.BlockSpec((tm, k), lambda i, j: (i, x_col // k))]
    args = [x]
    if gain is not None:
        in_specs.append(pl.BlockSpec((1, k), lambda i, j: (0, 0)))
        args.append(gain.reshape(1, k).astype(F32))
    in_specs.append(pl.BlockSpec((k, tn), lambda i, j: (0, j)))
    args.append(w)
    if residual is not None:
        in_specs.append(pl.BlockSpec((tm, tn), lambda i, j: (i, j)))
        args.append(residual)
    return pl.pallas_call(
        functools.partial(_nm_kernel, has_norm=gain is not None, has_res=residual is not None),
        grid=(m // tm, n // tn),
        in_specs=in_specs,
        out_specs=pl.BlockSpec((tm, tn), lambda i, j: (i, j)),
        out_shape=jax.ShapeDtypeStruct((m, n), out_dtype),
        scratch_shapes=[pltpu.VMEM((tm, k), BF16)],
        compiler_params=_params("parallel", "arbitrary"),
        name="norm_matmul",
    )(*args)


def _out_proj_kernel(a_ref, b_ref, wa_ref, wb_ref, r_ref, o_ref):
    o_ref[...] = r_ref[...] + _bdot(a_ref[...], wa_ref[...]) + _bdot(b_ref[...], wb_ref[...])


def out_proj(a, b, w, residual, tm=512, tn=512):
    m, ka = a.shape
    n = w.shape[1]
    assert b.shape[1] == ka and w.shape[0] == 2 * ka
    return pl.pallas_call(
        _out_proj_kernel,
        grid=(m // tm, n // tn),
        in_specs=[
            pl.BlockSpec((tm, ka), lambda i, j: (i, 0)),
            pl.BlockSpec((tm, ka), lambda i, j: (i, 0)),
            pl.BlockSpec((ka, tn), lambda i, j: (0, j)),
            pl.BlockSpec((ka, tn), lambda i, j: (1, j)),
            pl.BlockSpec((tm, tn), lambda i, j: (i, j)),
        ],
        out_specs=pl.BlockSpec((tm, tn), lambda i, j: (i, j)),
        out_shape=jax.ShapeDtypeStruct((m, n), F32),
        compiler_params=_params("parallel", "arbitrary"),
        name="out_proj",
    )(a, b, w, w, residual)


def _ple_kernel(x_ref, yt_ref, g_ref, wg_ref, p_ref, wp_ref, o_ref, xn_ref, xs_ref):
    j = pl.program_id(1)
    tn = o_ref.shape[1]

    @pl.when(j == 0)
    def _():
        x = x_ref[...] + yt_ref[...].T
        xn_ref[...] = _rms(x, g_ref[...]).astype(BF16)
        for jj in range(xs_ref.shape[0]):
            xs_ref[jj] = x[:, jj * tn:(jj + 1) * tn]

    gate = _sigmoid(jnp.dot(xn_ref[...], wg_ref[...], preferred_element_type=F32))
    emb = _bdot(p_ref[...], wp_ref[...])
    o_ref[...] = xs_ref[j] + gate * emb


def ple_update(x, yt, p_i, norm_g, w_gate, w_proj, tm=512, tn=512):
    m, d = x.shape
    pd = p_i.shape[1]
    return pl.pallas_call(
        _ple_kernel,
        grid=(m // tm, d // tn),
        in_specs=[
            pl.BlockSpec((tm, d), lambda i, j: (i, 0)),
            pl.BlockSpec((d, tm), lambda i, j: (0, i)),
            pl.BlockSpec((1, d), lambda i, j: (0, 0)),
            pl.BlockSpec((d, tn), lambda i, j: (0, j)),
            pl.BlockSpec((tm, pd), lambda i, j: (i, 0)),
            pl.BlockSpec((pd, tn), lambda i, j: (0, j)),
        ],
        out_specs=pl.BlockSpec((tm, tn), lambda i, j: (i, j)),
        out_shape=jax.ShapeDtypeStruct((m, d), F32),
        scratch_shapes=[pltpu.VMEM((tm, d), BF16), pltpu.VMEM((d // tn, tm, tn), F32)],
        compiler_params=_params("parallel", "arbitrary"),
        name="ple_update",
    )(x, yt, norm_g.reshape(1, d), w_gate, p_i, w_proj)


def _rmsnorm_kernel(x_ref, g_ref, o_ref):
    o_ref[...] = _rms(x_ref[...], g_ref[...])


def rmsnorm(x, gain, tm=512):
    m, d = x.shape
    return pl.pallas_call(
        _rmsnorm_kernel,
        grid=(m // tm,),
        in_specs=[pl.BlockSpec((tm, d), lambda i: (i, 0)), pl.BlockSpec((1, d), lambda i: (0, 0))],
        out_specs=pl.BlockSpec((tm, d), lambda i: (i, 0)),
        out_shape=jax.ShapeDtypeStruct((m, d), F32),
        compiler_params=_params("parallel"),
        name="rmsnorm",
    )(x, gain.reshape(1, d))


def _peer_fold_kernel(keys_ref, wq_ref, o_ref):
    o_ref[0, 0] = _hdot_nt(keys_ref[0, 0], wq_ref[...])


def peer_fold(w_q, keys):
    hk = PEER_QDIM // 2
    return pl.pallas_call(
        _peer_fold_kernel,
        grid=(2, PEER_HEADS),
        in_specs=[
            pl.BlockSpec((1, 1, PEER_NKEYS, hk), lambda c, h: (h, c, 0, 0)),
            pl.BlockSpec((D_MODEL, hk), lambda c, h: (0, h * 2 + c)),
        ],
        out_specs=pl.BlockSpec((1, 1, PEER_NKEYS, D_MODEL), lambda c, h: (c, h, 0, 0)),
        out_shape=jax.ShapeDtypeStruct((2, PEER_HEADS, PEER_NKEYS, D_MODEL), F32),
        compiler_params=_params("parallel", "parallel"),
        name="peer_fold",
    )(keys, w_q)


def _rmsnorm_t_kernel(x_ref, g_ref, o_ref):
    o_ref[...] = _rms(x_ref[...], g_ref[...]).T.astype(BF16)


def rmsnorm_t(x, gain, tm=512):
    m, d = x.shape
    return pl.pallas_call(
        _rmsnorm_t_kernel,
        grid=(m // tm,),
        in_specs=[pl.BlockSpec((tm, d), lambda i: (i, 0)), pl.BlockSpec((1, d), lambda i: (0, 0))],
        out_specs=pl.BlockSpec((d, tm), lambda i: (0, i)),
        out_shape=jax.ShapeDtypeStruct((d, m), BF16),
        compiler_params=_params("parallel"),
        name="rmsnorm_t",
    )(x, gain.reshape(1, d))


def _sort_desc(v):
    v = list(v)
    n = len(v)
    k = 2
    while k <= n:
        j = k // 2
        while j >= 1:
            for i in range(n):
                l = i ^ j
                if l > i:
                    hi, lo = jnp.maximum(v[i], v[l]), jnp.minimum(v[i], v[l])
                    v[i], v[l] = (hi, lo) if (i & k) == 0 else (lo, hi)
            j //= 2
        k *= 2
    return v


def _merge_top(a, b):
    n = len(a)
    v = [jnp.maximum(a[i], b[n - 1 - i]) for i in range(n)]
    j = n // 2
    while j >= 1:
        for i in range(n):
            l = i ^ j
            if l > i:
                v[i], v[l] = jnp.maximum(v[i], v[l]), jnp.minimum(v[i], v[l])
        j //= 2
    return v


def _top_sorted(vals, n):
    vals = list(vals)
    while len(vals) % n:
        vals.append(jnp.full_like(vals[0], -jnp.inf))
    acc = _sort_desc(vals[:n])
    for g in range(1, len(vals) // n):
        acc = _merge_top(acc, _sort_desc(vals[g * n:(g + 1) * n]))
    return acc


def _count_leading(pred, b):
    t0 = pred(b[15])
    t1 = pred(b[7])
    t2 = pred(jnp.where(t1, b[11], b[3]))
    t3 = pred(jnp.where(t1, jnp.where(t2, b[13], b[9]), jnp.where(t2, b[5], b[1])))
    hi = jnp.where(t2, jnp.where(t3, b[14], b[12]), jnp.where(t3, b[10], b[8]))
    lo = jnp.where(t2, jnp.where(t3, b[6], b[4]), jnp.where(t3, b[2], b[0]))
    t4 = pred(jnp.where(t1, hi, lo))
    cnt = (jnp.where(t1, 8.0, 0.0) + jnp.where(t2, 4.0, 0.0)) + (jnp.where(t3, 2.0, 0.0) + jnp.where(t4, 1.0, 0.0))
    return jnp.where(t0, 16.0, cnt)


_PEER_PAIRS = [(i, j) for i in range(PEER_TOPK) for j in range(PEER_TOPK) if (i + 1) * (j + 1) <= PEER_TOPK]


def _peer_select_kernel(wf_ref, ht_ref, e1_ref, n1_ref, r2_ref, e2_ref, sub_ref):
    nk, k = PEER_NKEYS, PEER_TOPK
    tt = ht_ref.shape[1]
    ht = ht_ref[...]
    for c in range(2):
        sub_ref[c] = jnp.dot(wf_ref[c], ht, preferred_element_type=F32)
    row = lax.broadcasted_iota(jnp.int32, (SUBLANES, LANES), 0)

    def head_row(v, h):
        return jnp.sum(jnp.where(row == h, v, 0.0), axis=0, keepdims=True)

    def lane_group(lg, carry):
        lanes = pl.ds(pl.multiple_of(lg * LANES, LANES), LANES)

        def top_of_head(h, packed, c):
            base = pl.multiple_of(h * nk, nk)
            slabs = [sub_ref[c, pl.ds(base + SUBLANES * j, SUBLANES), lanes] for j in range(nk // SUBLANES)]
            top = _sort_desc(slabs)
            for sh in (4, 2, 1):
                top = _merge_top(top, [pltpu.roll(t, sh, 0) for t in top])
            return tuple(jnp.where(row == h, top[i], packed[i]) for i in range(k))

        zero = tuple(jnp.zeros((SUBLANES, LANES), F32) for _ in range(k))
        a = lax.fori_loop(0, PEER_HEADS, functools.partial(top_of_head, c=0), zero)
        b = lax.fori_loop(0, PEER_HEADS, functools.partial(top_of_head, c=1), zero)
        best = _top_sorted([a[i] + b[j] for i, j in _PEER_PAIRS], k)
        thr, vmax = best[k - 1], best[0]
        z = jnp.zeros((SUBLANES, LANES), F32)
        for i in range(k):
            z = z + jnp.exp(best[i] - vmax)
        inv_z = 1.0 / z

        def emit(h, carry):
            base = pl.multiple_of(h * nk, nk)
            s1 = sub_ref[0, pl.ds(base, nk), lanes]
            s2 = sub_ref[1, pl.ds(base, nk), lanes]
            thr_h = head_row(thr, h)
            bh = [head_row(b[j], h) for j in range(k)]
            n1 = _count_leading(lambda x: s1 + x >= thr_h, bh)
            r2 = _count_leading(lambda x: x > s2, bh)
            e1_ref[h, :, lanes] = jnp.exp(s1 - head_row(a[0], h)) * head_row(inv_z, h)
            n1_ref[h, :, lanes] = n1
            r2_ref[h, :, lanes] = r2.astype(BF16)
            e2_ref[h, :, lanes] = jnp.exp(s2 - head_row(b[0], h)).astype(BF16)
            return carry

        lax.fori_loop(0, PEER_HEADS, emit, 0)
        return carry

    lax.fori_loop(0, tt // LANES, lane_group, 0)


def peer_select(wf, ht, tt=256):
    d, s = ht.shape
    nrow = PEER_HEADS * PEER_NKEYS
    shape = (PEER_HEADS, PEER_NKEYS, s)
    ospec = pl.BlockSpec((PEER_HEADS, PEER_NKEYS, tt), lambda i: (0, 0, i))
    return pl.pallas_call(
        _peer_select_kernel,
        grid=(s // tt,),
        in_specs=[pl.BlockSpec((2, nrow, d), lambda i: (0, 0, 0)), pl.BlockSpec((d, tt), lambda i: (0, i))],
        out_specs=[ospec] * 4,
        out_shape=[jax.ShapeDtypeStruct(shape, F32)] * 2 + [jax.ShapeDtypeStruct(shape, BF16)] * 2,
        scratch_shapes=[pltpu.VMEM((2, nrow, tt), F32)],
        compiler_params=_params("parallel"),
        name="peer_select",
    )(wf, ht)


def _gelu_tanh(x):
    return 0.5 * x * (1.0 + jnp.tanh(math.sqrt(2.0 / math.pi) * (x + 0.044715 * (x * x * x))))


def _peer_dense_kernel(u_ref, ht_ref, vt_ref, en_ref, r2_ref, e2_ref, o_ref, act0_ref, act1_ref, ga_ref):
    nk = PEER_NKEYS
    j = pl.program_id(1)
    eb = u_ref.shape[0]
    nblk = eb // nk

    @pl.when(j == 0)
    def _():
        o_ref[...] = jnp.zeros_like(o_ref)
        act1_ref[...] = jnp.zeros_like(act1_ref)

    done = jnp.maximum(j - 1, 0)

    def step(prev_ref, next_ref):
        for ii in range(nblk):
            en = en_ref[done * nblk + ii]
            gate = None
            for h in range(PEER_HEADS):
                e1 = en[h:h + 1]
                n1 = en[PEER_HEADS + h:PEER_HEADS + h + 1]
                g = jnp.where(r2_ref[h] < n1, e2_ref[h] * e1, jnp.zeros((), BF16))
                gate = g if gate is None else gate + g
            ga_ref[pl.ds(ii * nk, nk), :] = gate * prev_ref[pl.ds(ii * nk, nk), :]
        next_ref[...] = _gelu_tanh(jnp.dot(u_ref[...], ht_ref[...], preferred_element_type=F32)).astype(BF16)
        o_ref[...] += jnp.dot(vt_ref[...], ga_ref[...], preferred_element_type=F32)

    pl.when(j % 2 == 0)(functools.partial(step, act1_ref, act0_ref))
    pl.when(j % 2 == 1)(functools.partial(step, act0_ref, act1_ref))


PEER_EB = 1024


def peer_dense(u_all, ht, vt_all, layer, e1, n1, r2, e2, tt=1024):
    d, s = ht.shape
    tt = min(tt, s)
    ne, eb = u_all.shape[1], PEER_EB
    last = ne // eb - 1
    en = jnp.concatenate([e1, n1], axis=0).transpose(1, 0, 2).astype(BF16)
    once = pl.Buffered(1)
    gspec = pl.BlockSpec((PEER_HEADS, PEER_NKEYS, tt), lambda i, j: (0, 0, i), pipeline_mode=once)
    return pl.pallas_call(
        _peer_dense_kernel,
        grid=(s // tt, ne // eb + 1),
        in_specs=[
            pl.BlockSpec((None, eb, d), lambda i, j: (layer, jnp.minimum(j, last), 0)),
            pl.BlockSpec((d, tt), lambda i, j: (0, i), pipeline_mode=once),
            pl.BlockSpec((None, None, d, eb), lambda i, j: (layer, jnp.maximum(j - 1, 0), 0, 0)),
            pl.BlockSpec((PEER_NKEYS, 2 * PEER_HEADS, tt), lambda i, j: (0, 0, i), pipeline_mode=once),
            gspec, gspec,
        ],
        out_specs=pl.BlockSpec((d, tt), lambda i, j: (0, i)),
        out_shape=jax.ShapeDtypeStruct((d, s), F32),
        scratch_shapes=[pltpu.VMEM((eb, tt), BF16), pltpu.VMEM((eb, tt), BF16), pltpu.VMEM((eb, tt), BF16)],
        compiler_params=_params("parallel", "arbitrary"),
        name="peer_dense",
    )(u_all, ht, vt_all, en, r2, e2)


def prep_peer_tables(peer_u, peer_v):
    nl, ne, d = peer_v.shape
    vt_all = peer_v.astype(BF16).reshape(nl, ne // PEER_EB, PEER_EB, d).transpose(0, 1, 3, 2)
    return peer_u.astype(BF16), vt_all


def peer(x, ln_g, w_q, keys, u_all, vt_all, layer, tm=512, tt=256):
    wf = peer_fold(w_q, keys).reshape(2, PEER_HEADS * PEER_NKEYS, D_MODEL).astype(BF16)
    ht = rmsnorm_t(x, ln_g, tm=tm)
    return peer_dense(u_all, ht, vt_all, layer, *peer_select(wf, ht, tt=tt))


def _mla_kernel(qi_ref, ki_ref, qn_ref, qr_ref, qrot_ref, cq_ref, sq_ref, kn_ref, kr_ref, krot_ref, ck_ref, sk_ref,
                vt_ref, o_ref, q1_ref, q2_ref, m_ref, l_ref, acc_ref, *, scale):
    t = pl.program_id(1)
    qi, ki = qi_ref[t], ki_ref[t]
    tq, tk = qn_ref.shape[1], kn_ref.shape[0]
    ratio = tq // tk

    @pl.when(ki == 0)
    def _():
        q1_ref[...] = (qn_ref[...] * scale).astype(BF16)
        q2_ref[...] = ((qr_ref[...] * cq_ref[...] + qrot_ref[...] * sq_ref[...]) * scale).astype(BF16)
        m_ref[...] = jnp.full_like(m_ref, -jnp.inf)
        l_ref[...] = jnp.zeros_like(l_ref)
        acc_ref[...] = jnp.zeros_like(acc_ref)

    def step(masked):
        kr = (kr_ref[...] * ck_ref[...] + krot_ref[...] * sk_ref[...]).astype(BF16)
        s = (jnp.dot(kn_ref[...], q1_ref[...], preferred_element_type=F32)
             + jnp.dot(kr, q2_ref[...], preferred_element_type=F32))
        if masked:
            kpos = ki * tk + lax.broadcasted_iota(jnp.int32, (tk, tq), 0)
            qpos = qi * tq + lax.broadcasted_iota(jnp.int32, (tk, tq), 1)
            s = jnp.where(kpos <= qpos, s, -jnp.inf)
        m_old = m_ref[...]
        m_new = jnp.maximum(m_old, jnp.max(s, axis=0, keepdims=True))
        alpha = jnp.exp(m_old - m_new)
        p = jnp.exp(s - m_new)
        l_ref[...] = alpha * l_ref[...] + jnp.sum(p, axis=0, keepdims=True)
        acc_ref[...] = alpha * acc_ref[...] + jnp.dot(vt_ref[...], p.astype(BF16), preferred_element_type=F32)
        m_ref[...] = m_new

    pl.when(ki < qi * ratio)(functools.partial(step, False))
    pl.when(ki >= qi * ratio)(functools.partial(step, True))

    @pl.when(ki == (qi + 1) * ratio - 1)
    def _():
        o_ref[...] = acc_ref[...] / l_ref[...]


def mla_attention(qt, kv, vt, kr, krot, cos, sin, cos_t, sin_t, tq=1024, tk=512):
    s = kv.shape[0]
    tq, tk = min(tq, s), min(tk, s)
    ratio = tq // tk
    hh, dn, dr = MLA_HEADS, MLA_NOPE, MLA_ROPE
    pairs = [(qi, ki) for qi in range(s // tq) for ki in range((qi + 1) * ratio)]
    qi_tab = jnp.array([pr[0] for pr in pairs], jnp.int32)
    ki_tab = jnp.array([pr[1] for pr in pairs], jnp.int32)
    r0 = hh * dn // dr
    qmap = lambda h, t, qt_, kt_: (0, qt_[t])
    kmap = lambda h, t, qt_, kt_: (kt_[t], 0)
    grid_spec = pltpu.PrefetchScalarGridSpec(
        num_scalar_prefetch=2,
        grid=(hh, len(pairs)),
        in_specs=[
            pl.BlockSpec((dn, tq), lambda h, t, qt_, kt_: (h, qt_[t])),
            pl.BlockSpec((dr, tq), lambda h, t, qt_, kt_: (r0 + h, qt_[t])),
            pl.BlockSpec((dr, tq), lambda h, t, qt_, kt_: (r0 + hh + h, qt_[t])),
            pl.BlockSpec((dr, tq), qmap),
            pl.BlockSpec((dr, tq), qmap),
            pl.BlockSpec((tk, dn), lambda h, t, qt_, kt_: (kt_[t], h)),
            pl.BlockSpec((tk, dr), kmap),
            pl.BlockSpec((tk, dr), kmap),
            pl.BlockSpec((tk, dr), kmap),
            pl.BlockSpec((tk, dr), kmap),
            pl.BlockSpec((MLA_V, tk), lambda h, t, qt_, kt_: (h, kt_[t])),
        ],
        out_specs=pl.BlockSpec((MLA_V, tq), lambda h, t, qt_, kt_: (h, qt_[t])),
        scratch_shapes=[pltpu.VMEM((dn, tq), BF16), pltpu.VMEM((dr, tq), BF16), pltpu.VMEM((1, tq), F32),
                        pltpu.VMEM((1, tq), F32), pltpu.VMEM((MLA_V, tq), F32)],
    )
    return pl.pallas_call(
        functools.partial(_mla_kernel, scale=(MLA_NOPE + MLA_ROPE) ** -0.5),
        grid_spec=grid_spec,
        out_shape=jax.ShapeDtypeStruct((hh * MLA_V, s), F32),
        compiler_params=_params("parallel", "arbitrary"),
        name="mla_attention",
    )(qi_tab, ki_tab, qt, qt, qt, cos_t, sin_t, kv, kr, krot, cos, sin, vt)


def _causal_conv(cur_ref, prev_ref, w_ref, first, bias=None):
    prev = jnp.where(first, 0.0, prev_ref[...])
    xe = jnp.concatenate([prev, cur_ref[...]], axis=0)
    w = w_ref[...]
    acc = w[CONV_WIDTH - 1:CONV_WIDTH] * xe[SUBLANES:]
    for j in range(CONV_WIDTH - 1):
        acc = acc + w[j:j + 1] * pltpu.roll(xe, CONV_WIDTH - 1 - j, 0)[SUBLANES:]
    return acc if bias is None else acc + bias


def _unit_lower_inverses(lows, c):
    n = lows[0].shape[0]
    r = lax.broadcasted_iota(jnp.int32, (n, n), 0)
    q = lax.broadcasted_iota(jnp.int32, (n, n), 1)
    eye = (r == q).astype(F32)
    prev = [jnp.where((r // 16) == (q // 16), low, 0.0) for low in lows]
    ps = [-d for d in prev]
    xs = [eye + p for p in ps]
    for _ in range(3):
        ps = [_bdot(p, p) for p in ps]
        xs = [x + _bdot(x, p) for x, p in zip(xs, ps)]
    size = 32
    while size <= c:
        cur = [jnp.where((r // size) == (q // size), low, 0.0) for low in lows] if size < c else lows
        ts = [_bdot(x, cu - pr) for x, cu, pr in zip(xs, cur, prev)]
        xs = [x - _bdot(t, x) for x, t in zip(xs, ts)]
        prev = cur
        size *= 2
    return xs


def _gdn_kernel(q_ref, qp_ref, k_ref, kp_ref, v_ref, vp_ref, z_ref, wq_ref, wk_ref, wv_ref,
                ar_ref, br_ref, alog_ref, dtb_ref, ng_ref, o_ref, st_ref):
    c = GDN_CHUNK
    tt = q_ref.shape[0]
    first = pl.program_id(1) == 0

    @pl.when(first)
    def _():
        st_ref[...] = jnp.zeros_like(st_ref)

    def l2n(x):
        return x * lax.rsqrt(jnp.sum(x * x, axis=-1, keepdims=True) + 1e-6)

    q = l2n(_silu(_causal_conv(q_ref, qp_ref, wq_ref, first))) * (GDN_DK ** -0.5)
    k = l2n(_silu(_causal_conv(k_ref, kp_ref, wk_ref, first)))
    v = _silu(_causal_conv(v_ref, vp_ref, wv_ref, first))
    neg_a = -jnp.exp(alog_ref[0, :, 0:1])
    dtb = dtb_ref[0, :, 0:1]
    nb = 2 * c
    r = lax.broadcasted_iota(jnp.int32, (nb, nb), 0)
    cc = lax.broadcasted_iota(jnp.int32, (nb, nb), 1)
    same = (r // c) == (cc // c)
    incl, strict = same & (r >= cc), same & (r > cc)
    incl_t = same & (r <= cc)
    top = lax.broadcasted_iota(jnp.int32, (nb, 1), 0) < c
    blocks = range(tt // nb)
    rows = [slice(n * nb, (n + 1) * nb) for n in blocks]
    as_col = lambda row: jnp.sum(jnp.where(r == cc, row, 0.0), axis=1, keepdims=True)
    g_row = [neg_a * _softplus(ar_ref[0, n] + dtb) for n in blocks]
    g_col = [as_col(g) for g in g_row]
    beta = [as_col(_sigmoid(br_ref[0, n])) for n in blocks]
    gc = [jnp.sum(jnp.where(incl, g, 0.0), axis=1, keepdims=True) for g in g_row]
    gr = [jnp.sum(jnp.where(incl_t, g, 0.0), axis=0, keepdims=True) for g in g_col]
    decay = [jnp.where(incl, jnp.exp(jnp.where(incl, a - b, 0.0)), 0.0) for a, b in zip(gc, gr)]
    kk = [_bdot_nt(k[sl], k[sl]) for sl in rows]
    qk = [_bdot_nt(q[sl], k[sl]) for sl in rows]
    inv = _unit_lower_inverses([jnp.where(strict, b * m * d, 0.0) for b, m, d in zip(beta, kk, decay)], c)
    eg = [jnp.exp(g) for g in gc]
    sol = [_bdot(x, jnp.concatenate([b * v[sl], (b * e) * k[sl]], axis=1)) for x, b, e, sl in zip(inv, beta, eg, rows)]
    a_qk = [m * d for m, d in zip(qk, decay)]
    k_end = [k[sl] * jnp.exp(jnp.where(top, g[c - 1:c], g[nb - 1:nb]) - g) for g, sl in zip(gc, rows)]
    k_end_t = [[ke[:c].T, ke[c:].T] for ke in k_end]
    q_dec = [q[sl] * e for e, sl in zip(eg, rows)]
    state = st_ref[...]
    us, o_state = [], []
    for n in blocks:
        for j in range(2):
            cs = slice(j * c, (j + 1) * c)
            u = sol[n][cs, :GDN_DV] - _bdot(sol[n][cs, GDN_DV:], state)
            o_state.append(_bdot(q_dec[n][cs], state))
            state = jnp.exp(gc[n][(j + 1) * c - 1:(j + 1) * c]) * state + _bdot(k_end_t[n][j], u)
            us.append(u)
    st_ref[...] = state
    o_loc = [_bdot(a, jnp.concatenate(us[2 * n:2 * n + 2], axis=0)) for n, a in zip(blocks, a_qk)]
    o = jnp.concatenate(o_state, axis=0) + jnp.concatenate(o_loc, axis=0)
    o_ref[...] = _rms(o, ng_ref[...]) * _silu(z_ref[...])


def gated_delta_net(proj, col0, a_raw, b_raw, conv_w, a_log, dt_bias, norm_g, tt=256):
    s = proj.shape[0]
    tt = min(tt, s)
    hh, c = GDN_HEADS, GDN_CHUNK
    b0 = col0 // LANES
    nblk = GDN_HEADS * GDN_DK // LANES

    def cur(g):
        return pl.BlockSpec((tt, LANES), lambda h, i: (i, b0 + g * nblk + h))

    def prev(g):
        return pl.BlockSpec((SUBLANES, LANES), lambda h, i: (jnp.maximum(i * (tt // SUBLANES) - 1, 0), b0 + g * nblk + h))

    def wspec(g):
        return pl.BlockSpec((CONV_WIDTH, LANES), lambda h, i: (0, g * nblk + h))

    a_t, b_t = a_raw.T, b_raw.T
    c = 2 * c
    rowspec = pl.BlockSpec((1, tt // c, 1, c), lambda h, i: (h, i, 0, 0))
    hspec = pl.BlockSpec((1, 1, LANES), lambda h, i: (h, 0, 0))
    bcast = lambda p: jnp.broadcast_to(p.astype(F32)[:, None, None], (hh, 1, LANES))
    return pl.pallas_call(
        _gdn_kernel,
        grid=(hh, s // tt),
        in_specs=[cur(0), prev(0), cur(1), prev(1), cur(2), prev(2), cur(3), wspec(0), wspec(1), wspec(2),
                  rowspec, rowspec, hspec, hspec, pl.BlockSpec((1, GDN_DV), lambda h, i: (0, 0))],
        out_specs=pl.BlockSpec((tt, GDN_DV), lambda h, i: (i, h)),
        out_shape=jax.ShapeDtypeStruct((s, hh * GDN_DV), F32),
        scratch_shapes=[pltpu.VMEM((GDN_DK, GDN_DV), F32)],
        compiler_params=_params("parallel", "arbitrary"),
        name="gated_delta_net",
    )(proj, proj, proj, proj, proj, proj, proj, conv_w, conv_w, conv_w,
      a_t.reshape(hh, s // c, 1, c), b_t.reshape(hh, s // c, 1, c),
      bcast(a_log), bcast(dt_bias), norm_g.reshape(1, GDN_DV).astype(F32))


def _rot_half_cols(w, half):
    return jnp.concatenate([-w[..., half:], w[..., :half]], axis=-1)


def _pad_cols(w, n):
    return jnp.pad(w, ((0, 0), (0, n - w.shape[1])))


AB_QKVZ = MLA_Q_RANK + MLA_KV_RANK + 2 * MLA_ROPE
AB_GATES = AB_QKVZ + 4 * GDN_HEADS * GDN_DK
AB_PAD = AB_GATES + LANES


def prep_ab(w_in, w_uq, w_ukv, w_out):
    rq, rkv, rr = MLA_Q_RANK, MLA_KV_RANK, MLA_ROPE
    w_kr = w_in[:, rq + rkv:rq + rkv + rr]
    w_main = jnp.concatenate([w_in[:, :rq + rkv], w_kr, _rot_half_cols(w_kr, rr // 2),
                              w_in[:, rq + rkv + rr:]], axis=1)
    w_main = _pad_cols(w_main, AB_PAD).astype(BF16)
    uq = w_uq.reshape(rq, MLA_HEADS, MLA_NOPE + MLA_ROPE)
    uq_r = uq[..., MLA_NOPE:]
    uq2 = jnp.concatenate([uq[..., :MLA_NOPE].reshape(rq, -1), uq_r.reshape(rq, -1),
                           _rot_half_cols(uq_r, rr // 2).reshape(rq, -1)], axis=1).astype(BF16)
    ukv = w_ukv.reshape(rkv, MLA_HEADS, MLA_NOPE + MLA_V)
    ukv2 = jnp.concatenate([ukv[..., :MLA_NOPE].reshape(rkv, -1), ukv[..., MLA_NOPE:].reshape(rkv, -1)], axis=1).astype(BF16)
    return w_main, uq2, ukv2, w_out.astype(BF16)


def mixer_ab(x, ln, w_main, uq2, ukv2, w_out, q_norm, kv_norm, conv_w, a_log, dt_bias, gdn_norm, cos2, sin2,
             tm=512, t_attn=512, t_gdn=256):
    s = x.shape[0]
    rq, rkv, rr = MLA_Q_RANK, MLA_KV_RANK, MLA_ROPE
    proj = norm_matmul(x, w_main, gain=ln, tm=min(2 * tm, x.shape[0]))
    qfull = norm_matmul(proj, uq2, gain=q_norm, tm=tm)
    kv = norm_matmul(proj, ukv2, gain=kv_norm, tm=tm, out_dtype=BF16, x_col=rq)
    nn = MLA_HEADS * MLA_NOPE
    o_a = mla_attention(qfull.T, kv, kv[:, nn:].T, proj[:, rq + rkv:rq + rkv + rr], proj[:, rq + rkv + rr:AB_QKVZ],
                        cos2, sin2, cos2.T, sin2.T, tq=2 * t_attn, tk=t_attn).T
    o_b = gated_delta_net(proj, AB_QKVZ, proj[:, AB_GATES:AB_GATES + GDN_HEADS],
                          proj[:, AB_GATES + GDN_HEADS:AB_GATES + 2 * GDN_HEADS], conv_w, a_log, dt_bias, gdn_norm, tt=t_gdn)
    return out_proj(o_a, o_b, w_out, x, tm=min(2 * tm, x.shape[0]))


CD_R, CD_K, CD_V, CD_Z, CD_DT, CD_X = (i * 1024 for i in range(6))
CD_B = 6144
CD_C = CD_B + SSD_GROUPS * SSD_STATE
CD_WA = CD_C + SSD_GROUPS * SSD_STATE
CD_G = CD_WA + LANES
CD_PAD = 7168
GROUP_W = SSD_INNER // SSD_GROUPS


def _lower_ones(c):
    r = lax.broadcasted_iota(jnp.int32, (c, c), 0)
    q = lax.broadcasted_iota(jnp.int32, (c, c), 1)
    return r >= q, r > q


def _ssd_kernel(x_ref, xp_ref, b_ref, bp_ref, c_ref, cp_ref, z_ref, dt_ref, wx_ref, wb_ref, wc_ref,
                bx_ref, bb_ref, bc_ref, dtb_ref, alog_ref, dskip_ref, ng_ref, o_ref, st_ref):
    c = SSD_CHUNK
    tt = x_ref.shape[0]
    first = pl.program_id(1) == 0

    @pl.when(first)
    def _():
        st_ref[...] = jnp.zeros_like(st_ref)

    xs_all = _silu(_causal_conv(x_ref, xp_ref, wx_ref, first, bx_ref[...]))
    bm_all = _silu(_causal_conv(b_ref, bp_ref, wb_ref, first, bb_ref[...]))
    cm_all = _silu(_causal_conv(c_ref, cp_ref, wc_ref, first, bc_ref[...]))
    dt_all = _softplus(dt_ref[...] + dtb_ref[...])
    a_all = -jnp.exp(alog_ref[...]) * dt_all
    incl, _ = _lower_ones(c)
    tri = incl.astype(F32)
    left = lax.broadcasted_iota(jnp.int32, (c, LANES), 1) < SSD_HEADDIM
    npair = GROUP_W // LANES
    outs = []
    for n in range(tt // c):
        sl = slice(n * c, (n + 1) * c)
        xs, bm, cm, dt = xs_all[sl], bm_all[sl], cm_all[sl], dt_all[sl]
        acs = _bdot(tri, a_all[sl])
        xdt = xs * dt
        cb = _bdot_nt(cm, bm)
        bm_t = bm.T
        ys = []
        for p in range(npair):
            ls = slice(p * LANES, (p + 1) * LANES)
            acs_p = acs[:, ls]
            acs_t = acs_p.T
            xp = xdt[:, ls]
            yd = []
            for hd in range(2):
                col = acs_p[:, hd * SSD_HEADDIM:hd * SSD_HEADDIM + 1]
                row = acs_t[hd * SSD_HEADDIM:hd * SSD_HEADDIM + 1, :]
                lmat = jnp.where(incl, jnp.exp(jnp.where(incl, col - row, 0.0)), 0.0)
                yd.append(_bdot(cb * lmat, xp))
            last = acs_p[c - 1:c]
            prev_t = st_ref[p]
            y_off = _bdot(cm, prev_t) * jnp.exp(acs_p)
            st_ref[p] = jnp.exp(last) * prev_t + _bdot(bm_t, xp * jnp.exp(last - acs_p))
            ys.append(jnp.where(left, yd[0], yd[1]) + y_off)
        outs.append(jnp.concatenate(ys, axis=1) + xs * dskip_ref[...])
    y = jnp.concatenate(outs, axis=0) * _silu(z_ref[...])
    o_ref[...] = _rms(y, ng_ref[...])


def mamba2_ssd(proj, conv_w, conv_b, dt_bias, a_log, d_skip, norm_g, tt=256):
    s = proj.shape[0]
    tt = min(tt, s)
    gw, ns = GROUP_W, SSD_STATE
    per = lambda v: jnp.repeat(v.astype(F32), SSD_HEADDIM).reshape(1, SSD_INNER)

    def cur(col, w):
        return pl.BlockSpec((tt, w), lambda g, i: (i, col // w + g))

    def prev(col, w):
        return pl.BlockSpec((SUBLANES, w), lambda g, i: (jnp.maximum(i * (tt // SUBLANES) - 1, 0), col // w + g))

    def par(rows, col, w):
        return pl.BlockSpec((rows, w), lambda g, i: (0, col // w + g))

    cb = conv_b.reshape(1, -1).astype(F32)
    return pl.pallas_call(
        _ssd_kernel,
        grid=(SSD_GROUPS, s // tt),
        in_specs=[cur(CD_X, gw), prev(CD_X, gw), cur(CD_B, ns), prev(CD_B, ns), cur(CD_C, ns), prev(CD_C, ns),
                  cur(CD_Z, gw), cur(CD_DT, gw),
                  par(CONV_WIDTH, 0, gw), par(CONV_WIDTH, SSD_INNER, ns), par(CONV_WIDTH, SSD_INNER + SSD_GROUPS * ns, ns),
                  par(1, 0, gw), par(1, SSD_INNER, ns), par(1, SSD_INNER + SSD_GROUPS * ns, ns),
                  par(1, 0, gw), par(1, 0, gw), par(1, 0, gw), par(1, 0, gw)],
        out_specs=pl.BlockSpec((tt, gw), lambda g, i: (i, g)),
        out_shape=jax.ShapeDtypeStruct((s, SSD_INNER), F32),
        scratch_shapes=[pltpu.VMEM((gw // LANES, ns, LANES), F32)],
        compiler_params=_params("parallel", "arbitrary"),
        name="mamba2_ssd",
    )(proj, proj, proj, proj, proj, proj, proj, proj, conv_w, conv_w, conv_w, cb, cb, cb,
      per(dt_bias), per(a_log), per(d_skip), norm_g.reshape(1, SSD_INNER).astype(F32))


def _pair_ones():
    r = lax.broadcasted_iota(jnp.int32, (LANES, LANES), 0)
    q = lax.broadcasted_iota(jnp.int32, (LANES, LANES), 1)
    return (r // RWKV_HEAD) == (q // RWKV_HEAD)


def _head_sums(x, ones):
    return jnp.concatenate([_hdot(x[:, i * LANES:(i + 1) * LANES], ones) for i in range(x.shape[1] // LANES)], axis=1)


def _rwkv_prep_kernel(r_ref, rp_ref, k_ref, kp_ref, v_ref, vp_ref, wa_ref, wap_ref, g0_ref, g0p_ref, g1_ref, g1p_ref,
                      mur_ref, muk_ref, muv_ref, muwa_ref, mug0_ref, mug1_ref, w0_ref, w2_ref, a0_ref, a2_ref, g2_ref,
                      kk_ref, ka_ref, rk_ref,
                      ro_ref, lw_ref, ko_ref, vo_ref, po_ref, qo_ref, go_ref, bo_ref):
    first = pl.program_id(0) == 0

    def mix(cur_ref, prev_ref, mu_ref):
        cur = cur_ref[...]
        prev = jnp.where(first, 0.0, prev_ref[...])
        shifted = pltpu.roll(jnp.concatenate([prev, cur], axis=0), 1, 0)[SUBLANES:]
        return cur + (shifted - cur) * mu_ref[...]

    r = mix(r_ref, rp_ref, mur_ref)
    k = mix(k_ref, kp_ref, muk_ref)
    v = mix(v_ref, vp_ref, muv_ref)
    wa = mix(wa_ref, wap_ref, muwa_ref)
    g0 = mix(g0_ref, g0p_ref, mug0_ref)
    g1 = mix(g1_ref, g1p_ref, mug1_ref)
    log_w = -math.exp(-0.5) * _sigmoid(w0_ref[...] + _bdot(jnp.tanh(wa), w2_ref[...]))
    a = _sigmoid(a0_ref[...] + _bdot(wa, a2_ref[...]))
    gate = _bdot(_sigmoid(g0), g2_ref[0:LANES, :]) + _bdot(_sigmoid(g1), g2_ref[LANES:2 * LANES, :])
    ones = _pair_ones().astype(F32)
    kx = k * kk_ref[...]
    kk = kx * lax.rsqrt(_head_sums(kx * kx, ones) + 1e-6)
    k_mod = k * (1.0 + (a - 1.0) * ka_ref[...])
    ro_ref[...] = r
    lw_ref[...] = log_w
    ko_ref[...] = k_mod
    vo_ref[...] = v
    po_ref[...] = -kk * a
    qo_ref[...] = kk
    go_ref[...] = gate
    bo_ref[...] = _head_sums(r * k_mod * rk_ref[...], ones) * v


def rwkv_prep(proj, mu, w0, w2, a0, a2, g2, k_k, k_a, r_k, tt=256):
    s = proj.shape[0]
    tt = min(tt, s)
    ri = RWKV_INNER
    row = lambda v: v.reshape(1, -1).astype(F32)
    mu_r, mu_k, mu_v = (row(mu[i * ri:(i + 1) * ri]) for i in range(3))
    mu_wa = row(mu[3 * ri:3 * ri + LANES])
    mu_g = row(jnp.pad(mu[3 * ri + LANES:], (0, 2 * LANES - RWKV_G_LORA)))
    zeros = jnp.zeros((RWKV_W_LORA, ri), F32)
    w2p = jnp.concatenate([w2, zeros], axis=0).astype(BF16)
    a2p = jnp.concatenate([zeros, a2], axis=0).astype(BF16)
    g2p = jnp.pad(g2, ((0, 2 * LANES - RWKV_G_LORA), (0, 0))).astype(BF16)

    def cur(col, w):
        return pl.BlockSpec((tt, w), lambda i: (i, col // w))

    def prev(col, w):
        return pl.BlockSpec((SUBLANES, w), lambda i: (jnp.maximum(i * (tt // SUBLANES) - 1, 0), col // w))

    full = lambda a: pl.BlockSpec(a.shape, lambda i: (0, 0))
    params = [mu_r, mu_k, mu_v, mu_wa, mu_g[:, :LANES], mu_g[:, LANES:], row(w0), w2p, row(a0), a2p, g2p,
              row(k_k), row(k_a), row(r_k)]
    out = jax.ShapeDtypeStruct((s, ri), F32)
    return pl.pallas_call(
        _rwkv_prep_kernel,
        grid=(s // tt,),
        in_specs=[cur(CD_R, ri), prev(CD_R, ri), cur(CD_K, ri), prev(CD_K, ri), cur(CD_V, ri), prev(CD_V, ri),
                  cur(CD_WA, LANES), prev(CD_WA, LANES), cur(CD_G, LANES), prev(CD_G, LANES),
                  cur(CD_G + LANES, LANES), prev(CD_G + LANES, LANES)] + [full(a) for a in params],
        out_specs=[pl.BlockSpec((tt, ri), lambda i: (i, 0))] * 8,
        out_shape=[out] * 8,
        compiler_params=_params("arbitrary"),
        name="rwkv_prep",
    )(*([proj] * 12), *params)


RWKV_MY_CHUNK = 64


def _rwkv_scan_kernel(r_ref, lw_ref, k_ref, v_ref, p_ref, q_ref, g_ref, b_ref, lnw_ref, lnb_ref, o_ref, st_ref):
    c = RWKV_MY_CHUNK
    tt = r_ref.shape[0]

    @pl.when(pl.program_id(1) == 0)
    def _():
        st_ref[...] = jnp.zeros_like(st_ref)

    tri = _lower_ones(c)[0].astype(F32)
    left = lax.broadcasted_iota(jnp.int32, (c, LANES), 1) < RWKV_HEAD
    pair = _pair_ones()
    ones = pair.astype(F32)
    r_id = lax.broadcasted_iota(jnp.int32, (LANES, LANES), 0)
    c_id = lax.broadcasted_iota(jnp.int32, (LANES, LANES), 1)
    eye = r_id == c_id
    top = r_id < c
    strict = pair & ((r_id % c) > (c_id % c))
    incl = pair & ((r_id % c) >= (c_id % c))
    stack2 = lambda a: jnp.concatenate([a, a], axis=0)
    by_head = lambda a: jnp.concatenate([jnp.where(left, a, 0.0), jnp.where(left, 0.0, a)], axis=0)
    unstack = lambda a: jnp.where(left, a[:c], a[c:])
    chunks = range(tt // c)
    rows = [slice(n * c, (n + 1) * c) for n in chunks]
    w = [lw_ref[sl, :] for sl in rows]
    v = [v_ref[sl, :] for sl in rows]
    lw = [_bdot(tri, x) for x in w]
    lam_in = [jnp.exp(x) for x in lw]
    inv_lam = [jnp.exp(-x) for x in lw]
    q_bar = [q_ref[sl, :] * jnp.exp(a - b) for sl, a, b in zip(rows, lw, w)]
    r_bar = [r_ref[sl, :] * x for sl, x in zip(rows, lam_in)]
    pk = [jnp.concatenate([p_ref[sl, :] * x, k_ref[sl, :] * x], axis=0) for sl, x in zip(rows, inv_lam)]
    mq = [_bdot_nt(by_head(a), b) for a, b in zip(q_bar, pk)]
    mr = [_bdot_nt(by_head(a), b) for a, b in zip(r_bar, pk)]
    mq_sw = [pltpu.roll(x, c, 1) for x in mq]
    mr_sw = [pltpu.roll(x, c, 1) for x in mr]
    m_qp = [jnp.where(strict, jnp.where(top, a, b), 0.0) for a, b in zip(mq, mq_sw)]
    m_qk = [jnp.where(strict, jnp.where(top, b, a), 0.0) for a, b in zip(mq, mq_sw)]
    m_rp = [jnp.where(incl, jnp.where(top, a, b), 0.0) for a, b in zip(mr, mr_sw)]
    m_rk = [jnp.where(incl, jnp.where(top, b, a), 0.0) for a, b in zip(mr, mr_sw)]
    vv = [stack2(x) for x in v]
    inv = _unit_lower_inverses([-x for x in m_qp], c)
    qkv = [_bdot(a, b) for a, b in zip(m_qk, vv)]
    sol = [_bdot(x, jnp.concatenate([stack2(a), b], axis=1)) for x, a, b in zip(inv, q_bar, qkv)]
    ws = [unstack(x[:, :LANES]) for x in sol]
    wv = [unstack(x[:, LANES:]) for x in sol]
    y_loc = [unstack(_bdot(a, b)) for a, b in zip(m_rk, vv)]
    lam_end = [x[c - 1:c] for x in lam_in]
    lam_col = [jnp.sum(jnp.where(eye, x, 0.0), axis=1, keepdims=True) for x in lam_end]
    pk_end_t = [(a * b).T for a, b in zip(pk, lam_end)]
    state = st_ref[...]
    us, y_state = [], []
    for n in chunks:
        u = _bdot(ws[n], state) + wv[n]
        y_state.append(_bdot(r_bar[n], state))
        upd = _bdot(pk_end_t[n], jnp.concatenate([u, v[n]], axis=0))
        state = lam_col[n] * state + jnp.where(pair, upd, 0.0)
        us.append(u)
    outs = [a + unstack(_bdot(b, stack2(u))) + d for a, b, u, d in zip(y_state, m_rp, us, y_loc)]
    st_ref[...] = state
    y = jnp.concatenate(outs, axis=0)
    mean = _hdot(y, ones) * (1.0 / RWKV_HEAD)
    yc = y - mean
    var = _hdot(yc * yc, ones) * (1.0 / RWKV_HEAD)
    y = yc * lax.rsqrt(var + RWKV_GN_EPS) * lnw_ref[...] + lnb_ref[...]
    o_ref[...] = (y + b_ref[...]) * g_ref[...]


def rwkv_scan(r, lw, k, v, p, q, gate, bonus, ln_w, ln_b, tt=256):
    s = r.shape[0]
    tt = min(tt, s)
    spec = pl.BlockSpec((tt, LANES), lambda h, i: (i, h))
    pspec = pl.BlockSpec((1, LANES), lambda h, i: (0, h))
    return pl.pallas_call(
        _rwkv_scan_kernel,
        grid=(RWKV_INNER // LANES, s // tt),
        in_specs=[spec] * 8 + [pspec] * 2,
        out_specs=spec,
        out_shape=jax.ShapeDtypeStruct((s, RWKV_INNER), F32),
        scratch_shapes=[pltpu.VMEM((LANES, LANES), F32)],
        compiler_params=_params("parallel", "arbitrary"),
        name="rwkv_scan",
    )(r, lw, k, v, p, q, gate, bonus, ln_w.reshape(1, -1).astype(F32), ln_b.reshape(1, -1).astype(F32))


def prep_cd(w_in, w_out):
    si, ri = SSD_INNER, RWKV_INNER
    z, xbc, dt, rw = w_in[:, :si], w_in[:, si:2 * si + 512], w_in[:, 2 * si + 512:2 * si + 528], w_in[:, 2 * si + 528:]
    dt_exp = jnp.repeat(dt, SSD_HEADDIM, axis=1)
    cols = [rw[:, :3 * ri], z, dt_exp, xbc, rw[:, 3 * ri:]]
    return _pad_cols(jnp.concatenate(cols, axis=1), CD_PAD).astype(BF16), w_out.astype(BF16)


def mixer_cd(x, ln, w_main, w_out, ssd_conv_w, ssd_conv_b, ssd_dt_bias, ssd_a_log, ssd_d, ssd_norm,
             mu, w0, w2, a0, a2, g2, k_k, k_a, r_k, ln_w, ln_b, tm=512, tt=256):
    proj = norm_matmul(x, w_main, gain=ln, tm=min(2 * tm, x.shape[0]))
    o_c = mamba2_ssd(proj, ssd_conv_w, ssd_conv_b, ssd_dt_bias, ssd_a_log, ssd_d, ssd_norm, tt=tt)
    o_d = rwkv_scan(*rwkv_prep(proj, mu, w0, w2, a0, a2, g2, k_k, k_a, r_k.reshape(-1), tt=tt), ln_w, ln_b, tt=2 * tt)
    return out_proj(o_c, o_d, w_out, x, tm=min(2 * tm, x.shape[0]))


def _rope_tables(s):
    inv = 1.0 / (ROPE_THETA ** (jnp.arange(0, MLA_ROPE, 2, dtype=F32) / MLA_ROPE))
    ang = jnp.arange(s, dtype=F32)[:, None] * inv[None, :]
    cos, sin = jnp.cos(ang), jnp.sin(ang)
    return jnp.concatenate([cos, cos], axis=1), jnp.concatenate([sin, sin], axis=1)


def kernel(x, p, ln_mix, ln_ffn, ab_w_in, mla_q_norm, mla_w_uq, mla_kv_norm, mla_w_ukv, gdn_conv_w, gdn_a_log, gdn_dt_bias, gdn_norm, ab_w_out, cd_w_in, ssd_conv_w, ssd_conv_b, ssd_dt_bias, ssd_a_log, ssd_d, ssd_norm, rwkv_mu, rwkv_w0, rwkv_w2, rwkv_a0, rwkv_a2, rwkv_g2, rwkv_k_k, rwkv_k_a, rwkv_r_k, rwkv_ln_w, rwkv_ln_b, cd_w_out, peer_w_q, peer_keys, peer_u, peer_v, ple_w_proj, ple_norm, ple_w_gate, final_norm):
    assert x.shape[0] == 1
    s = x.shape[1]
    tm = min(512, s)
    tt = min(256, s)
    cos2, sin2 = _rope_tables(s)
    u_all, vt_all = prep_peer_tables(peer_u, peer_v)
    xs = x[0]
    for i in range(DEPTH):
        j = i // 2
        if i % 2 == 0:
            wts = prep_ab(ab_w_in[j], mla_w_uq[j], mla_w_ukv[j], ab_w_out[j])
            xs = mixer_ab(xs, ln_mix[i], *wts, mla_q_norm[j], mla_kv_norm[j], gdn_conv_w[j], gdn_a_log[j],
                          gdn_dt_bias[j], gdn_norm[j], cos2, sin2, tm=tm, t_attn=tm, t_gdn=tm)
        else:
            wts = prep_cd(cd_w_in[j], cd_w_out[j])
            xs = mixer_cd(xs, ln_mix[i], *wts, ssd_conv_w[j], ssd_conv_b[j], ssd_dt_bias[j], ssd_a_log[j], ssd_d[j],
                          ssd_norm[j], rwkv_mu[j], rwkv_w0[j], rwkv_w2[j], rwkv_a0[j], rwkv_a2[j], rwkv_g2[j],
                          rwkv_k_k[j], rwkv_k_a[j], rwkv_r_k[j], rwkv_ln_w[j], rwkv_ln_b[j], tm=tm, tt=tt)
        yt = peer(xs, ln_ffn[i], peer_w_q[i], peer_keys[i], u_all, vt_all, i, tm=tm, tt=tt)
        xs = ple_update(xs, yt, p[i, 0], ple_norm[i], ple_w_gate[i].astype(BF16), ple_w_proj[i].astype(BF16), tm=tm)
    return rmsnorm(xs, final_norm, tm=tm)[None]
```

```python
import functools
import math

import jax
import jax.numpy as jnp
from jax import lax
from jax.experimental import pallas as pl
from jax.experimental.pallas import tpu as pltpu

F32 = jnp.float32
BF16 = jnp.bfloat16
HIGHEST = lax.Precision.HIGHEST

D_MODEL = 2048
DEPTH = 4
PLE_DIM = 256
RMS_EPS = 1e-6
MLA_HEADS = 8
MLA_Q_RANK = 512
MLA_KV_RANK = 256
MLA_NOPE = 128
MLA_ROPE = 64
MLA_V = 128
ROPE_THETA = 10000.0
GDN_HEADS = 8
GDN_DK = 128
GDN_DV = 128
GDN_CHUNK = 64
SSD_HEADS = 16
SSD_HEADDIM = 64
SSD_GROUPS = 2
SSD_STATE = 128
SSD_CHUNK = 128
SSD_INNER = SSD_HEADS * SSD_HEADDIM
RWKV_HEADS = 16
RWKV_HEAD = 64
RWKV_INNER = RWKV_HEADS * RWKV_HEAD
RWKV_W_LORA = 64
RWKV_A_LORA = 64
RWKV_G_LORA = 160
RWKV_GN_EPS = 64e-5
CONV_WIDTH = 4
PEER_HEADS = 8
PEER_NKEYS = 128
PEER_EXPERTS = PEER_NKEYS * PEER_NKEYS
PEER_QDIM = 256
PEER_TOPK = 16

LANES = 128
SUBLANES = 8
VMEM_LIMIT = 56 * 1024 * 1024


def _params(*sem):
    return pltpu.CompilerParams(dimension_semantics=sem, vmem_limit_bytes=VMEM_LIMIT)


def _bdot(a, b):
    return jnp.dot(a.astype(BF16), b.astype(BF16), preferred_element_type=F32)


def _bdot_nt(a, b):
    return lax.dot_general(a.astype(BF16), b.astype(BF16), (((1,), (1,)), ((), ())), preferred_element_type=F32)


def _hdot(a, b):
    return jnp.dot(a, b, preferred_element_type=F32, precision=HIGHEST)


def _hdot_nt(a, b):
    return lax.dot_general(a, b, (((1,), (1,)), ((), ())), preferred_element_type=F32, precision=HIGHEST)


def _rms(x, gain):
    return x * lax.rsqrt(jnp.mean(x * x, axis=-1, keepdims=True) + RMS_EPS) * gain


def _sigmoid(x):
    return 1.0 / (1.0 + jnp.exp(-x))


def _silu(x):
    return x * _sigmoid(x)


def _softplus(x):
    return jnp.maximum(x, 0.0) + jnp.log(1.0 + jnp.exp(-jnp.abs(x)))


def _nm_kernel(*refs, has_norm, has_res):
    it = iter(refs)
    x_ref = next(it)
    g_ref = next(it) if has_norm else None
    w_ref = next(it)
    r_ref = next(it) if has_res else None
    o_ref = next(it)
    xn_ref = next(it)

    @pl.when(pl.program_id(1) == 0)
    def _():
        x = x_ref[...].astype(F32)
        if has_norm:
            x = _rms(x, g_ref[...])
        xn_ref[...] = x.astype(BF16)

    acc = jnp.dot(xn_ref[...], w_ref[...], preferred_element_type=F32)
    if has_res:
        acc = acc + r_ref[...]
    o_ref[...] = acc.astype(o_ref.dtype)


def norm_matmul(x, w, gain=None, residual=None, tm=512, tn=512, out_dtype=F32, x_col=0):
    m = x.shape[0]
    k, n = w.shape
    tn = min(tn, n)
    assert m % tm == 0 and n % tn == 0 and x_col % k == 0
    in_specs = [pl.BlockSpec((tm, k), lambda i, j: (i, x_col // k))]
    args = [x]
    if gain is not None:
        in_specs.append(pl.BlockSpec((1, k), lambda i, j: (0, 0)))
        args.append(gain.reshape(1, k).astype(F32))
    in_specs.append(pl.BlockSpec((k, tn), lambda i, j: (0, j)))
    args.append(w)
    if residual is not None:
        in_specs.append(pl.BlockSpec((tm, tn), lambda i, j: (i, j)))
        args.append(residual)
    return pl.pallas_call(
        functools.partial(_nm_kernel, has_norm=gain is not None, has_res=residual is not None),
        grid=(m // tm, n // tn),
        in_specs=in_specs,
        out_specs=pl.BlockSpec((tm, tn), lambda i, j: (i, j)),
        out_shape=jax.ShapeDtypeStruct((m, n), out_dtype),
        scratch_shapes=[pltpu.VMEM((tm, k), BF16)],
        compiler_params=_params("parallel", "arbitrary"),
        name="norm_matmul",
    )(*args)


def _out_proj_kernel(a_ref, b_ref, wa_ref, wb_ref, r_ref, o_ref):
    o_ref[...] = r_ref[...] + _bdot(a_ref[...], wa_ref[...]) + _bdot(b_ref[...], wb_ref[...])


def out_proj(a, b, w, residual, tm=512, tn=512):
    m, ka = a.shape
    n = w.shape[1]
    assert b.shape[1] == ka and w.shape[0] == 2 * ka
    return pl.pallas_call(
        _out_proj_kernel,
        grid=(m // tm, n // tn),
        in_specs=[
            pl.BlockSpec((tm, ka), lambda i, j: (i, 0)),
            pl.BlockSpec((tm, ka), lambda i, j: (i, 0)),
            pl.BlockSpec((ka, tn), lambda i, j: (0, j)),
            pl.BlockSpec((ka, tn), lambda i, j: (1, j)),
            pl.BlockSpec((tm, tn), lambda i, j: (i, j)),
        ],
        out_specs=pl.BlockSpec((tm, tn), lambda i, j: (i, j)),
        out_shape=jax.ShapeDtypeStruct((m, n), F32),
        compiler_params=_params("parallel", "arbitrary"),
        name="out_proj",
    )(a, b, w, w, residual)


def _ple_kernel(x_ref, yt_ref, g_ref, wg_ref, p_ref, wp_ref, o_ref, xn_ref, xs_ref):
    j = pl.program_id(1)
    tn = o_ref.shape[1]

    @pl.when(j == 0)
    def _():
        x = x_ref[...] + yt_ref[...].T
        xn_ref[...] = _rms(x, g_ref[...]).astype(BF16)
        for jj in range(xs_ref.shape[0]):
            xs_ref[jj] = x[:, jj * tn:(jj + 1) * tn]

    gate = _sigmoid(jnp.dot(xn_ref[...], wg_ref[...], preferred_element_type=F32))
    emb = _bdot(p_ref[...], wp_ref[...])
    o_ref[...] = xs_ref[j] + gate * emb


def ple_update(x, yt, p_i, norm_g, w_gate, w_proj, tm=512, tn=512):
    m, d = x.shape
    pd = p_i.shape[1]
    return pl.pallas_call(
        _ple_kernel,
        grid=(m // tm, d // tn),
        in_specs=[
            pl.BlockSpec((tm, d), lambda i, j: (i, 0)),
            pl.BlockSpec((d, tm), lambda i, j: (0, i)),
            pl.BlockSpec((1, d), lambda i, j: (0, 0)),
            pl.BlockSpec((d, tn), lambda i, j: (0, j)),
            pl.BlockSpec((tm, pd), lambda i, j: (i, 0)),
            pl.BlockSpec((pd, tn), lambda i, j: (0, j)),
        ],
        out_specs=pl.BlockSpec((tm, tn), lambda i, j: (i, j)),
        out_shape=jax.ShapeDtypeStruct((m, d), F32),
        scratch_shapes=[pltpu.VMEM((tm, d), BF16), pltpu.VMEM((d // tn, tm, tn), F32)],
        compiler_params=_params("parallel", "arbitrary"),
        name="ple_update",
    )(x, yt, norm_g.reshape(1, d), w_gate, p_i, w_proj)


def _rmsnorm_kernel(x_ref, g_ref, o_ref):
    o_ref[...] = _rms(x_ref[...], g_ref[...])


def rmsnorm(x, gain, tm=512):
    m, d = x.shape
    return pl.pallas_call(
        _rmsnorm_kernel,
        grid=(m // tm,),
        in_specs=[pl.BlockSpec((tm, d), lambda i: (i, 0)), pl.BlockSpec((1, d), lambda i: (0, 0))],
        out_specs=pl.BlockSpec((tm, d), lambda i: (i, 0)),
        out_shape=jax.ShapeDtypeStruct((m, d), F32),
        compiler_params=_params("parallel"),
        name="rmsnorm",
    )(x, gain.reshape(1, d))


def _peer_fold_kernel(keys_ref, wq_ref, o_ref):
    o_ref[0, 0] = _hdot_nt(keys_ref[0, 0], wq_ref[...])


def peer_fold(w_q, keys):
    hk = PEER_QDIM // 2
    return pl.pallas_call(
        _peer_fold_kernel,
        grid=(2, PEER_HEADS),
        in_specs=[
            pl.BlockSpec((1, 1, PEER_NKEYS, hk), lambda c, h: (h, c, 0, 0)),
            pl.BlockSpec((D_MODEL, hk), lambda c, h: (0, h * 2 + c)),
        ],
        out_specs=pl.BlockSpec((1, 1, PEER_NKEYS, D_MODEL), lambda c, h: (c, h, 0, 0)),
        out_shape=jax.ShapeDtypeStruct((2, PEER_HEADS, PEER_NKEYS, D_MODEL), F32),
        compiler_params=_params("parallel", "parallel"),
        name="peer_fold",
    )(keys, w_q)


def _rmsnorm_t_kernel(x_ref, g_ref, o_ref):
    o_ref[...] = _rms(x_ref[...], g_ref[...]).T.astype(BF16)


def rmsnorm_t(x, gain, tm=512):
    m, d = x.shape
    return pl.pallas_call(
        _rmsnorm_t_kernel,
        grid=(m // tm,),
        in_specs=[pl.BlockSpec((tm, d), lambda i: (i, 0)), pl.BlockSpec((1, d), lambda i: (0, 0))],
        out_specs=pl.BlockSpec((d, tm), lambda i: (0, i)),
        out_shape=jax.ShapeDtypeStruct((d, m), BF16),
        compiler_params=_params("parallel"),
        name="rmsnorm_t",
    )(x, gain.reshape(1, d))


def _sort_desc(v):
    v = list(v)
    n = len(v)
    k = 2
    while k <= n:
        j = k // 2
        while j >= 1:
            for i in range(n):
                l = i ^ j
                if l > i:
                    hi, lo = jnp.maximum(v[i], v[l]), jnp.minimum(v[i], v[l])
                    v[i], v[l] = (hi, lo) if (i & k) == 0 else (lo, hi)
            j //= 2
        k *= 2
    return v


def _merge_top(a, b):
    n = len(a)
    v = [jnp.maximum(a[i], b[n - 1 - i]) for i in range(n)]
    j = n // 2
    while j >= 1:
        for i in range(n):
            l = i ^ j
            if l > i:
                v[i], v[l] = jnp.maximum(v[i], v[l]), jnp.minimum(v[i], v[l])
        j //= 2
    return v


def _top_sorted(vals, n):
    vals = list(vals)
    while len(vals) % n:
        vals.append(jnp.full_like(vals[0], -jnp.inf))
    acc = _sort_desc(vals[:n])
    for g in range(1, len(vals) // n):
        acc = _merge_top(acc, _sort_desc(vals[g * n:(g + 1) * n]))
    return acc


def _count_leading(pred, b):
    t0 = pred(b[15])
    t1 = pred(b[7])
    t2 = pred(jnp.where(t1, b[11], b[3]))
    t3 = pred(jnp.where(t1, jnp.where(t2, b[13], b[9]), jnp.where(t2, b[5], b[1])))
    hi = jnp.where(t2, jnp.where(t3, b[14], b[12]), jnp.where(t3, b[10], b[8]))
    lo = jnp.where(t2, jnp.where(t3, b[6], b[4]), jnp.where(t3, b[2], b[0]))
    t4 = pred(jnp.where(t1, hi, lo))
    cnt = (jnp.where(t1, 8.0, 0.0) + jnp.where(t2, 4.0, 0.0)) + (jnp.where(t3, 2.0, 0.0) + jnp.where(t4, 1.0, 0.0))
    return jnp.where(t0, 16.0, cnt)


_PEER_PAIRS = [(i, j) for i in range(PEER_TOPK) for j in range(PEER_TOPK) if (i + 1) * (j + 1) <= PEER_TOPK]


def _peer_select_kernel(wf_ref, ht_ref, e1_ref, n1_ref, r2_ref, e2_ref, sub_ref):
    nk, k = PEER_NKEYS, PEER_TOPK
    tt = ht_ref.shape[1]
    ht = ht_ref[...]
    for c in range(2):
        sub_ref[c] = jnp.dot(wf_ref[c], ht, preferred_element_type=F32)
    row = lax.broadcasted_iota(jnp.int32, (SUBLANES, LANES), 0)

    def head_row(v, h):
        return jnp.sum(jnp.where(row == h, v, 0.0), axis=0, keepdims=True)

    def lane_group(lg, carry):
        lanes = pl.ds(pl.multiple_of(lg * LANES, LANES), LANES)

        def top_of_head(h, packed, c):
            base = pl.multiple_of(h * nk, nk)
            slabs = [sub_ref[c, pl.ds(base + SUBLANES * j, SUBLANES), lanes] for j in range(nk // SUBLANES)]
            top = _sort_desc(slabs)
            for sh in (4, 2, 1):
                top = _merge_top(top, [pltpu.roll(t, sh, 0) for t in top])
            return tuple(jnp.where(row == h, top[i], packed[i]) for i in range(k))

        zero = tuple(jnp.zeros((SUBLANES, LANES), F32) for _ in range(k))
        a = lax.fori_loop(0, PEER_HEADS, functools.partial(top_of_head, c=0), zero)
        b = lax.fori_loop(0, PEER_HEADS, functools.partial(top_of_head, c=1), zero)
        best = _top_sorted([a[i] + b[j] for i, j in _PEER_PAIRS], k)
        thr, vmax = best[k - 1], best[0]
        z = jnp.zeros((SUBLANES, LANES), F32)
        for i in range(k):
            z = z + jnp.exp(best[i] - vmax)
        inv_z = 1.0 / z

        def emit(h, carry):
            base = pl.multiple_of(h * nk, nk)
            s1 = sub_ref[0, pl.ds(base, nk), lanes]
            s2 = sub_ref[1, pl.ds(base, nk), lanes]
            thr_h = head_row(thr, h)
            bh = [head_row(b[j], h) for j in range(k)]
            n1 = _count_leading(lambda x: s1 + x >= thr_h, bh)
            r2 = _count_leading(lambda x: x > s2, bh)
            e1_ref[h, :, lanes] = jnp.exp(s1 - head_row(a[0], h)) * head_row(inv_z, h)
            n1_ref[h, :, lanes] = n1
            r2_ref[h, :, lanes] = r2.astype(BF16)
            e2_ref[h, :, lanes] = jnp.exp(s2 - head_row(b[0], h)).astype(BF16)
            return carry

        lax.fori_loop(0, PEER_HEADS, emit, 0)
        return carry

    lax.fori_loop(0, tt // LANES, lane_group, 0)


def peer_select(wf, ht, tt=256):
    d, s = ht.shape
    nrow = PEER_HEADS * PEER_NKEYS
    shape = (PEER_HEADS, PEER_NKEYS, s)
    ospec = pl.BlockSpec((PEER_HEADS, PEER_NKEYS, tt), lambda i: (0, 0, i))
    return pl.pallas_call(
        _peer_select_kernel,
        grid=(s // tt,),
        in_specs=[pl.BlockSpec((2, nrow, d), lambda i: (0, 0, 0)), pl.BlockSpec((d, tt), lambda i: (0, i))],
        out_specs=[ospec] * 4,
        out_shape=[jax.ShapeDtypeStruct(shape, F32)] * 2 + [jax.ShapeDtypeStruct(shape, BF16)] * 2,
        scratch_shapes=[pltpu.VMEM((2, nrow, tt), F32)],
        compiler_params=_params("parallel"),
        name="peer_select",
    )(wf, ht)


def _gelu_tanh(x):
    return 0.5 * x * (1.0 + jnp.tanh(math.sqrt(2.0 / math.pi) * (x + 0.044715 * (x * x * x))))


def _peer_dense_kernel(u_ref, ht_ref, vt_ref, e1_ref, n1_ref, r2_ref, e2_ref, o_ref, act0_ref, act1_ref, ga_ref):
    nk = PEER_NKEYS
    j = pl.program_id(1)
    eb = u_ref.shape[0]
    nblk = eb // nk

    @pl.when(j == 0)
    def _():
        o_ref[...] = jnp.zeros_like(o_ref)
        act1_ref[...] = jnp.zeros_like(act1_ref)

    done = jnp.maximum(j - 1, 0)

    def step(prev_ref, next_ref):
        for ii in range(nblk):
            i1 = done * nblk + ii
            act = _gelu_tanh(prev_ref[pl.ds(ii * nk, nk), :])
            gate = None
            for h in range(PEER_HEADS):
                n1 = n1_ref[h, pl.ds(i1, 1), :].astype(BF16)
                e1 = e1_ref[h, pl.ds(i1, 1), :].astype(BF16)
                g = jnp.where(r2_ref[h] < n1, e2_ref[h] * e1, jnp.zeros((), BF16))
                gate = g if gate is None else gate + g
            ga_ref[pl.ds(ii * nk, nk), :] = (gate.astype(F32) * act).astype(BF16)
        next_ref[...] = jnp.dot(u_ref[...], ht_ref[...], preferred_element_type=F32)
        o_ref[...] += jnp.dot(vt_ref[...], ga_ref[...], preferred_element_type=F32)

    pl.when(j % 2 == 0)(functools.partial(step, act1_ref, act0_ref))
    pl.when(j % 2 == 1)(functools.partial(step, act0_ref, act1_ref))


PEER_EB = 1024


def peer_dense(u_all, ht, vt_all, layer, e1, n1, r2, e2, tt=512):
    d, s = ht.shape
    ne, eb = u_all.shape[1], PEER_EB
    last = ne // eb - 1
    gspec = pl.BlockSpec((PEER_HEADS, PEER_NKEYS, tt), lambda i, j: (0, 0, i))
    return pl.pallas_call(
        _peer_dense_kernel,
        grid=(s // tt, ne // eb + 1),
        in_specs=[
            pl.BlockSpec((None, eb, d), lambda i, j: (layer, jnp.minimum(j, last), 0)),
            pl.BlockSpec((d, tt), lambda i, j: (0, i)),
            pl.BlockSpec((None, None, d, eb), lambda i, j: (layer, jnp.maximum(j - 1, 0), 0, 0)),
            gspec, gspec, gspec, gspec,
        ],
        out_specs=pl.BlockSpec((d, tt), lambda i, j: (0, i)),
        out_shape=jax.ShapeDtypeStruct((d, s), F32),
        scratch_shapes=[pltpu.VMEM((eb, tt), F32), pltpu.VMEM((eb, tt), F32), pltpu.VMEM((eb, tt), BF16)],
        compiler_params=_params("parallel", "arbitrary"),
        name="peer_dense",
    )(u_all, ht, vt_all, e1, n1, r2, e2)


def prep_peer_tables(peer_u, peer_v):
    nl, ne, d = peer_v.shape
    vt_all = peer_v.astype(BF16).reshape(nl, ne // PEER_EB, PEER_EB, d).transpose(0, 1, 3, 2)
    return peer_u.astype(BF16), vt_all


def peer(x, ln_g, w_q, keys, u_all, vt_all, layer, tm=512, tt=256):
    wf = peer_fold(w_q, keys).reshape(2, PEER_HEADS * PEER_NKEYS, D_MODEL).astype(BF16)
    ht = rmsnorm_t(x, ln_g, tm=tm)
    return peer_dense(u_all, ht, vt_all, layer, *peer_select(wf, ht, tt=tt), tt=tm)


MLA_HEAD_GROUP = 4


def _mla_kernel(qi_ref, ki_ref, qn_ref, qr_ref, qrot_ref, cq_ref, sq_ref, kn_ref, kr_ref, krot_ref, ck_ref, sk_ref,
                vt_ref, o_ref, q1_ref, q2_ref, m_ref, l_ref, acc_ref, *, scale):
    t = pl.program_id(1)
    qi, ki = qi_ref[t], ki_ref[t]
    tq, tk = qn_ref.shape[1], kn_ref.shape[0]
    ratio = tq // tk
    dn, dr, dv = MLA_NOPE, MLA_ROPE, MLA_V
    heads = range(qn_ref.shape[0] // dn)

    @pl.when(ki == 0)
    def _():
        q1_ref[...] = (qn_ref[...] * scale).astype(BF16)
        for h in heads:
            rows = pl.ds(h * dr, dr)
            q2_ref[rows, :] = ((qr_ref[rows, :] * cq_ref[...] + qrot_ref[rows, :] * sq_ref[...]) * scale).astype(BF16)
        m_ref[...] = jnp.full_like(m_ref, -jnp.inf)
        l_ref[...] = jnp.zeros_like(l_ref)
        acc_ref[...] = jnp.zeros_like(acc_ref)

    def step(masked):
        kr = (kr_ref[...] * ck_ref[...] + krot_ref[...] * sk_ref[...]).astype(BF16)
        s = [jnp.dot(kn_ref[:, h * dn:(h + 1) * dn], q1_ref[h * dn:(h + 1) * dn, :], preferred_element_type=F32)
             + jnp.dot(kr, q2_ref[h * dr:(h + 1) * dr, :], preferred_element_type=F32) for h in heads]
        if masked:
            kpos = ki * tk + lax.broadcasted_iota(jnp.int32, (tk, tq), 0)
            qpos = qi * tq + lax.broadcasted_iota(jnp.int32, (tk, tq), 1)
            keep = kpos <= qpos
            s = [jnp.where(keep, x, -jnp.inf) for x in s]
        m_old = [m_ref[h] for h in heads]
        m_new = [jnp.maximum(a, jnp.max(x, axis=0, keepdims=True)) for a, x in zip(m_old, s)]
        alpha = [jnp.exp(a - b) for a, b in zip(m_old, m_new)]
        p = [jnp.exp(x - b) for x, b in zip(s, m_new)]
        for h in heads:
            rows = pl.ds(h * dv, dv)
            l_ref[h] = alpha[h] * l_ref[h] + jnp.sum(p[h], axis=0, keepdims=True)
            acc_ref[rows, :] = alpha[h] * acc_ref[rows, :] + jnp.dot(vt_ref[rows, :], p[h].astype(BF16),
                                                                       preferred_element_type=F32)
            m_ref[h] = m_new[h]

    pl.when(ki < qi * ratio)(functools.partial(step, False))
    pl.when(ki >= qi * ratio)(functools.partial(step, True))

    @pl.when(ki == (qi + 1) * ratio - 1)
    def _():
        for h in heads:
            rows = pl.ds(h * dv, dv)
            o_ref[rows, :] = acc_ref[rows, :] / l_ref[h]


def mla_attention(qt, kv, vt, kr, krot, cos, sin, cos_t, sin_t, tq=1024, tk=512):
    s = kv.shape[0]
    tq, tk = min(tq, s), min(tk, s)
    ratio = tq // tk
    hh, dn, dr = MLA_HEADS, MLA_NOPE, MLA_ROPE
    pairs = [(qi, ki) for qi in range(s // tq) for ki in range((qi + 1) * ratio)]
    qi_tab = jnp.array([pr[0] for pr in pairs], jnp.int32)
    ki_tab = jnp.array([pr[1] for pr in pairs], jnp.int32)
    g = MLA_HEAD_GROUP
    r0 = hh * dn // (g * dr)
    qmap = lambda h, t, qt_, kt_: (0, qt_[t])
    kmap = lambda h, t, qt_, kt_: (kt_[t], 0)
    grid_spec = pltpu.PrefetchScalarGridSpec(
        num_scalar_prefetch=2,
        grid=(hh // g, len(pairs)),
        in_specs=[
            pl.BlockSpec((g * dn, tq), lambda h, t, qt_, kt_: (h, qt_[t])),
            pl.BlockSpec((g * dr, tq), lambda h, t, qt_, kt_: (r0 + h, qt_[t])),
            pl.BlockSpec((g * dr, tq), lambda h, t, qt_, kt_: (r0 + hh // g + h, qt_[t])),
            pl.BlockSpec((dr, tq), qmap),
            pl.BlockSpec((dr, tq), qmap),
            pl.BlockSpec((tk, g * dn), lambda h, t, qt_, kt_: (kt_[t], h)),
            pl.BlockSpec((tk, dr), kmap),
            pl.BlockSpec((tk, dr), kmap),
            pl.BlockSpec((tk, dr), kmap),
            pl.BlockSpec((tk, dr), kmap),
            pl.BlockSpec((g * MLA_V, tk), lambda h, t, qt_, kt_: (h, kt_[t])),
        ],
        out_specs=pl.BlockSpec((g * MLA_V, tq), lambda h, t, qt_, kt_: (h, qt_[t])),
        scratch_shapes=[pltpu.VMEM((g * dn, tq), BF16), pltpu.VMEM((g * dr, tq), BF16), pltpu.VMEM((g, 1, tq), F32),
                        pltpu.VMEM((g, 1, tq), F32), pltpu.VMEM((g * MLA_V, tq), F32)],
    )
    return pl.pallas_call(
        functools.partial(_mla_kernel, scale=(MLA_NOPE + MLA_ROPE) ** -0.5),
        grid_spec=grid_spec,
        out_shape=jax.ShapeDtypeStruct((hh * MLA_V, s), F32),
        compiler_params=_params("parallel", "arbitrary"),
        name="mla_attention",
    )(qi_tab, ki_tab, qt, qt, qt, cos_t, sin_t, kv, kr, krot, cos, sin, vt)


def _causal_conv(cur_ref, prev_ref, w_ref, first, bias=None):
    prev = jnp.where(first, 0.0, prev_ref[...])
    xe = jnp.concatenate([prev, cur_ref[...]], axis=0)
    w = w_ref[...]
    acc = w[CONV_WIDTH - 1:CONV_WIDTH] * xe[SUBLANES:]
    for j in range(CONV_WIDTH - 1):
        acc = acc + w[j:j + 1] * pltpu.roll(xe, CONV_WIDTH - 1 - j, 0)[SUBLANES:]
    return acc if bias is None else acc + bias


def _unit_lower_inverses(lows, c):
    n = lows[0].shape[0]
    r = lax.broadcasted_iota(jnp.int32, (n, n), 0)
    q = lax.broadcasted_iota(jnp.int32, (n, n), 1)
    eye = (r == q).astype(F32)
    prev = [jnp.where((r // 16) == (q // 16), low, 0.0) for low in lows]
    ps = [-d for d in prev]
    xs = [eye + p for p in ps]
    for _ in range(3):
        ps = [_bdot(p, p) for p in ps]
        xs = [x + _bdot(x, p) for x, p in zip(xs, ps)]
    size = 32
    while size <= c:
        cur = [jnp.where((r // size) == (q // size), low, 0.0) for low in lows] if size < c else lows
        ts = [_bdot(x, cu - pr) for x, cu, pr in zip(xs, cur, prev)]
        xs = [x - _bdot(t, x) for x, t in zip(xs, ts)]
        prev = cur
        size *= 2
    return xs


GDN_HEAD_GROUP = 4


def _gdn_kernel(q_ref, qp_ref, k_ref, kp_ref, v_ref, vp_ref, z_ref, wq_ref, wk_ref, wv_ref,
                ar_ref, br_ref, alog_ref, dtb_ref, ng_ref, o_ref, st_ref):
    c = GDN_CHUNK
    tt = q_ref.shape[0]
    heads = range(q_ref.shape[1] // GDN_DK)
    first = pl.program_id(1) == 0

    @pl.when(first)
    def _():
        st_ref[...] = jnp.zeros_like(st_ref)

    def l2n(x):
        return x * lax.rsqrt(jnp.sum(x * x, axis=-1, keepdims=True) + 1e-6)

    lanes = lambda x, h: x[:, h * GDN_DK:(h + 1) * GDN_DK]
    q_all = _silu(_causal_conv(q_ref, qp_ref, wq_ref, first))
    k_all = _silu(_causal_conv(k_ref, kp_ref, wk_ref, first))
    v_all = _silu(_causal_conv(v_ref, vp_ref, wv_ref, first))
    q = [l2n(lanes(q_all, h)) * (GDN_DK ** -0.5) for h in heads]
    k = [l2n(lanes(k_all, h)) for h in heads]
    v = [lanes(v_all, h) for h in heads]
    neg_a = [-jnp.exp(alog_ref[h, :, 0:1]) for h in heads]
    dtb = [dtb_ref[h, :, 0:1] for h in heads]
    nb = 2 * c
    r = lax.broadcasted_iota(jnp.int32, (nb, nb), 0)
    cc = lax.broadcasted_iota(jnp.int32, (nb, nb), 1)
    same = (r // c) == (cc // c)
    incl, strict = same & (r >= cc), same & (r > cc)
    incl_t = same & (r <= cc)
    top = lax.broadcasted_iota(jnp.int32, (nb, 1), 0) < c
    nblocks = tt // nb
    items = [(n, h) for n in range(nblocks) for h in heads]
    sl = lambda n: slice(n * nb, (n + 1) * nb)
    qs = [q[h][sl(n)] for n, h in items]
    ks = [k[h][sl(n)] for n, h in items]
    vs = [v[h][sl(n)] for n, h in items]
    as_col = lambda row: jnp.sum(jnp.where(r == cc, row, 0.0), axis=1, keepdims=True)
    g_row = [neg_a[h] * _softplus(ar_ref[h, n] + dtb[h]) for n, h in items]
    g_col = [as_col(g) for g in g_row]
    beta = [as_col(_sigmoid(br_ref[h, n])) for n, h in items]
    gc = [jnp.sum(jnp.where(incl, g, 0.0), axis=1, keepdims=True) for g in g_row]
    gr = [jnp.sum(jnp.where(incl_t, g, 0.0), axis=0, keepdims=True) for g in g_col]
    decay = [jnp.where(incl, jnp.exp(jnp.where(incl, a - b, 0.0)), 0.0) for a, b in zip(gc, gr)]
    kk = [_bdot_nt(x, x) for x in ks]
    qk = [_bdot_nt(a, b) for a, b in zip(qs, ks)]
    inv = _unit_lower_inverses([jnp.where(strict, b * m * d, 0.0) for b, m, d in zip(beta, kk, decay)], c)
    eg = [jnp.exp(g) for g in gc]
    sol = [_bdot(x, jnp.concatenate([b * vv, (b * e) * kx], axis=1)) for x, b, e, vv, kx in zip(inv, beta, eg, vs, ks)]
    a_qk = [m * d for m, d in zip(qk, decay)]
    k_end = [kx * jnp.exp(jnp.where(top, g[c - 1:c], g[nb - 1:nb]) - g) for g, kx in zip(gc, ks)]
    k_end_t = [[ke[:c].T, ke[c:].T] for ke in k_end]
    q_dec = [a * e for a, e in zip(qs, eg)]
    state = [st_ref[h] for h in heads]
    us = [[None, None] for _ in items]
    o_state = [[None, None] for _ in items]
    for n in range(nblocks):
        for j in range(2):
            cs = slice(j * c, (j + 1) * c)
            for h in heads:
                it = n * len(heads) + h
                u = sol[it][cs, :GDN_DV] - _bdot(sol[it][cs, GDN_DV:], state[h])
                o_state[it][j] = _bdot(q_dec[it][cs], state[h])
                state[h] = jnp.exp(gc[it][(j + 1) * c - 1:(j + 1) * c]) * state[h] + _bdot(k_end_t[it][j], u)
                us[it][j] = u
    for h in heads:
        st_ref[h] = state[h]
    o_blk = [jnp.concatenate(o_state[it], axis=0) + _bdot(a_qk[it], jnp.concatenate(us[it], axis=0))
             for it in range(len(items))]
    o = jnp.concatenate([_rms(jnp.concatenate([o_blk[n * len(heads) + h] for n in range(nblocks)], axis=0), ng_ref[...])
                         for h in heads], axis=1)
    o_ref[...] = o * _silu(z_ref[...])


def gated_delta_net(proj, col0, a_raw, b_raw, conv_w, a_log, dt_bias, norm_g, tt=256):
    s = proj.shape[0]
    tt = min(tt, s)
    hh, c, g = GDN_HEADS, GDN_CHUNK, GDN_HEAD_GROUP
    w = g * GDN_DK
    assert col0 % w == 0
    b0 = col0 // w
    nblk = GDN_HEADS * GDN_DK // w

    def cur(grp):
        return pl.BlockSpec((tt, w), lambda h, i: (i, b0 + grp * nblk + h))

    def prev(grp):
        return pl.BlockSpec((SUBLANES, w), lambda h, i: (jnp.maximum(i * (tt // SUBLANES) - 1, 0), b0 + grp * nblk + h))

    def wspec(grp):
        return pl.BlockSpec((CONV_WIDTH, w), lambda h, i: (0, grp * nblk + h))

    a_t, b_t = a_raw.T, b_raw.T
    c = 2 * c
    rowspec = pl.BlockSpec((g, tt // c, 1, c), lambda h, i: (h, i, 0, 0))
    hspec = pl.BlockSpec((g, 1, LANES), lambda h, i: (h, 0, 0))
    bcast = lambda p: jnp.broadcast_to(p.astype(F32)[:, None, None], (hh, 1, LANES))
    return pl.pallas_call(
        _gdn_kernel,
        grid=(hh // g, s // tt),
        in_specs=[cur(0), prev(0), cur(1), prev(1), cur(2), prev(2), cur(3), wspec(0), wspec(1), wspec(2),
                  rowspec, rowspec, hspec, hspec, pl.BlockSpec((1, GDN_DV), lambda h, i: (0, 0))],
        out_specs=pl.BlockSpec((tt, w), lambda h, i: (i, h)),
        out_shape=jax.ShapeDtypeStruct((s, hh * GDN_DV), F32),
        scratch_shapes=[pltpu.VMEM((g, GDN_DK, GDN_DV), F32)],
        compiler_params=_params("parallel", "arbitrary"),
        name="gated_delta_net",
    )(proj, proj, proj, proj, proj, proj, proj, conv_w, conv_w, conv_w,
      a_t.reshape(hh, s // c, 1, c), b_t.reshape(hh, s // c, 1, c),
      bcast(a_log), bcast(dt_bias), norm_g.reshape(1, GDN_DV).astype(F32))


def _rot_half_cols(w, half):
    return jnp.concatenate([-w[..., half:], w[..., :half]], axis=-1)


def _pad_cols(w, n):
    return jnp.pad(w, ((0, 0), (0, n - w.shape[1])))


AB_QKVZ = 0
AB_CQ = 4 * GDN_HEADS * GDN_DK
AB_CKV = AB_CQ + MLA_Q_RANK
AB_KR = AB_CKV + MLA_KV_RANK
AB_GATES = AB_KR + 2 * MLA_ROPE
AB_PAD = AB_GATES + LANES


def prep_ab(w_in, w_uq, w_ukv, w_out):
    rq, rkv, rr = MLA_Q_RANK, MLA_KV_RANK, MLA_ROPE
    w_kr = w_in[:, rq + rkv:rq + rkv + rr]
    gdn_w = 4 * GDN_HEADS * GDN_DK
    g0 = rq + rkv + rr
    w_main = jnp.concatenate([w_in[:, g0:g0 + gdn_w], w_in[:, :rq + rkv], w_kr, _rot_half_cols(w_kr, rr // 2),
                              w_in[:, g0 + gdn_w:]], axis=1)
    w_main = _pad_cols(w_main, AB_PAD).astype(BF16)
    uq = w_uq.reshape(rq, MLA_HEADS, MLA_NOPE + MLA_ROPE)
    uq_r = uq[..., MLA_NOPE:]
    uq2 = jnp.concatenate([uq[..., :MLA_NOPE].reshape(rq, -1), uq_r.reshape(rq, -1),
                           _rot_half_cols(uq_r, rr // 2).reshape(rq, -1)], axis=1).astype(BF16)
    ukv = w_ukv.reshape(rkv, MLA_HEADS, MLA_NOPE + MLA_V)
    ukv2 = jnp.concatenate([ukv[..., :MLA_NOPE].reshape(rkv, -1), ukv[..., MLA_NOPE:].reshape(rkv, -1)], axis=1).astype(BF16)
    return w_main, uq2, ukv2, w_out.astype(BF16)


def mixer_ab(x, ln, w_main, uq2, ukv2, w_out, q_norm, kv_norm, conv_w, a_log, dt_bias, gdn_norm, cos2, sin2,
             tm=512, t_attn=512, t_gdn=256):
    s = x.shape[0]
    rq, rkv, rr = MLA_Q_RANK, MLA_KV_RANK, MLA_ROPE
    proj = norm_matmul(x, w_main, gain=ln, tm=min(2 * tm, x.shape[0]))
    qfull = norm_matmul(proj, uq2, gain=q_norm, tm=tm, x_col=AB_CQ)
    kv = norm_matmul(proj, ukv2, gain=kv_norm, tm=tm, out_dtype=BF16, x_col=AB_CKV)
    nn = MLA_HEADS * MLA_NOPE
    o_a = mla_attention(qfull.T, kv, kv[:, nn:].T, proj[:, AB_KR:AB_KR + rr], proj[:, AB_KR + rr:AB_GATES],
                        cos2, sin2, cos2.T, sin2.T, tq=2 * t_attn, tk=t_attn).T
    o_b = gated_delta_net(proj, AB_QKVZ, proj[:, AB_GATES:AB_GATES + GDN_HEADS],
                          proj[:, AB_GATES + GDN_HEADS:AB_GATES + 2 * GDN_HEADS], conv_w, a_log, dt_bias, gdn_norm, tt=t_gdn)
    return out_proj(o_a, o_b, w_out, x, tm=min(2 * tm, x.shape[0]))


CD_R, CD_K, CD_V, CD_Z, CD_DT, CD_X = (i * 1024 for i in range(6))
CD_B = 6144
CD_C = CD_B + SSD_GROUPS * SSD_STATE
CD_WA = CD_C + SSD_GROUPS * SSD_STATE
CD_G = CD_WA + LANES
CD_PAD = 7168
GROUP_W = SSD_INNER // SSD_GROUPS


def _lower_ones(c):
    r = lax.broadcasted_iota(jnp.int32, (c, c), 0)
    q = lax.broadcasted_iota(jnp.int32, (c, c), 1)
    return r >= q, r > q


def _ssd_kernel(x_ref, xp_ref, b_ref, bp_ref, c_ref, cp_ref, z_ref, dt_ref, wx_ref, wb_ref, wc_ref,
                bx_ref, bb_ref, bc_ref, dtb_ref, alog_ref, dskip_ref, ng_ref, o_ref, st_ref):
    c = SSD_CHUNK
    tt = x_ref.shape[0]
    first = pl.program_id(1) == 0

    @pl.when(first)
    def _():
        st_ref[...] = jnp.zeros_like(st_ref)

    xs_all = _silu(_causal_conv(x_ref, xp_ref, wx_ref, first, bx_ref[...]))
    bm_all = _silu(_causal_conv(b_ref, bp_ref, wb_ref, first, bb_ref[...]))
    cm_all = _silu(_causal_conv(c_ref, cp_ref, wc_ref, first, bc_ref[...]))
    dt_all = _softplus(dt_ref[...] + dtb_ref[...])
    a_all = -jnp.exp(alog_ref[...]) * dt_all
    incl, _ = _lower_ones(c)
    tri = incl.astype(F32)
    left = lax.broadcasted_iota(jnp.int32, (c, LANES), 1) < SSD_HEADDIM
    npair = GROUP_W // LANES
    outs = []
    for n in range(tt // c):
        sl = slice(n * c, (n + 1) * c)
        xs, bm, cm, dt = xs_all[sl], bm_all[sl], cm_all[sl], dt_all[sl]
        acs = _bdot(tri, a_all[sl])
        xdt = xs * dt
        cb = _bdot_nt(cm, bm)
        bm_t = bm.T
        ys = []
        for p in range(npair):
            ls = slice(p * LANES, (p + 1) * LANES)
            acs_p = acs[:, ls]
            acs_t = acs_p.T
            xp = xdt[:, ls]
            yd = []
            for hd in range(2):
                col = acs_p[:, hd * SSD_HEADDIM:hd * SSD_HEADDIM + 1]
                row = acs_t[hd * SSD_HEADDIM:hd * SSD_HEADDIM + 1, :]
                lmat = jnp.where(incl, jnp.exp(jnp.where(incl, col - row, 0.0)), 0.0)
                yd.append(_bdot(cb * lmat, xp))
            last = acs_p[c - 1:c]
            prev_t = st_ref[p]
            y_off = _bdot(cm, prev_t) * jnp.exp(acs_p)
            st_ref[p] = jnp.exp(last) * prev_t + _bdot(bm_t, xp * jnp.exp(last - acs_p))
            ys.append(jnp.where(left, yd[0], yd[1]) + y_off)
        outs.append(jnp.concatenate(ys, axis=1) + xs * dskip_ref[...])
    y = jnp.concatenate(outs, axis=0) * _silu(z_ref[...])
    o_ref[...] = _rms(y, ng_ref[...])


def mamba2_ssd(proj, conv_w, conv_b, dt_bias, a_log, d_skip, norm_g, tt=256):
    s = proj.shape[0]
    tt = min(tt, s)
    gw, ns = GROUP_W, SSD_STATE
    per = lambda v: jnp.repeat(v.astype(F32), SSD_HEADDIM).reshape(1, SSD_INNER)

    def cur(col, w):
        return pl.BlockSpec((tt, w), lambda g, i: (i, col // w + g))

    def prev(col, w):
        return pl.BlockSpec((SUBLANES, w), lambda g, i: (jnp.maximum(i * (tt // SUBLANES) - 1, 0), col // w + g))

    def par(rows, col, w):
        return pl.BlockSpec((rows, w), lambda g, i: (0, col // w + g))

    cb = conv_b.reshape(1, -1).astype(F32)
    return pl.pallas_call(
        _ssd_kernel,
        grid=(SSD_GROUPS, s // tt),
        in_specs=[cur(CD_X, gw), prev(CD_X, gw), cur(CD_B, ns), prev(CD_B, ns), cur(CD_C, ns), prev(CD_C, ns),
                  cur(CD_Z, gw), cur(CD_DT, gw),
                  par(CONV_WIDTH, 0, gw), par(CONV_WIDTH, SSD_INNER, ns), par(CONV_WIDTH, SSD_INNER + SSD_GROUPS * ns, ns),
                  par(1, 0, gw), par(1, SSD_INNER, ns), par(1, SSD_INNER + SSD_GROUPS * ns, ns),
                  par(1, 0, gw), par(1, 0, gw), par(1, 0, gw), par(1, 0, gw)],
        out_specs=pl.BlockSpec((tt, gw), lambda g, i: (i, g)),
        out_shape=jax.ShapeDtypeStruct((s, SSD_INNER), F32),
        scratch_shapes=[pltpu.VMEM((gw // LANES, ns, LANES), F32)],
        compiler_params=_params("parallel", "arbitrary"),
        name="mamba2_ssd",
    )(proj, proj, proj, proj, proj, proj, proj, proj, conv_w, conv_w, conv_w, cb, cb, cb,
      per(dt_bias), per(a_log), per(d_skip), norm_g.reshape(1, SSD_INNER).astype(F32))


def _pair_ones():
    r = lax.broadcasted_iota(jnp.int32, (LANES, LANES), 0)
    q = lax.broadcasted_iota(jnp.int32, (LANES, LANES), 1)
    return (r // RWKV_HEAD) == (q // RWKV_HEAD)


def _head_sums(x, ones):
    return jnp.concatenate([_hdot(x[:, i * LANES:(i + 1) * LANES], ones) for i in range(x.shape[1] // LANES)], axis=1)


def _rwkv_prep_kernel(r_ref, rp_ref, k_ref, kp_ref, v_ref, vp_ref, wa_ref, wap_ref, g0_ref, g0p_ref, g1_ref, g1p_ref,
                      mur_ref, muk_ref, muv_ref, muwa_ref, mug0_ref, mug1_ref, w0_ref, w2_ref, a0_ref, a2_ref, g2_ref,
                      kk_ref, ka_ref, rk_ref,
                      ro_ref, lw_ref, ko_ref, vo_ref, po_ref, qo_ref, go_ref, bo_ref):
    first = pl.program_id(0) == 0

    def mix(cur_ref, prev_ref, mu_ref):
        cur = cur_ref[...]
        prev = jnp.where(first, 0.0, prev_ref[...])
        shifted = pltpu.roll(jnp.concatenate([prev, cur], axis=0), 1, 0)[SUBLANES:]
        return cur + (shifted - cur) * mu_ref[...]

    r = mix(r_ref, rp_ref, mur_ref)
    k = mix(k_ref, kp_ref, muk_ref)
    v = mix(v_ref, vp_ref, muv_ref)
    wa = mix(wa_ref, wap_ref, muwa_ref)
    g0 = mix(g0_ref, g0p_ref, mug0_ref)
    g1 = mix(g1_ref, g1p_ref, mug1_ref)
    log_w = -math.exp(-0.5) * _sigmoid(w0_ref[...] + _bdot(jnp.tanh(wa), w2_ref[...]))
    a = _sigmoid(a0_ref[...] + _bdot(wa, a2_ref[...]))
    gate = _bdot(_sigmoid(g0), g2_ref[0:LANES, :]) + _bdot(_sigmoid(g1), g2_ref[LANES:2 * LANES, :])
    ones = _pair_ones().astype(F32)
    kx = k * kk_ref[...]
    kk = kx * lax.rsqrt(_head_sums(kx * kx, ones) + 1e-6)
    k_mod = k * (1.0 + (a - 1.0) * ka_ref[...])
    ro_ref[...] = r
    lw_ref[...] = log_w
    ko_ref[...] = k_mod
    vo_ref[...] = v
    po_ref[...] = -kk * a
    qo_ref[...] = kk
    go_ref[...] = gate
    bo_ref[...] = _head_sums(r * k_mod * rk_ref[...], ones) * v


def rwkv_prep(proj, mu, w0, w2, a0, a2, g2, k_k, k_a, r_k, tt=256):
    s = proj.shape[0]
    tt = min(tt, s)
    ri = RWKV_INNER
    row = lambda v: v.reshape(1, -1).astype(F32)
    mu_r, mu_k, mu_v = (row(mu[i * ri:(i + 1) * ri]) for i in range(3))
    mu_wa = row(mu[3 * ri:3 * ri + LANES])
    mu_g = row(jnp.pad(mu[3 * ri + LANES:], (0, 2 * LANES - RWKV_G_LORA)))
    zeros = jnp.zeros((RWKV_W_LORA, ri), F32)
    w2p = jnp.concatenate([w2, zeros], axis=0).astype(BF16)
    a2p = jnp.concatenate([zeros, a2], axis=0).astype(BF16)
    g2p = jnp.pad(g2, ((0, 2 * LANES - RWKV_G_LORA), (0, 0))).astype(BF16)

    def cur(col, w):
        return pl.BlockSpec((tt, w), lambda i: (i, col // w))

    def prev(col, w):
        return pl.BlockSpec((SUBLANES, w), lambda i: (jnp.maximum(i * (tt // SUBLANES) - 1, 0), col // w))

    full = lambda a: pl.BlockSpec(a.shape, lambda i: (0, 0))
    params = [mu_r, mu_k, mu_v, mu_wa, mu_g[:, :LANES], mu_g[:, LANES:], row(w0), w2p, row(a0), a2p, g2p,
              row(k_k), row(k_a), row(r_k)]
    out = jax.ShapeDtypeStruct((s, ri), F32)
    return pl.pallas_call(
        _rwkv_prep_kernel,
        grid=(s // tt,),
        in_specs=[cur(CD_R, ri), prev(CD_R, ri), cur(CD_K, ri), prev(CD_K, ri), cur(CD_V, ri), prev(CD_V, ri),
                  cur(CD_WA, LANES), prev(CD_WA, LANES), cur(CD_G, LANES), prev(CD_G, LANES),
                  cur(CD_G + LANES, LANES), prev(CD_G + LANES, LANES)] + [full(a) for a in params],
        out_specs=[pl.BlockSpec((tt, ri), lambda i: (i, 0))] * 8,
        out_shape=[out] * 8,
        compiler_params=_params("arbitrary"),
        name="rwkv_prep",
    )(*([proj] * 12), *params)


RWKV_MY_CHUNK = 64


def _rwkv_scan_kernel(r_ref, lw_ref, k_ref, v_ref, p_ref, q_ref, g_ref, b_ref, lnw_ref, lnb_ref, o_ref, st_ref):
    c = RWKV_MY_CHUNK
    tt = r_ref.shape[0]

    @pl.when(pl.program_id(1) == 0)
    def _():
        st_ref[...] = jnp.zeros_like(st_ref)

    tri = _lower_ones(c)[0].astype(F32)
    left = lax.broadcasted_iota(jnp.int32, (c, LANES), 1) < RWKV_HEAD
    pair = _pair_ones()
    ones = pair.astype(F32)
    r_id = lax.broadcasted_iota(jnp.int32, (LANES, LANES), 0)
    c_id = lax.broadcasted_iota(jnp.int32, (LANES, LANES), 1)
    eye = r_id == c_id
    top = r_id < c
    strict = pair & ((r_id % c) > (c_id % c))
    incl = pair & ((r_id % c) >= (c_id % c))
    stack2 = lambda a: jnp.concatenate([a, a], axis=0)
    by_head = lambda a: jnp.concatenate([jnp.where(left, a, 0.0), jnp.where(left, 0.0, a)], axis=0)
    unstack = lambda a: jnp.where(left, a[:c], a[c:])
    pairs = range(r_ref.shape[1] // LANES)
    nchunks = tt // c
    items = [(n, pr) for n in range(nchunks) for pr in pairs]
    blk = lambda ref, n, pr: ref[n * c:(n + 1) * c, pr * LANES:(pr + 1) * LANES]
    w = [blk(lw_ref, n, pr) for n, pr in items]
    v = [blk(v_ref, n, pr) for n, pr in items]
    lw = [_bdot(tri, x) for x in w]
    lam_in = [jnp.exp(x) for x in lw]
    inv_lam = [jnp.exp(-x) for x in lw]
    q_bar = [blk(q_ref, n, pr) * jnp.exp(a - b) for (n, pr), a, b in zip(items, lw, w)]
    r_bar = [blk(r_ref, n, pr) * x for (n, pr), x in zip(items, lam_in)]
    pk = [jnp.concatenate([blk(p_ref, n, pr) * x, blk(k_ref, n, pr) * x], axis=0) for (n, pr), x in zip(items, inv_lam)]
    mq = [_bdot_nt(by_head(a), b) for a, b in zip(q_bar, pk)]
    mr = [_bdot_nt(by_head(a), b) for a, b in zip(r_bar, pk)]
    mq_sw = [pltpu.roll(x, c, 1) for x in mq]
    mr_sw = [pltpu.roll(x, c, 1) for x in mr]
    m_qp = [jnp.where(strict, jnp.where(top, a, b), 0.0) for a, b in zip(mq, mq_sw)]
    m_qk = [jnp.where(strict, jnp.where(top, b, a), 0.0) for a, b in zip(mq, mq_sw)]
    m_rp = [jnp.where(incl, jnp.where(top, a, b), 0.0) for a, b in zip(mr, mr_sw)]
    m_rk = [jnp.where(incl, jnp.where(top, b, a), 0.0) for a, b in zip(mr, mr_sw)]
    vv = [stack2(x) for x in v]
    inv = _unit_lower_inverses([-x for x in m_qp], c)
    qkv = [_bdot(a, b) for a, b in zip(m_qk, vv)]
    sol = [_bdot(x, jnp.concatenate([stack2(a), b], axis=1)) for x, a, b in zip(inv, q_bar, qkv)]
    ws = [unstack(x[:, :LANES]) for x in sol]
    wv = [unstack(x[:, LANES:]) for x in sol]
    y_loc = [unstack(_bdot(a, b)) for a, b in zip(m_rk, vv)]
    lam_end = [x[c - 1:c] for x in lam_in]
    lam_col = [jnp.sum(jnp.where(eye, x, 0.0), axis=1, keepdims=True) for x in lam_end]
    pk_end_t = [(a * b).T for a, b in zip(pk, lam_end)]
    state = [st_ref[pr] for pr in pairs]
    us, y_state = [], []
    for it, (n, pr) in enumerate(items):
        u = _bdot(ws[it], state[pr]) + wv[it]
        y_state.append(_bdot(r_bar[it], state[pr]))
        upd = _bdot(pk_end_t[it], jnp.concatenate([u, v[it]], axis=0))
        state[pr] = lam_col[it] * state[pr] + jnp.where(pair, upd, 0.0)
        us.append(u)
    for pr in pairs:
        st_ref[pr] = state[pr]
    outs = [a + unstack(_bdot(b, stack2(u))) + d for a, b, u, d in zip(y_state, m_rp, us, y_loc)]
    ys = []
    for pr in pairs:
        y = jnp.concatenate([outs[n * len(pairs) + pr] for n in range(nchunks)], axis=0)
        mean = _hdot(y, ones) * (1.0 / RWKV_HEAD)
        yc = y - mean
        var = _hdot(yc * yc, ones) * (1.0 / RWKV_HEAD)
        ys.append(yc * lax.rsqrt(var + RWKV_GN_EPS))
    y = jnp.concatenate(ys, axis=1) * lnw_ref[...] + lnb_ref[...]
    o_ref[...] = (y + b_ref[...]) * g_ref[...]


RWKV_PAIR_GROUP = 4


def rwkv_scan(r, lw, k, v, p, q, gate, bonus, ln_w, ln_b, tt=256):
    s = r.shape[0]
    tt = min(tt, s)
    w = RWKV_PAIR_GROUP * LANES
    spec = pl.BlockSpec((tt, w), lambda h, i: (i, h))
    pspec = pl.BlockSpec((1, w), lambda h, i: (0, h))
    return pl.pallas_call(
        _rwkv_scan_kernel,
        grid=(RWKV_INNER // w, s // tt),
        in_specs=[spec] * 8 + [pspec] * 2,
        out_specs=spec,
        out_shape=jax.ShapeDtypeStruct((s, RWKV_INNER), F32),
        scratch_shapes=[pltpu.VMEM((RWKV_PAIR_GROUP, LANES, LANES), F32)],
        compiler_params=_params("parallel", "arbitrary"),
        name="rwkv_scan",
    )(r, lw, k, v, p, q, gate, bonus, ln_w.reshape(1, -1).astype(F32), ln_b.reshape(1, -1).astype(F32))


def prep_cd(w_in, w_out):
    si, ri = SSD_INNER, RWKV_INNER
    z, xbc, dt, rw = w_in[:, :si], w_in[:, si:2 * si + 512], w_in[:, 2 * si + 512:2 * si + 528], w_in[:, 2 * si + 528:]
    dt_exp = jnp.repeat(dt, SSD_HEADDIM, axis=1)
    cols = [rw[:, :3 * ri], z, dt_exp, xbc, rw[:, 3 * ri:]]
    return _pad_cols(jnp.concatenate(cols, axis=1), CD_PAD).astype(BF16), w_out.astype(BF16)


def mixer_cd(x, ln, w_main, w_out, ssd_conv_w, ssd_conv_b, ssd_dt_bias, ssd_a_log, ssd_d, ssd_norm,
             mu, w0, w2, a0, a2, g2, k_k, k_a, r_k, ln_w, ln_b, tm=512, tt=256):
    proj = norm_matmul(x, w_main, gain=ln, tm=min(2 * tm, x.shape[0]))
    o_c = mamba2_ssd(proj, ssd_conv_w, ssd_conv_b, ssd_dt_bias, ssd_a_log, ssd_d, ssd_norm, tt=tt)
    o_d = rwkv_scan(*rwkv_prep(proj, mu, w0, w2, a0, a2, g2, k_k, k_a, r_k.reshape(-1), tt=tt), ln_w, ln_b, tt=2 * tt)
    return out_proj(o_c, o_d, w_out, x, tm=min(2 * tm, x.shape[0]))


def _rope_tables(s):
    inv = 1.0 / (ROPE_THETA ** (jnp.arange(0, MLA_ROPE, 2, dtype=F32) / MLA_ROPE))
    ang = jnp.arange(s, dtype=F32)[:, None] * inv[None, :]
    cos, sin = jnp.cos(ang), jnp.sin(ang)
    return jnp.concatenate([cos, cos], axis=1), jnp.concatenate([sin, sin], axis=1)


def kernel(x, p, ln_mix, ln_ffn, ab_w_in, mla_q_norm, mla_w_uq, mla_kv_norm, mla_w_ukv, gdn_conv_w, gdn_a_log, gdn_dt_bias, gdn_norm, ab_w_out, cd_w_in, ssd_conv_w, ssd_conv_b, ssd_dt_bias, ssd_a_log, ssd_d, ssd_norm, rwkv_mu, rwkv_w0, rwkv_w2, rwkv_a0, rwkv_a2, rwkv_g2, rwkv_k_k, rwkv_k_a, rwkv_r_k, rwkv_ln_w, rwkv_ln_b, cd_w_out, peer_w_q, peer_keys, peer_u, peer_v, ple_w_proj, ple_norm, ple_w_gate, final_norm):
    assert x.shape[0] == 1
    s = x.shape[1]
    tm = min(512, s)
    tt = min(256, s)
    cos2, sin2 = _rope_tables(s)
    u_all, vt_all = prep_peer_tables(peer_u, peer_v)
    xs = x[0]
    for i in range(DEPTH):
        j = i // 2
        if i % 2 == 0:
            wts = prep_ab(ab_w_in[j], mla_w_uq[j], mla_w_ukv[j], ab_w_out[j])
            xs = mixer_ab(xs, ln_mix[i], *wts, mla_q_norm[j], mla_kv_norm[j], gdn_conv_w[j], gdn_a_log[j],
                          gdn_dt_bias[j], gdn_norm[j], cos2, sin2, tm=tm, t_attn=tm, t_gdn=tm)
        else:
            wts = prep_cd(cd_w_in[j], cd_w_out[j])
            xs = mixer_cd(xs, ln_mix[i], *wts, ssd_conv_w[j], ssd_conv_b[j], ssd_dt_bias[j], ssd_a_log[j], ssd_d[j],
                          ssd_norm[j], rwkv_mu[j], rwkv_w0[j], rwkv_w2[j], rwkv_a0[j], rwkv_a2[j], rwkv_g2[j],
                          rwkv_k_k[j], rwkv_k_a[j], rwkv_r_k[j], rwkv_ln_w[j], rwkv_ln_b[j], tm=tm, tt=tt)
        yt = peer(xs, ln_ffn[i], peer_w_q[i], peer_keys[i], u_all, vt_all, i, tm=tm, tt=tt)
        xs = ple_update(xs, yt, p[i, 0], ple_norm[i], ple_w_gate[i].astype(BF16), ple_w_proj[i].astype(BF16), tm=tm)
    return rmsnorm(xs, final_norm, tm=tm)[None]
```

```python
import functools
import math

import jax
import jax.numpy as jnp
from jax import lax
from jax.experimental import pallas as pl
from jax.experimental.pallas import tpu as pltpu

F32 = jnp.float32
BF16 = jnp.bfloat16
HIGHEST = lax.Precision.HIGHEST

D_MODEL = 2048
DEPTH = 4
PLE_DIM = 256
RMS_EPS = 1e-6
MLA_HEADS = 8
MLA_Q_RANK = 512
MLA_KV_RANK = 256
MLA_NOPE = 128
MLA_ROPE = 64
MLA_V = 128
ROPE_THETA = 10000.0
GDN_HEADS = 8
GDN_DK = 128
GDN_DV = 128
GDN_CHUNK = 64
SSD_HEADS = 16
SSD_HEADDIM = 64
SSD_GROUPS = 2
SSD_STATE = 128
SSD_CHUNK = 128
SSD_INNER = SSD_HEADS * SSD_HEADDIM
RWKV_HEADS = 16
RWKV_HEAD = 64
RWKV_INNER = RWKV_HEADS * RWKV_HEAD
RWKV_W_LORA = 64
RWKV_A_LORA = 64
RWKV_G_LORA = 160
RWKV_GN_EPS = 64e-5
CONV_WIDTH = 4
PEER_HEADS = 8
PEER_NKEYS = 128
PEER_EXPERTS = PEER_NKEYS * PEER_NKEYS
PEER_QDIM = 256
PEER_TOPK = 16

LANES = 128
SUBLANES = 8
VMEM_LIMIT = 56 * 1024 * 1024


def _params(*sem):
    return pltpu.CompilerParams(dimension_semantics=sem, vmem_limit_bytes=VMEM_LIMIT)


def _bdot(a, b):
    return jnp.dot(a.astype(BF16), b.astype(BF16), preferred_element_type=F32)


def _bdot_nt(a, b):
    return lax.dot_general(a.astype(BF16), b.astype(BF16), (((1,), (1,)), ((), ())), preferred_element_type=F32)


def _hdot(a, b):
    return jnp.dot(a, b, preferred_element_type=F32, precision=HIGHEST)


def _hdot_nt(a, b):
    return lax.dot_general(a, b, (((1,), (1,)), ((), ())), preferred_element_type=F32, precision=HIGHEST)


def _rms(x, gain):
    return x * lax.rsqrt(jnp.mean(x * x, axis=-1, keepdims=True) + RMS_EPS) * gain


def _sigmoid(x):
    return 1.0 / (1.0 + jnp.exp(-x))


def _silu(x):
    return x * _sigmoid(x)


def _softplus(x):
    return jnp.maximum(x, 0.0) + jnp.log(1.0 + jnp.exp(-jnp.abs(x)))


def _nm_kernel(*refs, has_norm, has_res):
    it = iter(refs)
    x_ref = next(it)
    g_ref = next(it) if has_norm else None
    w_ref = next(it)
    r_ref = next(it) if has_res else None
    o_ref = next(it)
    xn_ref = next(it)

    @pl.when(pl.program_id(1) == 0)
    def _():
        x = x_ref[...].astype(F32)
        if has_norm:
            x = _rms(x, g_ref[...])
        xn_ref[...] = x.astype(BF16)

    acc = jnp.dot(xn_ref[...], w_ref[...], preferred_element_type=F32)
    if has_res:
        acc = acc + r_ref[...]
    o_ref[...] = acc.astype(o_ref.dtype)


def norm_matmul(x, w, gain=None, residual=None, tm=512, tn=512, out_dtype=F32, x_col=0):
    m = x.shape[0]
    k, n = w.shape
    tn = min(tn, n)
    assert m % tm == 0 and n % tn == 0 and x_col % k == 0
    in_specs = [pl.BlockSpec((tm, k), lambda i, j: (i, x_col // k))]
    args = [x]
    if gain is not None:
        in_specs.append(pl.BlockSpec((1, k), lambda i, j: (0, 0)))
        args.append(gain.reshape(1, k).astype(F32))
    in_specs.append(pl.BlockSpec((k, tn), lambda i, j: (0, j)))
    args.append(w)
    if residual is not None:
        in_specs.append(pl.BlockSpec((tm, tn), lambda i, j: (i, j)))
        args.append(residual)
    return pl.pallas_call(
        functools.partial(_nm_kernel, has_norm=gain is not None, has_res=residual is not None),
        grid=(m // tm, n // tn),
        in_specs=in_specs,
        out_specs=pl.BlockSpec((tm, tn), lambda i, j: (i, j)),
        out_shape=jax.ShapeDtypeStruct((m, n), out_dtype),
        scratch_shapes=[pltpu.VMEM((tm, k), BF16)],
        compiler_params=_params("parallel", "arbitrary"),
        name="norm_matmul",
    )(*args)


def _out_proj_kernel(a_ref, b_ref, wa_ref, wb_ref, r_ref, o_ref):
    o_ref[...] = r_ref[...] + _bdot(a_ref[...], wa_ref[...]) + _bdot(b_ref[...], wb_ref[...])


def out_proj(a, b, w, residual, tm=512, tn=512):
    m, ka = a.shape
    n = w.shape[1]
    assert b.shape[1] == ka and w.shape[0] == 2 * ka
    return pl.pallas_call(
        _out_proj_kernel,
        grid=(m // tm, n // tn),
        in_specs=[
            pl.BlockSpec((tm, ka), lambda i, j: (i, 0)),
            pl.BlockSpec((tm, ka), lambda i, j: (i, 0)),
            pl.BlockSpec((ka, tn), lambda i, j: (0, j)),
            pl.BlockSpec((ka, tn), lambda i, j: (1, j)),
            pl.BlockSpec((tm, tn), lambda i, j: (i, j)),
        ],
        out_specs=pl.BlockSpec((tm, tn), lambda i, j: (i, j)),
        out_shape=jax.ShapeDtypeStruct((m, n), F32),
        compiler_params=_params("parallel", "arbitrary"),
        name="out_proj",
    )(a, b, w, w, residual)


def _ple_kernel(x_ref, yt_ref, g_ref, wg_ref, p_ref, wp_ref, o_ref, xn_ref, xs_ref):
    j = pl.program_id(1)
    tn = o_ref.shape[1]

    @pl.when(j == 0)
    def _():
        x = x_ref[...] + yt_ref[...].T
        xn_ref[...] = _rms(x, g_ref[...]).astype(BF16)
        for jj in range(xs_ref.shape[0]):
            xs_ref[jj] = x[:, jj * tn:(jj + 1) * tn]

    gate = _sigmoid(jnp.dot(xn_ref[...], wg_ref[...], preferred_element_type=F32))
    emb = _bdot(p_ref[...], wp_ref[...])
    o_ref[...] = xs_ref[j] + gate * emb


def ple_update(x, yt, p_i, norm_g, w_gate, w_proj, tm=512, tn=512):
    m, d = x.shape
    pd = p_i.shape[1]
    return pl.pallas_call(
        _ple_kernel,
        grid=(m // tm, d // tn),
        in_specs=[
            pl.BlockSpec((tm, d), lambda i, j: (i, 0)),
            pl.BlockSpec((d, tm), lambda i, j: (0, i)),
            pl.BlockSpec((1, d), lambda i, j: (0, 0)),
            pl.BlockSpec((d, tn), lambda i, j: (0, j)),
            pl.BlockSpec((tm, pd), lambda i, j: (i, 0)),
            pl.BlockSpec((pd, tn), lambda i, j: (0, j)),
        ],
        out_specs=pl.BlockSpec((tm, tn), lambda i, j: (i, j)),
        out_shape=jax.ShapeDtypeStruct((m, d), F32),
        scratch_shapes=[pltpu.VMEM((tm, d), BF16), pltpu.VMEM((d // tn, tm, tn), F32)],
        compiler_params=_params("parallel", "arbitrary"),
        name="ple_update",
    )(x, yt, norm_g.reshape(1, d), w_gate, p_i, w_proj)


def _rmsnorm_kernel(x_ref, g_ref, o_ref):
    o_ref[...] = _rms(x_ref[...], g_ref[...])


def rmsnorm(x, gain, tm=512):
    m, d = x.shape
    return pl.pallas_call(
        _rmsnorm_kernel,
        grid=(m // tm,),
        in_specs=[pl.BlockSpec((tm, d), lambda i: (i, 0)), pl.BlockSpec((1, d), lambda i: (0, 0))],
        out_specs=pl.BlockSpec((tm, d), lambda i: (i, 0)),
        out_shape=jax.ShapeDtypeStruct((m, d), F32),
        compiler_params=_params("parallel"),
        name="rmsnorm",
    )(x, gain.reshape(1, d))


def _peer_fold_kernel(keys_ref, wq_ref, o_ref):
    o_ref[0, 0] = _hdot_nt(keys_ref[0, 0], wq_ref[...])


def peer_fold(w_q, keys):
    hk = PEER_QDIM // 2
    return pl.pallas_call(
        _peer_fold_kernel,
        grid=(2, PEER_HEADS),
        in_specs=[
            pl.BlockSpec((1, 1, PEER_NKEYS, hk), lambda c, h: (h, c, 0, 0)),
            pl.BlockSpec((D_MODEL, hk), lambda c, h: (0, h * 2 + c)),
        ],
        out_specs=pl.BlockSpec((1, 1, PEER_NKEYS, D_MODEL), lambda c, h: (c, h, 0, 0)),
        out_shape=jax.ShapeDtypeStruct((2, PEER_HEADS, PEER_NKEYS, D_MODEL), F32),
        compiler_params=_params("parallel", "parallel"),
        name="peer_fold",
    )(keys, w_q)


def _rmsnorm_t_kernel(x_ref, g_ref, o_ref):
    o_ref[...] = _rms(x_ref[...], g_ref[...]).T.astype(BF16)


def rmsnorm_t(x, gain, tm=512):
    m, d = x.shape
    return pl.pallas_call(
        _rmsnorm_t_kernel,
        grid=(m // tm,),
        in_specs=[pl.BlockSpec((tm, d), lambda i: (i, 0)), pl.BlockSpec((1, d), lambda i: (0, 0))],
        out_specs=pl.BlockSpec((d, tm), lambda i: (0, i)),
        out_shape=jax.ShapeDtypeStruct((d, m), BF16),
        compiler_params=_params("parallel"),
        name="rmsnorm_t",
    )(x, gain.reshape(1, d))


def _sort_desc(v):
    v = list(v)
    n = len(v)
    k = 2
    while k <= n:
        j = k // 2
        while j >= 1:
            for i in range(n):
                l = i ^ j
                if l > i:
                    hi, lo = jnp.maximum(v[i], v[l]), jnp.minimum(v[i], v[l])
                    v[i], v[l] = (hi, lo) if (i & k) == 0 else (lo, hi)
            j //= 2
        k *= 2
    return v


def _merge_top(a, b):
    n = len(a)
    v = [jnp.maximum(a[i], b[n - 1 - i]) for i in range(n)]
    j = n // 2
    while j >= 1:
        for i in range(n):
            l = i ^ j
            if l > i:
                v[i], v[l] = jnp.maximum(v[i], v[l]), jnp.minimum(v[i], v[l])
        j //= 2
    return v


def _top_sorted(vals, n):
    vals = list(vals)
    while len(vals) % n:
        vals.append(jnp.full_like(vals[0], -jnp.inf))
    acc = _sort_desc(vals[:n])
    for g in range(1, len(vals) // n):
        acc = _merge_top(acc, _sort_desc(vals[g * n:(g + 1) * n]))
    return acc


def _count_leading(pred, b):
    t0 = pred(b[15])
    t1 = pred(b[7])
    t2 = pred(jnp.where(t1, b[11], b[3]))
    t3 = pred(jnp.where(t1, jnp.where(t2, b[13], b[9]), jnp.where(t2, b[5], b[1])))
    hi = jnp.where(t2, jnp.where(t3, b[14], b[12]), jnp.where(t3, b[10], b[8]))
    lo = jnp.where(t2, jnp.where(t3, b[6], b[4]), jnp.where(t3, b[2], b[0]))
    t4 = pred(jnp.where(t1, hi, lo))
    cnt = (jnp.where(t1, 8.0, 0.0) + jnp.where(t2, 4.0, 0.0)) + (jnp.where(t3, 2.0, 0.0) + jnp.where(t4, 1.0, 0.0))
    return jnp.where(t0, 16.0, cnt)


_PEER_PAIRS = [(i, j) for i in range(PEER_TOPK) for j in range(PEER_TOPK) if (i + 1) * (j + 1) <= PEER_TOPK]


def _peer_select_kernel(wf_ref, ht_ref, e1_ref, n1_ref, r2_ref, e2_ref, sub_ref):
    nk, k = PEER_NKEYS, PEER_TOPK
    tt = ht_ref.shape[1]
    ht = ht_ref[...]
    for c in range(2):
        sub_ref[c] = jnp.dot(wf_ref[c], ht, preferred_element_type=F32)
    row = lax.broadcasted_iota(jnp.int32, (SUBLANES, LANES), 0)

    def head_row(v, h):
        return jnp.sum(jnp.where(row == h, v, 0.0), axis=0, keepdims=True)

    def lane_group(lg, carry):
        lanes = pl.ds(pl.multiple_of(lg * LANES, LANES), LANES)

        def top_of_head(h, packed, c):
            base = pl.multiple_of(h * nk, nk)
            slabs = [sub_ref[c, pl.ds(base + SUBLANES * j, SUBLANES), lanes] for j in range(nk // SUBLANES)]
            top = _sort_desc(slabs)
            for sh in (4, 2, 1):
                top = _merge_top(top, [pltpu.roll(t, sh, 0) for t in top])
            return tuple(jnp.where(row == h, top[i], packed[i]) for i in range(k))

        zero = tuple(jnp.zeros((SUBLANES, LANES), F32) for _ in range(k))
        a = lax.fori_loop(0, PEER_HEADS, functools.partial(top_of_head, c=0), zero)
        b = lax.fori_loop(0, PEER_HEADS, functools.partial(top_of_head, c=1), zero)
        best = _top_sorted([a[i] + b[j] for i, j in _PEER_PAIRS], k)
        thr, vmax = best[k - 1], best[0]
        z = jnp.zeros((SUBLANES, LANES), F32)
        for i in range(k):
            z = z + jnp.exp(best[i] - vmax)
        inv_z = 1.0 / z

        def emit(h, carry):
            base = pl.multiple_of(h * nk, nk)
            s1 = sub_ref[0, pl.ds(base, nk), lanes]
            s2 = sub_ref[1, pl.ds(base, nk), lanes]
            thr_h = head_row(thr, h)
            bh = [head_row(b[j], h) for j in range(k)]
            n1 = _count_leading(lambda x: s1 + x >= thr_h, bh)
            r2 = _count_leading(lambda x: x > s2, bh)
            e1_ref[h, :, lanes] = jnp.exp(s1 - head_row(a[0], h)) * head_row(inv_z, h)
            n1_ref[h, :, lanes] = n1
            r2_ref[h, :, lanes] = r2.astype(BF16)
            e2_ref[h, :, lanes] = jnp.exp(s2 - head_row(b[0], h)).astype(BF16)
            return carry

        lax.fori_loop(0, PEER_HEADS, emit, 0)
        return carry

    lax.fori_loop(0, tt // LANES, lane_group, 0)


def peer_select(wf, ht, tt=256):
    d, s = ht.shape
    nrow = PEER_HEADS * PEER_NKEYS
    shape = (PEER_HEADS, PEER_NKEYS, s)
    ospec = pl.BlockSpec((PEER_HEADS, PEER_NKEYS, tt), lambda i: (0, 0, i))
    return pl.pallas_call(
        _peer_select_kernel,
        grid=(s // tt,),
        in_specs=[pl.BlockSpec((2, nrow, d), lambda i: (0, 0, 0)), pl.BlockSpec((d, tt), lambda i: (0, i))],
        out_specs=[ospec] * 4,
        out_shape=[jax.ShapeDtypeStruct(shape, F32)] * 2 + [jax.ShapeDtypeStruct(shape, BF16)] * 2,
        scratch_shapes=[pltpu.VMEM((2, nrow, tt), F32)],
        compiler_params=_params("parallel"),
        name="peer_select",
    )(wf, ht)


def _gelu_tanh(x):
    return 0.5 * x * (1.0 + jnp.tanh(math.sqrt(2.0 / math.pi) * (x + 0.044715 * (x * x * x))))


def _peer_dense_kernel(u_ref, ht_ref, vt_ref, e1_ref, n1_ref, r2_ref, e2_ref, o_ref, act0_ref, act1_ref, ga_ref):
    nk = PEER_NKEYS
    j = pl.program_id(1)
    last = pl.num_programs(1) - 1
    eb = u_ref.shape[0]
    nblk = eb // nk
    done = jnp.maximum(j - 1, 0)

    def step(prev_ref, next_ref):
        if prev_ref is not None:
            for ii in range(nblk):
                i1 = done * nblk + ii
                act = _gelu_tanh(prev_ref[pl.ds(ii * nk, nk), :].astype(BF16))
                gate = None
                for h in range(PEER_HEADS):
                    n1 = n1_ref[h, pl.ds(i1, 1), :].astype(BF16)
                    e1 = e1_ref[h, pl.ds(i1, 1), :].astype(BF16)
                    g = jnp.where(r2_ref[h] < n1, e2_ref[h] * e1, jnp.zeros((), BF16))
                    gate = g if gate is None else gate + g
                ga_ref[pl.ds(ii * nk, nk), :] = gate * act
        if next_ref is not None:
            next_ref[...] = jnp.dot(u_ref[...], ht_ref[...], preferred_element_type=F32)
        if prev_ref is not None:
            o_ref[...] += jnp.dot(vt_ref[...], ga_ref[...], preferred_element_type=F32)
        else:
            o_ref[...] = jnp.zeros_like(o_ref)

    pl.when(j == 0)(functools.partial(step, None, act0_ref))
    pl.when(j % 2 == 1)(functools.partial(step, act0_ref, act1_ref))
    pl.when((j % 2 == 0) & (j > 0) & (j < last))(functools.partial(step, act1_ref, act0_ref))
    pl.when(j == last)(functools.partial(step, act1_ref, None))


PEER_EB = 1024


def peer_dense(u_all, ht, vt_all, layer, e1, n1, r2, e2, tt=512):
    d, s = ht.shape
    ne, eb = u_all.shape[1], PEER_EB
    assert (ne // eb) % 2 == 0
    last = ne // eb - 1
    gspec = pl.BlockSpec((PEER_HEADS, PEER_NKEYS, tt), lambda i, j: (0, 0, i))
    return pl.pallas_call(
        _peer_dense_kernel,
        grid=(s // tt, ne // eb + 1),
        in_specs=[
            pl.BlockSpec((None, eb, d), lambda i, j: (layer, jnp.minimum(j, last), 0)),
            pl.BlockSpec((d, tt), lambda i, j: (0, i)),
            pl.BlockSpec((None, None, d, eb), lambda i, j: (layer, jnp.maximum(j - 1, 0), 0, 0)),
            gspec, gspec, gspec, gspec,
        ],
        out_specs=pl.BlockSpec((d, tt), lambda i, j: (0, i)),
        out_shape=jax.ShapeDtypeStruct((d, s), F32),
        scratch_shapes=[pltpu.VMEM((eb, tt), F32), pltpu.VMEM((eb, tt), F32), pltpu.VMEM((eb, tt), BF16)],
        compiler_params=_params("parallel", "arbitrary"),
        name="peer_dense",
    )(u_all, ht, vt_all, e1, n1, r2, e2)


def prep_peer_tables(peer_u, peer_v):
    nl, ne, d = peer_v.shape
    vt_all = peer_v.astype(BF16).reshape(nl, ne // PEER_EB, PEER_EB, d).transpose(0, 1, 3, 2)
    return peer_u.astype(BF16), vt_all


def peer(x, ln_g, w_q, keys, u_all, vt_all, layer, tm=512, tt=256):
    wf = peer_fold(w_q, keys).reshape(2, PEER_HEADS * PEER_NKEYS, D_MODEL).astype(BF16)
    ht = rmsnorm_t(x, ln_g, tm=tm)
    return peer_dense(u_all, ht, vt_all, layer, *peer_select(wf, ht, tt=tt), tt=tm)


MLA_HEAD_GROUP = 4


def _mla_kernel(qi_ref, ki_ref, qn_ref, qr_ref, qrot_ref, cq_ref, sq_ref, kn_ref, kr_ref, krot_ref, ck_ref, sk_ref,
                vt_ref, o_ref, q1_ref, q2_ref, m_ref, l_ref, acc_ref, *, scale):
    t = pl.program_id(1)
    qi, ki = qi_ref[t], ki_ref[t]
    tq, tk = qn_ref.shape[1], kn_ref.shape[0]
    ratio = tq // tk
    dn, dr, dv = MLA_NOPE, MLA_ROPE, MLA_V
    heads = range(qn_ref.shape[0] // dn)

    @pl.when(ki == 0)
    def _():
        q1_ref[...] = (qn_ref[...] * scale).astype(BF16)
        for h in heads:
            rows = pl.ds(h * dr, dr)
            q2_ref[rows, :] = ((qr_ref[rows, :] * cq_ref[...] + qrot_ref[rows, :] * sq_ref[...]) * scale).astype(BF16)
        m_ref[...] = jnp.full_like(m_ref, -jnp.inf)
        l_ref[...] = jnp.zeros_like(l_ref)
        acc_ref[...] = jnp.zeros_like(acc_ref)

    def step(masked):
        kr = (kr_ref[...] * ck_ref[...] + krot_ref[...] * sk_ref[...]).astype(BF16)
        s = [jnp.dot(kn_ref[:, h * dn:(h + 1) * dn], q1_ref[h * dn:(h + 1) * dn, :], preferred_element_type=F32)
             + jnp.dot(kr, q2_ref[h * dr:(h + 1) * dr, :], preferred_element_type=F32) for h in heads]
        if masked:
            kpos = ki * tk + lax.broadcasted_iota(jnp.int32, (tk, tq), 0)
            qpos = qi * tq + lax.broadcasted_iota(jnp.int32, (tk, tq), 1)
            keep = kpos <= qpos
            s = [jnp.where(keep, x, -jnp.inf) for x in s]
        m_old = [m_ref[h] for h in heads]
        m_new = [jnp.maximum(a, jnp.max(x, axis=0, keepdims=True)) for a, x in zip(m_old, s)]
        alpha = [jnp.exp(a - b) for a, b in zip(m_old, m_new)]
        p = [jnp.exp(x - b) for x, b in zip(s, m_new)]
        for h in heads:
            rows = pl.ds(h * dv, dv)
            l_ref[h] = alpha[h] * l_ref[h] + jnp.sum(p[h], axis=0, keepdims=True)
            acc_ref[rows, :] = alpha[h] * acc_ref[rows, :] + jnp.dot(vt_ref[rows, :], p[h].astype(BF16),
                                                                       preferred_element_type=F32)
            m_ref[h] = m_new[h]

    pl.when(ki < qi * ratio)(functools.partial(step, False))
    pl.when(ki >= qi * ratio)(functools.partial(step, True))

    @pl.when(ki == (qi + 1) * ratio - 1)
    def _():
        for h in heads:
            rows = pl.ds(h * dv, dv)
            o_ref[rows, :] = acc_ref[rows, :] / l_ref[h]


def mla_attention(qt, kv, vt, kr, krot, cos, sin, cos_t, sin_t, tq=1024, tk=512):
    s = kv.shape[0]
    tq, tk = min(tq, s), min(tk, s)
    ratio = tq // tk
    hh, dn, dr = MLA_HEADS, MLA_NOPE, MLA_ROPE
    pairs = [(qi, ki) for qi in range(s // tq) for ki in range((qi + 1) * ratio)]
    qi_tab = jnp.array([pr[0] for pr in pairs], jnp.int32)
    ki_tab = jnp.array([pr[1] for pr in pairs], jnp.int32)
    g = MLA_HEAD_GROUP
    r0 = hh * dn // (g * dr)
    qmap = lambda h, t, qt_, kt_: (0, qt_[t])
    kmap = lambda h, t, qt_, kt_: (kt_[t], 0)
    grid_spec = pltpu.PrefetchScalarGridSpec(
        num_scalar_prefetch=2,
        grid=(hh // g, len(pairs)),
        in_specs=[
            pl.BlockSpec((g * dn, tq), lambda h, t, qt_, kt_: (h, qt_[t])),
            pl.BlockSpec((g * dr, tq), lambda h, t, qt_, kt_: (r0 + h, qt_[t])),
            pl.BlockSpec((g * dr, tq), lambda h, t, qt_, kt_: (r0 + hh // g + h, qt_[t])),
            pl.BlockSpec((dr, tq), qmap),
            pl.BlockSpec((dr, tq), qmap),
            pl.BlockSpec((tk, g * dn), lambda h, t, qt_, kt_: (kt_[t], h)),
            pl.BlockSpec((tk, dr), kmap),
            pl.BlockSpec((tk, dr), kmap),
            pl.BlockSpec((tk, dr), kmap),
            pl.BlockSpec((tk, dr), kmap),
            pl.BlockSpec((g * MLA_V, tk), lambda h, t, qt_, kt_: (h, kt_[t])),
        ],
        out_specs=pl.BlockSpec((g * MLA_V, tq), lambda h, t, qt_, kt_: (h, qt_[t])),
        scratch_shapes=[pltpu.VMEM((g * dn, tq), BF16), pltpu.VMEM((g * dr, tq), BF16), pltpu.VMEM((g, 1, tq), F32),
                        pltpu.VMEM((g, 1, tq), F32), pltpu.VMEM((g * MLA_V, tq), F32)],
    )
    return pl.pallas_call(
        functools.partial(_mla_kernel, scale=(MLA_NOPE + MLA_ROPE) ** -0.5),
        grid_spec=grid_spec,
        out_shape=jax.ShapeDtypeStruct((hh * MLA_V, s), F32),
        compiler_params=_params("parallel", "arbitrary"),
        name="mla_attention",
    )(qi_tab, ki_tab, qt, qt, qt, cos_t, sin_t, kv, kr, krot, cos, sin, vt)


def _causal_conv(cur_ref, prev_ref, w_ref, first, bias=None):
    prev = jnp.where(first, 0.0, prev_ref[...])
    xe = jnp.concatenate([prev, cur_ref[...]], axis=0)
    w = w_ref[...]
    acc = w[CONV_WIDTH - 1:CONV_WIDTH] * xe[SUBLANES:]
    for j in range(CONV_WIDTH - 1):
        acc = acc + w[j:j + 1] * pltpu.roll(xe, CONV_WIDTH - 1 - j, 0)[SUBLANES:]
    return acc if bias is None else acc + bias


def _unit_lower_inverses(lows, c):
    n = lows[0].shape[0]
    r = lax.broadcasted_iota(jnp.int32, (n, n), 0)
    q = lax.broadcasted_iota(jnp.int32, (n, n), 1)
    eye = (r == q).astype(F32)
    prev = [jnp.where((r // 16) == (q // 16), low, 0.0) for low in lows]
    ps = [-d for d in prev]
    xs = [eye + p for p in ps]
    for _ in range(3):
        ps = [_bdot(p, p) for p in ps]
        xs = [x + _bdot(x, p) for x, p in zip(xs, ps)]
    size = 32
    while size <= c:
        cur = [jnp.where((r // size) == (q // size), low, 0.0) for low in lows] if size < c else lows
        ts = [_bdot(x, cu - pr) for x, cu, pr in zip(xs, cur, prev)]
        xs = [x - _bdot(t, x) for x, t in zip(xs, ts)]
        prev = cur
        size *= 2
    return xs


GDN_HEAD_GROUP = 4


def _gdn_kernel(q_ref, qp_ref, k_ref, kp_ref, v_ref, vp_ref, z_ref, wq_ref, wk_ref, wv_ref,
                ar_ref, br_ref, alog_ref, dtb_ref, ng_ref, o_ref, st_ref):
    c = GDN_CHUNK
    tt = q_ref.shape[0]
    heads = range(q_ref.shape[1] // GDN_DK)
    first = pl.program_id(1) == 0

    @pl.when(first)
    def _():
        st_ref[...] = jnp.zeros_like(st_ref)

    def l2n(x):
        return x * lax.rsqrt(jnp.sum(x * x, axis=-1, keepdims=True) + 1e-6)

    lanes = lambda x, h: x[:, h * GDN_DK:(h + 1) * GDN_DK]
    q_all = _silu(_causal_conv(q_ref, qp_ref, wq_ref, first))
    k_all = _silu(_causal_conv(k_ref, kp_ref, wk_ref, first))
    v_all = _silu(_causal_conv(v_ref, vp_ref, wv_ref, first))
    q = [l2n(lanes(q_all, h)) * (GDN_DK ** -0.5) for h in heads]
    k = [l2n(lanes(k_all, h)) for h in heads]
    v = [lanes(v_all, h) for h in heads]
    neg_a = [-jnp.exp(alog_ref[h, :, 0:1]) for h in heads]
    dtb = [dtb_ref[h, :, 0:1] for h in heads]
    nb = 2 * c
    r = lax.broadcasted_iota(jnp.int32, (nb, nb), 0)
    cc = lax.broadcasted_iota(jnp.int32, (nb, nb), 1)
    same = (r // c) == (cc // c)
    incl, strict = same & (r >= cc), same & (r > cc)
    incl_t = same & (r <= cc)
    top = lax.broadcasted_iota(jnp.int32, (nb, 1), 0) < c
    nblocks = tt // nb
    items = [(n, h) for n in range(nblocks) for h in heads]
    sl = lambda n: slice(n * nb, (n + 1) * nb)
    qs = [q[h][sl(n)] for n, h in items]
    ks = [k[h][sl(n)] for n, h in items]
    vs = [v[h][sl(n)] for n, h in items]
    as_col = lambda row: jnp.sum(jnp.where(r == cc, row, 0.0), axis=1, keepdims=True)
    g_row = [neg_a[h] * _softplus(ar_ref[h, n] + dtb[h]) for n, h in items]
    g_col = [as_col(g) for g in g_row]
    beta = [as_col(_sigmoid(br_ref[h, n])) for n, h in items]
    gc = [jnp.sum(jnp.where(incl, g, 0.0), axis=1, keepdims=True) for g in g_row]
    gr = [jnp.sum(jnp.where(incl_t, g, 0.0), axis=0, keepdims=True) for g in g_col]
    decay = [jnp.where(incl, jnp.exp(jnp.where(incl, a - b, 0.0)), 0.0) for a, b in zip(gc, gr)]
    kk = [_bdot_nt(x, x) for x in ks]
    qk = [_bdot_nt(a, b) for a, b in zip(qs, ks)]
    inv = _unit_lower_inverses([jnp.where(strict, b * m * d, 0.0) for b, m, d in zip(beta, kk, decay)], c)
    eg = [jnp.exp(g) for g in gc]
    sol = [_bdot(x, jnp.concatenate([b * vv, (b * e) * kx], axis=1)) for x, b, e, vv, kx in zip(inv, beta, eg, vs, ks)]
    a_qk = [m * d for m, d in zip(qk, decay)]
    k_end = [kx * jnp.exp(jnp.where(top, g[c - 1:c], g[nb - 1:nb]) - g) for g, kx in zip(gc, ks)]
    k_end_t = [[ke[:c].T, ke[c:].T] for ke in k_end]
    q_dec = [a * e for a, e in zip(qs, eg)]
    state = [st_ref[h] for h in heads]
    us = [[None, None] for _ in items]
    o_state = [[None, None] for _ in items]
    for n in range(nblocks):
        for j in range(2):
            cs = slice(j * c, (j + 1) * c)
            for h in heads:
                it = n * len(heads) + h
                u = sol[it][cs, :GDN_DV] - _bdot(sol[it][cs, GDN_DV:], state[h])
                o_state[it][j] = _bdot(q_dec[it][cs], state[h])
                state[h] = jnp.exp(gc[it][(j + 1) * c - 1:(j + 1) * c]) * state[h] + _bdot(k_end_t[it][j], u)
                us[it][j] = u
    for h in heads:
        st_ref[h] = state[h]
    o_blk = [jnp.concatenate(o_state[it], axis=0) + _bdot(a_qk[it], jnp.concatenate(us[it], axis=0))
             for it in range(len(items))]
    o = jnp.concatenate([_rms(jnp.concatenate([o_blk[n * len(heads) + h] for n in range(nblocks)], axis=0), ng_ref[...])
                         for h in heads], axis=1)
    o_ref[...] = o * _silu(z_ref[...])


def gated_delta_net(proj, col0, a_raw, b_raw, conv_w, a_log, dt_bias, norm_g, tt=256):
    s = proj.shape[0]
    tt = min(tt, s)
    hh, c, g = GDN_HEADS, GDN_CHUNK, GDN_HEAD_GROUP
    w = g * GDN_DK
    assert col0 % w == 0
    b0 = col0 // w
    nblk = GDN_HEADS * GDN_DK // w

    def cur(grp):
        return pl.BlockSpec((tt, w), lambda h, i: (i, b0 + grp * nblk + h))

    def prev(grp):
        return pl.BlockSpec((SUBLANES, w), lambda h, i: (jnp.maximum(i * (tt // SUBLANES) - 1, 0), b0 + grp * nblk + h))

    def wspec(grp):
        return pl.BlockSpec((CONV_WIDTH, w), lambda h, i: (0, grp * nblk + h))

    a_t, b_t = a_raw.T, b_raw.T
    c = 2 * c
    rowspec = pl.BlockSpec((g, tt // c, 1, c), lambda h, i: (h, i, 0, 0))
    hspec = pl.BlockSpec((g, 1, LANES), lambda h, i: (h, 0, 0))
    bcast = lambda p: jnp.broadcast_to(p.astype(F32)[:, None, None], (hh, 1, LANES))
    return pl.pallas_call(
        _gdn_kernel,
        grid=(hh // g, s // tt),
        in_specs=[cur(0), prev(0), cur(1), prev(1), cur(2), prev(2), cur(3), wspec(0), wspec(1), wspec(2),
                  rowspec, rowspec, hspec, hspec, pl.BlockSpec((1, GDN_DV), lambda h, i: (0, 0))],
        out_specs=pl.BlockSpec((tt, w), lambda h, i: (i, h)),
        out_shape=jax.ShapeDtypeStruct((s, hh * GDN_DV), F32),
        scratch_shapes=[pltpu.VMEM((g, GDN_DK, GDN_DV), F32)],
        compiler_params=_params("parallel", "arbitrary"),
        name="gated_delta_net",
    )(proj, proj, proj, proj, proj, proj, proj, conv_w, conv_w, conv_w,
      a_t.reshape(hh, s // c, 1, c), b_t.reshape(hh, s // c, 1, c),
      bcast(a_log), bcast(dt_bias), norm_g.reshape(1, GDN_DV).astype(F32))


def _rot_half_cols(w, half):
    return jnp.concatenate([-w[..., half:], w[..., :half]], axis=-1)


def _pad_cols(w, n):
    return jnp.pad(w, ((0, 0), (0, n - w.shape[1])))


AB_QKVZ = 0
AB_CQ = 4 * GDN_HEADS * GDN_DK
AB_CKV = AB_CQ + MLA_Q_RANK
AB_KR = AB_CKV + MLA_KV_RANK
AB_GATES = AB_KR + 2 * MLA_ROPE
AB_PAD = AB_GATES + LANES


def prep_ab(w_in, w_uq, w_ukv, w_out):
    rq, rkv, rr = MLA_Q_RANK, MLA_KV_RANK, MLA_ROPE
    w_kr = w_in[:, rq + rkv:rq + rkv + rr]
    gdn_w = 4 * GDN_HEADS * GDN_DK
    g0 = rq + rkv + rr
    w_main = jnp.concatenate([w_in[:, g0:g0 + gdn_w], w_in[:, :rq + rkv], w_kr, _rot_half_cols(w_kr, rr // 2),
                              w_in[:, g0 + gdn_w:]], axis=1)
    w_main = _pad_cols(w_main, AB_PAD).astype(BF16)
    uq = w_uq.reshape(rq, MLA_HEADS, MLA_NOPE + MLA_ROPE)
    uq_r = uq[..., MLA_NOPE:]
    uq2 = jnp.concatenate([uq[..., :MLA_NOPE].reshape(rq, -1), uq_r.reshape(rq, -1),
                           _rot_half_cols(uq_r, rr // 2).reshape(rq, -1)], axis=1).astype(BF16)
    ukv = w_ukv.reshape(rkv, MLA_HEADS, MLA_NOPE + MLA_V)
    ukv2 = jnp.concatenate([ukv[..., :MLA_NOPE].reshape(rkv, -1), ukv[..., MLA_NOPE:].reshape(rkv, -1)], axis=1).astype(BF16)
    return w_main, uq2, ukv2, w_out.astype(BF16)


def mixer_ab(x, ln, w_main, uq2, ukv2, w_out, q_norm, kv_norm, conv_w, a_log, dt_bias, gdn_norm, cos2, sin2,
             tm=512, t_attn=512, t_gdn=256):
    s = x.shape[0]
    rq, rkv, rr = MLA_Q_RANK, MLA_KV_RANK, MLA_ROPE
    proj = norm_matmul(x, w_main, gain=ln, tm=min(2 * tm, x.shape[0]))
    qfull = norm_matmul(proj, uq2, gain=q_norm, tm=tm, x_col=AB_CQ)
    kv = norm_matmul(proj, ukv2, gain=kv_norm, tm=tm, out_dtype=BF16, x_col=AB_CKV)
    nn = MLA_HEADS * MLA_NOPE
    o_a = mla_attention(qfull.T, kv, kv[:, nn:].T, proj[:, AB_KR:AB_KR + rr], proj[:, AB_KR + rr:AB_GATES],
                        cos2, sin2, cos2.T, sin2.T, tq=2 * t_attn, tk=t_attn).T
    o_b = gated_delta_net(proj, AB_QKVZ, proj[:, AB_GATES:AB_GATES + GDN_HEADS],
                          proj[:, AB_GATES + GDN_HEADS:AB_GATES + 2 * GDN_HEADS], conv_w, a_log, dt_bias, gdn_norm, tt=t_gdn)
    return out_proj(o_a, o_b, w_out, x, tm=min(2 * tm, x.shape[0]))


CD_R, CD_K, CD_V, CD_Z, CD_DT, CD_X = (i * 1024 for i in range(6))
CD_B = 6144
CD_C = CD_B + SSD_GROUPS * SSD_STATE
CD_WA = CD_C + SSD_GROUPS * SSD_STATE
CD_G = CD_WA + LANES
CD_PAD = 7168
GROUP_W = SSD_INNER // SSD_GROUPS


def _lower_ones(c):
    r = lax.broadcasted_iota(jnp.int32, (c, c), 0)
    q = lax.broadcasted_iota(jnp.int32, (c, c), 1)
    return r >= q, r > q


def _ssd_kernel(x_ref, xp_ref, b_ref, bp_ref, c_ref, cp_ref, z_ref, dt_ref, wx_ref, wb_ref, wc_ref,
                bx_ref, bb_ref, bc_ref, dtb_ref, alog_ref, dskip_ref, ng_ref, o_ref, st_ref):
    c = SSD_CHUNK
    tt = x_ref.shape[0]
    first = pl.program_id(1) == 0

    @pl.when(first)
    def _():
        st_ref[...] = jnp.zeros_like(st_ref)

    xs_all = _silu(_causal_conv(x_ref, xp_ref, wx_ref, first, bx_ref[...]))
    bm_all = _silu(_causal_conv(b_ref, bp_ref, wb_ref, first, bb_ref[...]))
    cm_all = _silu(_causal_conv(c_ref, cp_ref, wc_ref, first, bc_ref[...]))
    dt_all = _softplus(dt_ref[...] + dtb_ref[...])
    a_all = -jnp.exp(alog_ref[...]) * dt_all
    incl, _ = _lower_ones(c)
    tri = incl.astype(F32)
    left = lax.broadcasted_iota(jnp.int32, (c, LANES), 1) < SSD_HEADDIM
    npair = GROUP_W // LANES
    outs = []
    for n in range(tt // c):
        sl = slice(n * c, (n + 1) * c)
        xs, bm, cm, dt = xs_all[sl], bm_all[sl], cm_all[sl], dt_all[sl]
        acs = _bdot(tri, a_all[sl])
        xdt = xs * dt
        cb = _bdot_nt(cm, bm)
        bm_t = bm.T
        ys = []
        for p in range(npair):
            ls = slice(p * LANES, (p + 1) * LANES)
            acs_p = acs[:, ls]
            acs_t = acs_p.T
            xp = xdt[:, ls]
            yd = []
            for hd in range(2):
                col = acs_p[:, hd * SSD_HEADDIM:hd * SSD_HEADDIM + 1]
                row = acs_t[hd * SSD_HEADDIM:hd * SSD_HEADDIM + 1, :]
                lmat = jnp.where(incl, jnp.exp(jnp.where(incl, col - row, 0.0)), 0.0)
                yd.append(_bdot(cb * lmat, xp))
            last = acs_p[c - 1:c]
            prev_t = st_ref[p]
            y_off = _bdot(cm, prev_t) * jnp.exp(acs_p)
            st_ref[p] = jnp.exp(last) * prev_t + _bdot(bm_t, xp * jnp.exp(last - acs_p))
            ys.append(jnp.where(left, yd[0], yd[1]) + y_off)
        outs.append(jnp.concatenate(ys, axis=1) + xs * dskip_ref[...])
    y = jnp.concatenate(outs, axis=0) * _silu(z_ref[...])
    o_ref[...] = _rms(y, ng_ref[...])


def mamba2_ssd(proj, conv_w, conv_b, dt_bias, a_log, d_skip, norm_g, tt=256):
    s = proj.shape[0]
    tt = min(tt, s)
    gw, ns = GROUP_W, SSD_STATE
    per = lambda v: jnp.repeat(v.astype(F32), SSD_HEADDIM).reshape(1, SSD_INNER)

    def cur(col, w):
        return pl.BlockSpec((tt, w), lambda g, i: (i, col // w + g))

    def prev(col, w):
        return pl.BlockSpec((SUBLANES, w), lambda g, i: (jnp.maximum(i * (tt // SUBLANES) - 1, 0), col // w + g))

    def par(rows, col, w):
        return pl.BlockSpec((rows, w), lambda g, i: (0, col // w + g))

    cb = conv_b.reshape(1, -1).astype(F32)
    return pl.pallas_call(
        _ssd_kernel,
        grid=(SSD_GROUPS, s // tt),
        in_specs=[cur(CD_X, gw), prev(CD_X, gw), cur(CD_B, ns), prev(CD_B, ns), cur(CD_C, ns), prev(CD_C, ns),
                  cur(CD_Z, gw), cur(CD_DT, gw),
                  par(CONV_WIDTH, 0, gw), par(CONV_WIDTH, SSD_INNER, ns), par(CONV_WIDTH, SSD_INNER + SSD_GROUPS * ns, ns),
                  par(1, 0, gw), par(1, SSD_INNER, ns), par(1, SSD_INNER + SSD_GROUPS * ns, ns),
                  par(1, 0, gw), par(1, 0, gw), par(1, 0, gw), par(1, 0, gw)],
        out_specs=pl.BlockSpec((tt, gw), lambda g, i: (i, g)),
        out_shape=jax.ShapeDtypeStruct((s, SSD_INNER), F32),
        scratch_shapes=[pltpu.VMEM((gw // LANES, ns, LANES), F32)],
        compiler_params=_params("parallel", "arbitrary"),
        name="mamba2_ssd",
    )(proj, proj, proj, proj, proj, proj, proj, proj, conv_w, conv_w, conv_w, cb, cb, cb,
      per(dt_bias), per(a_log), per(d_skip), norm_g.reshape(1, SSD_INNER).astype(F32))


def _pair_ones():
    r = lax.broadcasted_iota(jnp.int32, (LANES, LANES), 0)
    q = lax.broadcasted_iota(jnp.int32, (LANES, LANES), 1)
    return (r // RWKV_HEAD) == (q // RWKV_HEAD)


def _head_sums(x, ones):
    return jnp.concatenate([_hdot(x[:, i * LANES:(i + 1) * LANES], ones) for i in range(x.shape[1] // LANES)], axis=1)


def _rwkv_prep_kernel(r_ref, rp_ref, k_ref, kp_ref, v_ref, vp_ref, wa_ref, wap_ref, g0_ref, g0p_ref, g1_ref, g1p_ref,
                      mur_ref, muk_ref, muv_ref, muwa_ref, mug0_ref, mug1_ref, w0_ref, w2_ref, a0_ref, a2_ref, g2_ref,
                      kk_ref, ka_ref, rk_ref,
                      ro_ref, lw_ref, ko_ref, vo_ref, po_ref, qo_ref, go_ref, bo_ref):
    first = pl.program_id(0) == 0

    def mix(cur_ref, prev_ref, mu_ref):
        cur = cur_ref[...]
        prev = jnp.where(first, 0.0, prev_ref[...])
        shifted = pltpu.roll(jnp.concatenate([prev, cur], axis=0), 1, 0)[SUBLANES:]
        return cur + (shifted - cur) * mu_ref[...]

    r = mix(r_ref, rp_ref, mur_ref)
    k = mix(k_ref, kp_ref, muk_ref)
    v = mix(v_ref, vp_ref, muv_ref)
    wa = mix(wa_ref, wap_ref, muwa_ref)
    g0 = mix(g0_ref, g0p_ref, mug0_ref)
    g1 = mix(g1_ref, g1p_ref, mug1_ref)
    log_w = -math.exp(-0.5) * _sigmoid(w0_ref[...] + _bdot(jnp.tanh(wa), w2_ref[...]))
    a = _sigmoid(a0_ref[...] + _bdot(wa, a2_ref[...]))
    gate = _bdot(_sigmoid(g0), g2_ref[0:LANES, :]) + _bdot(_sigmoid(g1), g2_ref[LANES:2 * LANES, :])
    ones = _pair_ones().astype(F32)
    kx = k * kk_ref[...]
    kk = kx * lax.rsqrt(_head_sums(kx * kx, ones) + 1e-6)
    k_mod = k * (1.0 + (a - 1.0) * ka_ref[...])
    ro_ref[...] = r
    lw_ref[...] = log_w
    ko_ref[...] = k_mod
    vo_ref[...] = v
    po_ref[...] = -kk * a
    qo_ref[...] = kk
    go_ref[...] = gate
    bo_ref[...] = _head_sums(r * k_mod * rk_ref[...], ones) * v


def rwkv_prep(proj, mu, w0, w2, a0, a2, g2, k_k, k_a, r_k, tt=256):
    s = proj.shape[0]
    tt = min(tt, s)
    ri = RWKV_INNER
    row = lambda v: v.reshape(1, -1).astype(F32)
    mu_r, mu_k, mu_v = (row(mu[i * ri:(i + 1) * ri]) for i in range(3))
    mu_wa = row(mu[3 * ri:3 * ri + LANES])
    mu_g = row(jnp.pad(mu[3 * ri + LANES:], (0, 2 * LANES - RWKV_G_LORA)))
    zeros = jnp.zeros((RWKV_W_LORA, ri), F32)
    w2p = jnp.concatenate([w2, zeros], axis=0).astype(BF16)
    a2p = jnp.concatenate([zeros, a2], axis=0).astype(BF16)
    g2p = jnp.pad(g2, ((0, 2 * LANES - RWKV_G_LORA), (0, 0))).astype(BF16)

    def cur(col, w):
        return pl.BlockSpec((tt, w), lambda i: (i, col // w))

    def prev(col, w):
        return pl.BlockSpec((SUBLANES, w), lambda i: (jnp.maximum(i * (tt // SUBLANES) - 1, 0), col // w))

    full = lambda a: pl.BlockSpec(a.shape, lambda i: (0, 0))
    params = [mu_r, mu_k, mu_v, mu_wa, mu_g[:, :LANES], mu_g[:, LANES:], row(w0), w2p, row(a0), a2p, g2p,
              row(k_k), row(k_a), row(r_k)]
    out = jax.ShapeDtypeStruct((s, ri), F32)
    return pl.pallas_call(
        _rwkv_prep_kernel,
        grid=(s // tt,),
        in_specs=[cur(CD_R, ri), prev(CD_R, ri), cur(CD_K, ri), prev(CD_K, ri), cur(CD_V, ri), prev(CD_V, ri),
                  cur(CD_WA, LANES), prev(CD_WA, LANES), cur(CD_G, LANES), prev(CD_G, LANES),
                  cur(CD_G + LANES, LANES), prev(CD_G + LANES, LANES)] + [full(a) for a in params],
        out_specs=[pl.BlockSpec((tt, ri), lambda i: (i, 0))] * 8,
        out_shape=[out] * 8,
        compiler_params=_params("arbitrary"),
        name="rwkv_prep",
    )(*([proj] * 12), *params)


RWKV_MY_CHUNK = 64


def _rwkv_scan_kernel(r_ref, lw_ref, k_ref, v_ref, p_ref, q_ref, g_ref, b_ref, lnw_ref, lnb_ref, o_ref, st_ref):
    c = RWKV_MY_CHUNK
    tt = r_ref.shape[0]

    @pl.when(pl.program_id(1) == 0)
    def _():
        st_ref[...] = jnp.zeros_like(st_ref)

    tri = _lower_ones(c)[0].astype(F32)
    left = lax.broadcasted_iota(jnp.int32, (c, LANES), 1) < RWKV_HEAD
    pair = _pair_ones()
    ones = pair.astype(F32)
    r_id = lax.broadcasted_iota(jnp.int32, (LANES, LANES), 0)
    c_id = lax.broadcasted_iota(jnp.int32, (LANES, LANES), 1)
    eye = r_id == c_id
    top = r_id < c
    strict = pair & ((r_id % c) > (c_id % c))
    incl = pair & ((r_id % c) >= (c_id % c))
    stack2 = lambda a: jnp.concatenate([a, a], axis=0)
    by_head = lambda a: jnp.concatenate([jnp.where(left, a, 0.0), jnp.where(left, 0.0, a)], axis=0)
    unstack = lambda a: jnp.where(left, a[:c], a[c:])
    pairs = range(r_ref.shape[1] // LANES)
    nchunks = tt // c
    items = [(n, pr) for n in range(nchunks) for pr in pairs]
    blk = lambda ref, n, pr: ref[n * c:(n + 1) * c, pr * LANES:(pr + 1) * LANES]
    w = [blk(lw_ref, n, pr) for n, pr in items]
    v = [blk(v_ref, n, pr) for n, pr in items]
    lw = [_bdot(tri, x) for x in w]
    lam_in = [jnp.exp(x) for x in lw]
    inv_lam = [jnp.exp(-x) for x in lw]
    q_bar = [blk(q_ref, n, pr) * jnp.exp(a - b) for (n, pr), a, b in zip(items, lw, w)]
    r_bar = [blk(r_ref, n, pr) * x for (n, pr), x in zip(items, lam_in)]
    pk = [jnp.concatenate([blk(p_ref, n, pr) * x, blk(k_ref, n, pr) * x], axis=0) for (n, pr), x in zip(items, inv_lam)]
    mq = [_bdot_nt(by_head(a), b) for a, b in zip(q_bar, pk)]
    mr = [_bdot_nt(by_head(a), b) for a, b in zip(r_bar, pk)]
    mq_sw = [pltpu.roll(x, c, 1) for x in mq]
    mr_sw = [pltpu.roll(x, c, 1) for x in mr]
    m_qp = [jnp.where(strict, jnp.where(top, a, b), 0.0) for a, b in zip(mq, mq_sw)]
    m_qk = [jnp.where(strict, jnp.where(top, b, a), 0.0) for a, b in zip(mq, mq_sw)]
    m_rp = [jnp.where(incl, jnp.where(top, a, b), 0.0) for a, b in zip(mr, mr_sw)]
    m_rk = [jnp.where(incl, jnp.where(top, b, a), 0.0) for a, b in zip(mr, mr_sw)]
    vv = [stack2(x) for x in v]
    inv = _unit_lower_inverses([-x for x in m_qp], c)
    qkv = [_bdot(a, b) for a, b in zip(m_qk, vv)]
    sol = [_bdot(x, jnp.concatenate([stack2(a), b], axis=1)) for x, a, b in zip(inv, q_bar, qkv)]
    ws = [unstack(x[:, :LANES]) for x in sol]
    wv = [unstack(x[:, LANES:]) for x in sol]
    y_loc = [unstack(_bdot(a, b)) for a, b in zip(m_rk, vv)]
    lam_end = [x[c - 1:c] for x in lam_in]
    lam_col = [jnp.sum(jnp.where(eye, x, 0.0), axis=1, keepdims=True) for x in lam_end]
    pk_end_t = [(a * b).T for a, b in zip(pk, lam_end)]
    state = [st_ref[pr] for pr in pairs]
    us, y_state = [], []
    for it, (n, pr) in enumerate(items):
        u = _bdot(ws[it], state[pr]) + wv[it]
        y_state.append(_bdot(r_bar[it], state[pr]))
        upd = _bdot(pk_end_t[it], jnp.concatenate([u, v[it]], axis=0))
        state[pr] = lam_col[it] * state[pr] + jnp.where(pair, upd, 0.0)
        us.append(u)
    for pr in pairs:
        st_ref[pr] = state[pr]
    outs = [a + unstack(_bdot(b, stack2(u))) + d for a, b, u, d in zip(y_state, m_rp, us, y_loc)]
    ys = []
    for pr in pairs:
        y = jnp.concatenate([outs[n * len(pairs) + pr] for n in range(nchunks)], axis=0)
        mean = _hdot(y, ones) * (1.0 / RWKV_HEAD)
        yc = y - mean
        var = _hdot(yc * yc, ones) * (1.0 / RWKV_HEAD)
        ys.append(yc * lax.rsqrt(var + RWKV_GN_EPS))
    y = jnp.concatenate(ys, axis=1) * lnw_ref[...] + lnb_ref[...]
    o_ref[...] = (y + b_ref[...]) * g_ref[...]


RWKV_PAIR_GROUP = 4


def rwkv_scan(r, lw, k, v, p, q, gate, bonus, ln_w, ln_b, tt=256):
    s = r.shape[0]
    tt = min(tt, s)
    w = RWKV_PAIR_GROUP * LANES
    spec = pl.BlockSpec((tt, w), lambda h, i: (i, h))
    pspec = pl.BlockSpec((1, w), lambda h, i: (0, h))
    return pl.pallas_call(
        _rwkv_scan_kernel,
        grid=(RWKV_INNER // w, s // tt),
        in_specs=[spec] * 8 + [pspec] * 2,
        out_specs=spec,
        out_shape=jax.ShapeDtypeStruct((s, RWKV_INNER), F32),
        scratch_shapes=[pltpu.VMEM((RWKV_PAIR_GROUP, LANES, LANES), F32)],
        compiler_params=_params("parallel", "arbitrary"),
        name="rwkv_scan",
    )(r, lw, k, v, p, q, gate, bonus, ln_w.reshape(1, -1).astype(F32), ln_b.reshape(1, -1).astype(F32))


def prep_cd(w_in, w_out):
    si, ri = SSD_INNER, RWKV_INNER
    z, xbc, dt, rw = w_in[:, :si], w_in[:, si:2 * si + 512], w_in[:, 2 * si + 512:2 * si + 528], w_in[:, 2 * si + 528:]
    dt_exp = jnp.repeat(dt, SSD_HEADDIM, axis=1)
    cols = [rw[:, :3 * ri], z, dt_exp, xbc, rw[:, 3 * ri:]]
    return _pad_cols(jnp.concatenate(cols, axis=1), CD_PAD).astype(BF16), w_out.astype(BF16)


def mixer_cd(x, ln, w_main, w_out, ssd_conv_w, ssd_conv_b, ssd_dt_bias, ssd_a_log, ssd_d, ssd_norm,
             mu, w0, w2, a0, a2, g2, k_k, k_a, r_k, ln_w, ln_b, tm=512, tt=256):
    proj = norm_matmul(x, w_main, gain=ln, tm=min(2 * tm, x.shape[0]))
    o_c = mamba2_ssd(proj, ssd_conv_w, ssd_conv_b, ssd_dt_bias, ssd_a_log, ssd_d, ssd_norm, tt=tt)
    o_d = rwkv_scan(*rwkv_prep(proj, mu, w0, w2, a0, a2, g2, k_k, k_a, r_k.reshape(-1), tt=tt), ln_w, ln_b, tt=2 * tt)
    return out_proj(o_c, o_d, w_out, x, tm=min(2 * tm, x.shape[0]))


def _rope_tables(s):
    inv = 1.0 / (ROPE_THETA ** (jnp.arange(0, MLA_ROPE, 2, dtype=F32) / MLA_ROPE))
    ang = jnp.arange(s, dtype=F32)[:, None] * inv[None, :]
    cos, sin = jnp.cos(ang), jnp.sin(ang)
    return jnp.concatenate([cos, cos], axis=1), jnp.concatenate([sin, sin], axis=1)


def kernel(x, p, ln_mix, ln_ffn, ab_w_in, mla_q_norm, mla_w_uq, mla_kv_norm, mla_w_ukv, gdn_conv_w, gdn_a_log, gdn_dt_bias, gdn_norm, ab_w_out, cd_w_in, ssd_conv_w, ssd_conv_b, ssd_dt_bias, ssd_a_log, ssd_d, ssd_norm, rwkv_mu, rwkv_w0, rwkv_w2, rwkv_a0, rwkv_a2, rwkv_g2, rwkv_k_k, rwkv_k_a, rwkv_r_k, rwkv_ln_w, rwkv_ln_b, cd_w_out, peer_w_q, peer_keys, peer_u, peer_v, ple_w_proj, ple_norm, ple_w_gate, final_norm):
    assert x.shape[0] == 1
    s = x.shape[1]
    tm = min(512, s)
    tt = min(256, s)
    cos2, sin2 = _rope_tables(s)
    u_all, vt_all = prep_peer_tables(peer_u, peer_v)
    xs = x[0]
    for i in range(DEPTH):
        j = i // 2
        if i % 2 == 0:
            wts = prep_ab(ab_w_in[j], mla_w_uq[j], mla_w_ukv[j], ab_w_out[j])
            xs = mixer_ab(xs, ln_mix[i], *wts, mla_q_norm[j], mla_kv_norm[j], gdn_conv_w[j], gdn_a_log[j],
                          gdn_dt_bias[j], gdn_norm[j], cos2, sin2, tm=tm, t_attn=tm, t_gdn=tm)
        else:
            wts = prep_cd(cd_w_in[j], cd_w_out[j])
            xs = mixer_cd(xs, ln_mix[i], *wts, ssd_conv_w[j], ssd_conv_b[j], ssd_dt_bias[j], ssd_a_log[j], ssd_d[j],
                          ssd_norm[j], rwkv_mu[j], rwkv_w0[j], rwkv_w2[j], rwkv_a0[j], rwkv_a2[j], rwkv_g2[j],
                          rwkv_k_k[j], rwkv_k_a[j], rwkv_r_k[j], rwkv_ln_w[j], rwkv_ln_b[j], tm=tm, tt=tt)
        yt = peer(xs, ln_ffn[i], peer_w_q[i], peer_keys[i], u_all, vt_all, i, tm=tm, tt=tt)
        xs = ple_update(xs, yt, p[i, 0], ple_norm[i], ple_w_gate[i].astype(BF16), ple_w_proj[i].astype(BF16), tm=tm)
    return rmsnorm(xs, final_norm, tm=tm)[None]
```

```python
import functools
import math

import jax
import jax.numpy as jnp
from jax import lax
from jax.experimental import pallas as pl
from jax.experimental.pallas import tpu as pltpu

F32 = jnp.float32
BF16 = jnp.bfloat16
HIGHEST = lax.Precision.HIGHEST

D_MODEL = 2048
DEPTH = 4
PLE_DIM = 256
RMS_EPS = 1e-6
MLA_HEADS = 8
MLA_Q_RANK = 512
MLA_KV_RANK = 256
MLA_NOPE = 128
MLA_ROPE = 64
MLA_V = 128
ROPE_THETA = 10000.0
GDN_HEADS = 8
GDN_DK = 128
GDN_DV = 128
GDN_CHUNK = 64
SSD_HEADS = 16
SSD_HEADDIM = 64
SSD_GROUPS = 2
SSD_STATE = 128
SSD_CHUNK = 128
SSD_INNER = SSD_HEADS * SSD_HEADDIM
RWKV_HEADS = 16
RWKV_HEAD = 64
RWKV_INNER = RWKV_HEADS * RWKV_HEAD
RWKV_W_LORA = 64
RWKV_A_LORA = 64
RWKV_G_LORA = 160
RWKV_GN_EPS = 64e-5
CONV_WIDTH = 4
PEER_HEADS = 8
PEER_NKEYS = 128
PEER_EXPERTS = PEER_NKEYS * PEER_NKEYS
PEER_QDIM = 256
PEER_TOPK = 16

LANES = 128
SUBLANES = 8
VMEM_LIMIT = 56 * 1024 * 1024


def _params(*sem):
    return pltpu.CompilerParams(dimension_semantics=sem, vmem_limit_bytes=VMEM_LIMIT)


def _bdot(a, b):
    return jnp.dot(a.astype(BF16), b.astype(BF16), preferred_element_type=F32)


def _bdot_nt(a, b):
    return lax.dot_general(a.astype(BF16), b.astype(BF16), (((1,), (1,)), ((), ())), preferred_element_type=F32)


def _hdot(a, b):
    return jnp.dot(a, b, preferred_element_type=F32, precision=HIGHEST)


def _hdot_nt(a, b):
    return lax.dot_general(a, b, (((1,), (1,)), ((), ())), preferred_element_type=F32, precision=HIGHEST)


def _rms(x, gain):
    return x * lax.rsqrt(jnp.mean(x * x, axis=-1, keepdims=True) + RMS_EPS) * gain


def _sigmoid(x):
    return 1.0 / (1.0 + jnp.exp(-x))


def _silu(x):
    return x * _sigmoid(x)


def _softplus(x):
    return jnp.maximum(x, 0.0) + jnp.log(1.0 + jnp.exp(-jnp.abs(x)))


def _nm_kernel(*refs, has_norm, has_res):
    it = iter(refs)
    x_ref = next(it)
    g_ref = next(it) if has_norm else None
    w_ref = next(it)
    r_ref = next(it) if has_res else None
    o_ref = next(it)
    xn_ref = next(it)

    @pl.when(pl.program_id(1) == 0)
    def _():
        x = x_ref[...].astype(F32)
        if has_norm:
            x = _rms(x, g_ref[...])
        xn_ref[...] = x.astype(BF16)

    acc = jnp.dot(xn_ref[...], w_ref[...], preferred_element_type=F32)
    if has_res:
        acc = acc + r_ref[...]
    o_ref[...] = acc.astype(o_ref.dtype)


def norm_matmul(x, w, gain=None, residual=None, tm=512, tn=512, out_dtype=F32, x_col=0):
    m = x.shape[0]
    k, n = w.shape
    tn = min(tn, n)
    assert m % tm == 0 and n % tn == 0 and x_col % k == 0
    in_specs = [pl.BlockSpec((tm, k), lambda i, j: (i, x_col // k))]
    args = [x]
    if gain is not None:
        in_specs.append(pl.BlockSpec((1, k), lambda i, j: (0, 0)))
        args.append(gain.reshape(1, k).astype(F32))
    in_specs.append(pl.BlockSpec((k, tn), lambda i, j: (0, j)))
    args.append(w)
    if residual is not None:
        in_specs.append(pl.BlockSpec((tm, tn), lambda i, j: (i, j)))
        args.append(residual)
    return pl.pallas_call(
        functools.partial(_nm_kernel, has_norm=gain is not None, has_res=residual is not None),
        grid=(m // tm, n // tn),
        in_specs=in_specs,
        out_specs=pl.BlockSpec((tm, tn), lambda i, j: (i, j)),
        out_shape=jax.ShapeDtypeStruct((m, n), out_dtype),
        scratch_shapes=[pltpu.VMEM((tm, k), BF16)],
        compiler_params=_params("parallel", "arbitrary"),
        name="norm_matmul",
    )(*args)


def _out_proj_kernel(a_ref, b_ref, wa_ref, wb_ref, r_ref, o_ref):
    o_ref[...] = r_ref[...] + _bdot(a_ref[...], wa_ref[...]) + _bdot(b_ref[...], wb_ref[...])


def out_proj(a, b, w, residual, tm=512, tn=512):
    m, ka = a.shape
    n = w.shape[1]
    assert b.shape[1] == ka and w.shape[0] == 2 * ka
    return pl.pallas_call(
        _out_proj_kernel,
        grid=(m // tm, n // tn),
        in_specs=[
            pl.BlockSpec((tm, ka), lambda i, j: (i, 0)),
            pl.BlockSpec((tm, ka), lambda i, j: (i, 0)),
            pl.BlockSpec((ka, tn), lambda i, j: (0, j)),
            pl.BlockSpec((ka, tn), lambda i, j: (1, j)),
            pl.BlockSpec((tm, tn), lambda i, j: (i, j)),
        ],
        out_specs=pl.BlockSpec((tm, tn), lambda i, j: (i, j)),
        out_shape=jax.ShapeDtypeStruct((m, n), F32),
        compiler_params=_params("parallel", "arbitrary"),
        name="out_proj",
    )(a, b, w, w, residual)


def _ple_kernel(x_ref, yt_ref, g_ref, wg_ref, p_ref, wp_ref, o_ref, xn_ref, xs_ref):
    j = pl.program_id(1)
    tn = o_ref.shape[1]

    @pl.when(j == 0)
    def _():
        x = x_ref[...] + yt_ref[...].T
        xn_ref[...] = _rms(x, g_ref[...]).astype(BF16)
        for jj in range(xs_ref.shape[0]):
            xs_ref[jj] = x[:, jj * tn:(jj + 1) * tn]

    gate = _sigmoid(jnp.dot(xn_ref[...], wg_ref[...], preferred_element_type=F32))
    emb = _bdot(p_ref[...], wp_ref[...])
    o_ref[...] = xs_ref[j] + gate * emb


def ple_update(x, yt, p_i, norm_g, w_gate, w_proj, tm=512, tn=512):
    m, d = x.shape
    pd = p_i.shape[1]
    return pl.pallas_call(
        _ple_kernel,
        grid=(m // tm, d // tn),
        in_specs=[
            pl.BlockSpec((tm, d), lambda i, j: (i, 0)),
            pl.BlockSpec((d, tm), lambda i, j: (0, i)),
            pl.BlockSpec((1, d), lambda i, j: (0, 0)),
            pl.BlockSpec((d, tn), lambda i, j: (0, j)),
            pl.BlockSpec((tm, pd), lambda i, j: (i, 0)),
            pl.BlockSpec((pd, tn), lambda i, j: (0, j)),
        ],
        out_specs=pl.BlockSpec((tm, tn), lambda i, j: (i, j)),
        out_shape=jax.ShapeDtypeStruct((m, d), F32),
        scratch_shapes=[pltpu.VMEM((tm, d), BF16), pltpu.VMEM((d // tn, tm, tn), F32)],
        compiler_params=_params("parallel", "arbitrary"),
        name="ple_update",
    )(x, yt, norm_g.reshape(1, d), w_gate, p_i, w_proj)


def _rmsnorm_kernel(x_ref, g_ref, o_ref):
    o_ref[...] = _rms(x_ref[...], g_ref[...])


def rmsnorm(x, gain, tm=512):
    m, d = x.shape
    return pl.pallas_call(
        _rmsnorm_kernel,
        grid=(m // tm,),
        in_specs=[pl.BlockSpec((tm, d), lambda i: (i, 0)), pl.BlockSpec((1, d), lambda i: (0, 0))],
        out_specs=pl.BlockSpec((tm, d), lambda i: (i, 0)),
        out_shape=jax.ShapeDtypeStruct((m, d), F32),
        compiler_params=_params("parallel"),
        name="rmsnorm",
    )(x, gain.reshape(1, d))


def _peer_fold_kernel(keys_ref, wq_ref, o_ref):
    o_ref[0, 0] = _hdot_nt(keys_ref[0, 0], wq_ref[...])


def peer_fold(w_q, keys):
    hk = PEER_QDIM // 2
    return pl.pallas_call(
        _peer_fold_kernel,
        grid=(2, PEER_HEADS),
        in_specs=[
            pl.BlockSpec((1, 1, PEER_NKEYS, hk), lambda c, h: (h, c, 0, 0)),
            pl.BlockSpec((D_MODEL, hk), lambda c, h: (0, h * 2 + c)),
        ],
        out_specs=pl.BlockSpec((1, 1, PEER_NKEYS, D_MODEL), lambda c, h: (c, h, 0, 0)),
        out_shape=jax.ShapeDtypeStruct((2, PEER_HEADS, PEER_NKEYS, D_MODEL), F32),
        compiler_params=_params("parallel", "parallel"),
        name="peer_fold",
    )(keys, w_q)


def _rmsnorm_t_kernel(x_ref, g_ref, o_ref):
    o_ref[...] = _rms(x_ref[...], g_ref[...]).T.astype(BF16)


def rmsnorm_t(x, gain, tm=512):
    m, d = x.shape
    return pl.pallas_call(
        _rmsnorm_t_kernel,
        grid=(m // tm,),
        in_specs=[pl.BlockSpec((tm, d), lambda i: (i, 0)), pl.BlockSpec((1, d), lambda i: (0, 0))],
        out_specs=pl.BlockSpec((d, tm), lambda i: (0, i)),
        out_shape=jax.ShapeDtypeStruct((d, m), BF16),
        compiler_params=_params("parallel"),
        name="rmsnorm_t",
    )(x, gain.reshape(1, d))


def _sort_desc(v):
    v = list(v)
    n = len(v)
    k = 2
    while k <= n:
        j = k // 2
        while j >= 1:
            for i in range(n):
                l = i ^ j
                if l > i:
                    hi, lo = jnp.maximum(v[i], v[l]), jnp.minimum(v[i], v[l])
                    v[i], v[l] = (hi, lo) if (i & k) == 0 else (lo, hi)
            j //= 2
        k *= 2
    return v


def _merge_top(a, b):
    n = len(a)
    v = [jnp.maximum(a[i], b[n - 1 - i]) for i in range(n)]
    j = n // 2
    while j >= 1:
        for i in range(n):
            l = i ^ j
            if l > i:
                v[i], v[l] = jnp.maximum(v[i], v[l]), jnp.minimum(v[i], v[l])
        j //= 2
    return v


def _top_sorted(vals, n):
    vals = list(vals)
    while len(vals) % n:
        vals.append(jnp.full_like(vals[0], -jnp.inf))
    acc = _sort_desc(vals[:n])
    for g in range(1, len(vals) // n):
        acc = _merge_top(acc, _sort_desc(vals[g * n:(g + 1) * n]))
    return acc


def _count_leading(pred, b):
    t0 = pred(b[15])
    t1 = pred(b[7])
    t2 = pred(jnp.where(t1, b[11], b[3]))
    t3 = pred(jnp.where(t1, jnp.where(t2, b[13], b[9]), jnp.where(t2, b[5], b[1])))
    hi = jnp.where(t2, jnp.where(t3, b[14], b[12]), jnp.where(t3, b[10], b[8]))
    lo = jnp.where(t2, jnp.where(t3, b[6], b[4]), jnp.where(t3, b[2], b[0]))
    t4 = pred(jnp.where(t1, hi, lo))
    cnt = (jnp.where(t1, 8.0, 0.0) + jnp.where(t2, 4.0, 0.0)) + (jnp.where(t3, 2.0, 0.0) + jnp.where(t4, 1.0, 0.0))
    return jnp.where(t0, 16.0, cnt)


_PEER_PAIRS = [(i, j) for i in range(PEER_TOPK) for j in range(PEER_TOPK) if (i + 1) * (j + 1) <= PEER_TOPK]


def _peer_select_kernel(wf_ref, ht_ref, e1_ref, n1_ref, r2_ref, e2_ref, sub_ref):
    nk, k = PEER_NKEYS, PEER_TOPK
    tt = ht_ref.shape[1]
    ht = ht_ref[...]
    for c in range(2):
        sub_ref[c] = jnp.dot(wf_ref[c], ht, preferred_element_type=F32)
    row = lax.broadcasted_iota(jnp.int32, (SUBLANES, LANES), 0)

    def head_row(v, h):
        return jnp.sum(jnp.where(row == h, v, 0.0), axis=0, keepdims=True)

    def lane_group(lg, carry):
        lanes = pl.ds(pl.multiple_of(lg * LANES, LANES), LANES)

        def top_of_head(h, packed, c):
            base = pl.multiple_of(h * nk, nk)
            slabs = [sub_ref[c, pl.ds(base + SUBLANES * j, SUBLANES), lanes] for j in range(nk // SUBLANES)]
            top = _sort_desc(slabs)
            for sh in (4, 2, 1):
                top = _merge_top(top, [pltpu.roll(t, sh, 0) for t in top])
            return tuple(jnp.where(row == h, top[i], packed[i]) for i in range(k))

        zero = tuple(jnp.zeros((SUBLANES, LANES), F32) for _ in range(k))
        a = lax.fori_loop(0, PEER_HEADS, functools.partial(top_of_head, c=0), zero)
        b = lax.fori_loop(0, PEER_HEADS, functools.partial(top_of_head, c=1), zero)
        best = _top_sorted([a[i] + b[j] for i, j in _PEER_PAIRS], k)
        thr, vmax = best[k - 1], best[0]
        z = jnp.zeros((SUBLANES, LANES), F32)
        for i in range(k):
            z = z + jnp.exp(best[i] - vmax)
        inv_z = 1.0 / z

        def emit(h, carry):
            base = pl.multiple_of(h * nk, nk)
            s1 = sub_ref[0, pl.ds(base, nk), lanes]
            s2 = sub_ref[1, pl.ds(base, nk), lanes]
            thr_h = head_row(thr, h)
            bh = [head_row(b[j], h) for j in range(k)]
            n1 = _count_leading(lambda x: s1 + x >= thr_h, bh)
            r2 = _count_leading(lambda x: x > s2, bh)
            e1_ref[h, :, lanes] = jnp.exp(s1 - head_row(a[0], h)) * head_row(inv_z, h)
            n1_ref[h, :, lanes] = n1
            r2_ref[h, :, lanes] = r2.astype(BF16)
            e2_ref[h, :, lanes] = jnp.exp(s2 - head_row(b[0], h)).astype(BF16)
            return carry

        lax.fori_loop(0, PEER_HEADS, emit, 0)
        return carry

    lax.fori_loop(0, tt // LANES, lane_group, 0)


def peer_select(wf, ht, tt=256):
    d, s = ht.shape
    nrow = PEER_HEADS * PEER_NKEYS
    shape = (PEER_HEADS, PEER_NKEYS, s)
    ospec = pl.BlockSpec((PEER_HEADS, PEER_NKEYS, tt), lambda i: (0, 0, i))
    return pl.pallas_call(
        _peer_select_kernel,
        grid=(s // tt,),
        in_specs=[pl.BlockSpec((2, nrow, d), lambda i: (0, 0, 0)), pl.BlockSpec((d, tt), lambda i: (0, i))],
        out_specs=[ospec] * 4,
        out_shape=[jax.ShapeDtypeStruct(shape, F32)] * 2 + [jax.ShapeDtypeStruct(shape, BF16)] * 2,
        scratch_shapes=[pltpu.VMEM((2, nrow, tt), F32)],
        compiler_params=_params("parallel"),
        name="peer_select",
    )(wf, ht)


def _gelu_tanh(x):
    return 0.5 * x * (1.0 + jnp.tanh(math.sqrt(2.0 / math.pi) * (x + 0.044715 * (x * x * x))))


def _peer_dense_kernel(u_ref, ht_ref, vt_ref, e1_ref, n1_ref, r2_ref, e2_ref, o_ref, act0_ref, act1_ref, ga_ref):
    nk = PEER_NKEYS
    j = pl.program_id(1)
    last = pl.num_programs(1) - 1
    eb = u_ref.shape[0]
    nblk = eb // nk
    done = jnp.maximum(j - 1, 0)

    def step(prev_ref, next_ref):
        if prev_ref is not None:
            for ii in range(nblk):
                i1 = done * nblk + ii
                act = _gelu_tanh(prev_ref[pl.ds(ii * nk, nk), :].astype(BF16))
                gate = None
                for h in range(PEER_HEADS):
                    n1 = n1_ref[h, pl.ds(i1, 1), :].astype(BF16)
                    e1 = e1_ref[h, pl.ds(i1, 1), :].astype(BF16)
                    g = jnp.where(r2_ref[h] < n1, e2_ref[h] * e1, jnp.zeros((), BF16))
                    gate = g if gate is None else gate + g
                ga_ref[pl.ds(ii * nk, nk), :] = gate * act
        if next_ref is not None:
            next_ref[...] = jnp.dot(u_ref[...], ht_ref[...], preferred_element_type=F32)
        if prev_ref is not None:
            o_ref[...] += jnp.dot(vt_ref[...], ga_ref[...], preferred_element_type=F32)
        else:
            o_ref[...] = jnp.zeros_like(o_ref)

    pl.when(j == 0)(functools.partial(step, None, act0_ref))
    pl.when(j % 2 == 1)(functools.partial(step, act0_ref, act1_ref))
    pl.when((j % 2 == 0) & (j > 0) & (j < last))(functools.partial(step, act1_ref, act0_ref))
    pl.when(j == last)(functools.partial(step, act1_ref, None))


PEER_EB = 1024


def peer_dense(u_all, ht, vt_all, layer, e1, n1, r2, e2, tt=512):
    d, s = ht.shape
    ne, eb = u_all.shape[1], PEER_EB
    assert (ne // eb) % 2 == 0
    last = ne // eb - 1
    gspec = pl.BlockSpec((PEER_HEADS, PEER_NKEYS, tt), lambda i, j: (0, 0, i))
    return pl.pallas_call(
        _peer_dense_kernel,
        grid=(s // tt, ne // eb + 1),
        in_specs=[
            pl.BlockSpec((None, eb, d), lambda i, j: (layer, jnp.minimum(j, last), 0)),
            pl.BlockSpec((d, tt), lambda i, j: (0, i)),
            pl.BlockSpec((None, None, d, eb), lambda i, j: (layer, jnp.maximum(j - 1, 0), 0, 0)),
            gspec, gspec, gspec, gspec,
        ],
        out_specs=pl.BlockSpec((d, tt), lambda i, j: (0, i)),
        out_shape=jax.ShapeDtypeStruct((d, s), F32),
        scratch_shapes=[pltpu.VMEM((eb, tt), F32), pltpu.VMEM((eb, tt), F32), pltpu.VMEM((eb, tt), BF16)],
        compiler_params=_params("parallel", "arbitrary"),
        name="peer_dense",
    )(u_all, ht, vt_all, e1, n1, r2, e2)


def prep_peer_tables(peer_u, peer_v):
    nl, ne, d = peer_v.shape
    vt_all = peer_v.astype(BF16).reshape(nl, ne // PEER_EB, PEER_EB, d).transpose(0, 1, 3, 2)
    return peer_u.astype(BF16), vt_all


def peer(x, ln_g, w_q, keys, u_all, vt_all, layer, tm=512, tt=256):
    wf = peer_fold(w_q, keys).reshape(2, PEER_HEADS * PEER_NKEYS, D_MODEL).astype(BF16)
    ht = rmsnorm_t(x, ln_g, tm=tm)
    return peer_dense(u_all, ht, vt_all, layer, *peer_select(wf, ht, tt=tt), tt=tm)


MLA_HEAD_GROUP = 4


def _mla_kernel(qi_ref, ki_ref, qn_ref, qr_ref, qrot_ref, cq_ref, sq_ref, kn_ref, kr_ref, krot_ref, ck_ref, sk_ref,
                vt_ref, o_ref, q1_ref, q2_ref, m_ref, l_ref, acc_ref, *, scale):
    t = pl.program_id(1)
    qi, ki = qi_ref[t], ki_ref[t]
    tq, tk = qn_ref.shape[1], kn_ref.shape[0]
    ratio = tq // tk
    dn, dr, dv = MLA_NOPE, MLA_ROPE, MLA_V
    heads = range(qn_ref.shape[0] // dn)

    @pl.when(ki == 0)
    def _():
        q1_ref[...] = (qn_ref[...] * scale).astype(BF16)
        for h in heads:
            rows = pl.ds(h * dr, dr)
            q2_ref[rows, :] = ((qr_ref[rows, :] * cq_ref[...] + qrot_ref[rows, :] * sq_ref[...]) * scale).astype(BF16)
        m_ref[...] = jnp.full_like(m_ref, -jnp.inf)
        l_ref[...] = jnp.zeros_like(l_ref)
        acc_ref[...] = jnp.zeros_like(acc_ref)

    def step(masked):
        kr = (kr_ref[...] * ck_ref[...] + krot_ref[...] * sk_ref[...]).astype(BF16)
        s = [jnp.dot(kn_ref[:, h * dn:(h + 1) * dn], q1_ref[h * dn:(h + 1) * dn, :], preferred_element_type=F32)
             + jnp.dot(kr, q2_ref[h * dr:(h + 1) * dr, :], preferred_element_type=F32) for h in heads]
        if masked:
            kpos = ki * tk + lax.broadcasted_iota(jnp.int32, (tk, tq), 0)
            qpos = qi * tq + lax.broadcasted_iota(jnp.int32, (tk, tq), 1)
            keep = kpos <= qpos
            s = [jnp.where(keep, x, -jnp.inf) for x in s]
        m_old = [m_ref[h] for h in heads]
        m_new = [jnp.maximum(a, jnp.max(x, axis=0, keepdims=True)) for a, x in zip(m_old, s)]
        alpha = [jnp.exp(a - b) for a, b in zip(m_old, m_new)]
        p = [jnp.exp(x - b) for x, b in zip(s, m_new)]
        for h in heads:
            rows = pl.ds(h * dv, dv)
            l_ref[h] = alpha[h] * l_ref[h] + jnp.sum(p[h], axis=0, keepdims=True)
            acc_ref[rows, :] = alpha[h] * acc_ref[rows, :] + jnp.dot(vt_ref[rows, :], p[h].astype(BF16),
                                                                       preferred_element_type=F32)
            m_ref[h] = m_new[h]

    pl.when(ki < qi * ratio)(functools.partial(step, False))
    pl.when(ki >= qi * ratio)(functools.partial(step, True))

    @pl.when(ki == (qi + 1) * ratio - 1)
    def _():
        for h in heads:
            rows = pl.ds(h * dv, dv)
            o_ref[rows, :] = acc_ref[rows, :] / l_ref[h]


def mla_attention(qt, kv, vt, kr, krot, cos, sin, cos_t, sin_t, tq=1024, tk=512):
    s = kv.shape[0]
    tq, tk = min(tq, s), min(tk, s)
    ratio = tq // tk
    hh, dn, dr = MLA_HEADS, MLA_NOPE, MLA_ROPE
    pairs = [(qi, ki) for qi in range(s // tq) for ki in range((qi + 1) * ratio)]
    qi_tab = jnp.array([pr[0] for pr in pairs], jnp.int32)
    ki_tab = jnp.array([pr[1] for pr in pairs], jnp.int32)
    g = MLA_HEAD_GROUP
    r0 = hh * dn // (g * dr)
    qmap = lambda h, t, qt_, kt_: (0, qt_[t])
    kmap = lambda h, t, qt_, kt_: (kt_[t], 0)
    grid_spec = pltpu.PrefetchScalarGridSpec(
        num_scalar_prefetch=2,
        grid=(hh // g, len(pairs)),
        in_specs=[
            pl.BlockSpec((g * dn, tq), lambda h, t, qt_, kt_: (h, qt_[t])),
            pl.BlockSpec((g * dr, tq), lambda h, t, qt_, kt_: (r0 + h, qt_[t])),
            pl.BlockSpec((g * dr, tq), lambda h, t, qt_, kt_: (r0 + hh // g + h, qt_[t])),
            pl.BlockSpec((dr, tq), qmap),
            pl.BlockSpec((dr, tq), qmap),
            pl.BlockSpec((tk, g * dn), lambda h, t, qt_, kt_: (kt_[t], h)),
            pl.BlockSpec((tk, dr), kmap),
            pl.BlockSpec((tk, dr), kmap),
            pl.BlockSpec((tk, dr), kmap),
            pl.BlockSpec((tk, dr), kmap),
            pl.BlockSpec((g * MLA_V, tk), lambda h, t, qt_, kt_: (h, kt_[t])),
        ],
        out_specs=pl.BlockSpec((g * MLA_V, tq), lambda h, t, qt_, kt_: (h, qt_[t])),
        scratch_shapes=[pltpu.VMEM((g * dn, tq), BF16), pltpu.VMEM((g * dr, tq), BF16), pltpu.VMEM((g, 1, tq), F32),
                        pltpu.VMEM((g, 1, tq), F32), pltpu.VMEM((g * MLA_V, tq), F32)],
    )
    return pl.pallas_call(
        functools.partial(_mla_kernel, scale=(MLA_NOPE + MLA_ROPE) ** -0.5),
        grid_spec=grid_spec,
        out_shape=jax.ShapeDtypeStruct((hh * MLA_V, s), F32),
        compiler_params=_params("parallel", "arbitrary"),
        name="mla_attention",
    )(qi_tab, ki_tab, qt, qt, qt, cos_t, sin_t, kv, kr, krot, cos, sin, vt)


def _causal_conv(cur_ref, prev_ref, w_ref, first, bias=None):
    prev = jnp.where(first, 0.0, prev_ref[...])
    xe = jnp.concatenate([prev, cur_ref[...]], axis=0)
    w = w_ref[...]
    acc = w[CONV_WIDTH - 1:CONV_WIDTH] * xe[SUBLANES:]
    for j in range(CONV_WIDTH - 1):
        acc = acc + w[j:j + 1] * pltpu.roll(xe, CONV_WIDTH - 1 - j, 0)[SUBLANES:]
    return acc if bias is None else acc + bias


def _unit_lower_inverses(lows, c):
    n = lows[0].shape[0]
    r = lax.broadcasted_iota(jnp.int32, (n, n), 0)
    q = lax.broadcasted_iota(jnp.int32, (n, n), 1)
    eye = (r == q).astype(F32)
    prev = [jnp.where((r // 16) == (q // 16), low, 0.0) for low in lows]
    ps = [-d for d in prev]
    xs = [eye + p for p in ps]
    for _ in range(3):
        ps = [_bdot(p, p) for p in ps]
        xs = [x + _bdot(x, p) for x, p in zip(xs, ps)]
    size = 32
    while size <= c:
        cur = [jnp.where((r // size) == (q // size), low, 0.0) for low in lows] if size < c else lows
        ts = [_bdot(x, cu - pr) for x, cu, pr in zip(xs, cur, prev)]
        xs = [x - _bdot(t, x) for x, t in zip(xs, ts)]
        prev = cur
        size *= 2
    return xs


GDN_HEAD_GROUP = 4


def _gdn_kernel(q_ref, qp_ref, k_ref, kp_ref, v_ref, vp_ref, z_ref, wq_ref, wk_ref, wv_ref,
                ar_ref, br_ref, alog_ref, dtb_ref, ng_ref, o_ref, st_ref):
    c = GDN_CHUNK
    tt = q_ref.shape[0]
    heads = range(q_ref.shape[1] // GDN_DK)
    first = pl.program_id(1) == 0

    @pl.when(first)
    def _():
        st_ref[...] = jnp.zeros_like(st_ref)

    def l2n(x):
        return x * lax.rsqrt(jnp.sum(x * x, axis=-1, keepdims=True) + 1e-6)

    lanes = lambda x, h: x[:, h * GDN_DK:(h + 1) * GDN_DK]
    q_all = _silu(_causal_conv(q_ref, qp_ref, wq_ref, first))
    k_all = _silu(_causal_conv(k_ref, kp_ref, wk_ref, first))
    v_all = _silu(_causal_conv(v_ref, vp_ref, wv_ref, first))
    q = [l2n(lanes(q_all, h)) * (GDN_DK ** -0.5) for h in heads]
    k = [l2n(lanes(k_all, h)) for h in heads]
    v = [lanes(v_all, h) for h in heads]
    neg_a = [-jnp.exp(alog_ref[h, :, 0:1]) for h in heads]
    dtb = [dtb_ref[h, :, 0:1] for h in heads]
    nb = 2 * c
    r = lax.broadcasted_iota(jnp.int32, (nb, nb), 0)
    cc = lax.broadcasted_iota(jnp.int32, (nb, nb), 1)
    same = (r // c) == (cc // c)
    incl, strict = same & (r >= cc), same & (r > cc)
    incl_t = same & (r <= cc)
    top = lax.broadcasted_iota(jnp.int32, (nb, 1), 0) < c
    nblocks = tt // nb
    items = [(n, h) for n in range(nblocks) for h in heads]
    sl = lambda n: slice(n * nb, (n + 1) * nb)
    qs = [q[h][sl(n)] for n, h in items]
    ks = [k[h][sl(n)] for n, h in items]
    vs = [v[h][sl(n)] for n, h in items]
    as_col = lambda row: jnp.sum(jnp.where(r == cc, row, 0.0), axis=1, keepdims=True)
    g_row = [neg_a[h] * _softplus(ar_ref[h, n] + dtb[h]) for n, h in items]
    g_col = [as_col(g) for g in g_row]
    beta = [as_col(_sigmoid(br_ref[h, n])) for n, h in items]
    gc = [jnp.sum(jnp.where(incl, g, 0.0), axis=1, keepdims=True) for g in g_row]
    gr = [jnp.sum(jnp.where(incl_t, g, 0.0), axis=0, keepdims=True) for g in g_col]
    decay = [jnp.where(incl, jnp.exp(jnp.where(incl, a - b, 0.0)), 0.0) for a, b in zip(gc, gr)]
    kk = [_bdot_nt(x, x) for x in ks]
    qk = [_bdot_nt(a, b) for a, b in zip(qs, ks)]
    inv = _unit_lower_inverses([jnp.where(strict, b * m * d, 0.0) for b, m, d in zip(beta, kk, decay)], c)
    eg = [jnp.exp(g) for g in gc]
    sol = [_bdot(x, jnp.concatenate([b * vv, (b * e) * kx], axis=1)) for x, b, e, vv, kx in zip(inv, beta, eg, vs, ks)]
    a_qk = [m * d for m, d in zip(qk, decay)]
    k_end = [kx * jnp.exp(jnp.where(top, g[c - 1:c], g[nb - 1:nb]) - g) for g, kx in zip(gc, ks)]
    k_end_t = [[ke[:c].T, ke[c:].T] for ke in k_end]
    q_dec = [a * e for a, e in zip(qs, eg)]
    state = [st_ref[h] for h in heads]
    us = [[None, None] for _ in items]
    o_state = [[None, None] for _ in items]
    for n in range(nblocks):
        for j in range(2):
            cs = slice(j * c, (j + 1) * c)
            for h in heads:
                it = n * len(heads) + h
                u = sol[it][cs, :GDN_DV] - _bdot(sol[it][cs, GDN_DV:], state[h])
                o_state[it][j] = _bdot(q_dec[it][cs], state[h])
                state[h] = jnp.exp(gc[it][(j + 1) * c - 1:(j + 1) * c]) * state[h] + _bdot(k_end_t[it][j], u)
                us[it][j] = u
    for h in heads:
        st_ref[h] = state[h]
    o_blk = [jnp.concatenate(o_state[it], axis=0) + _bdot(a_qk[it], jnp.concatenate(us[it], axis=0))
             for it in range(len(items))]
    o = jnp.concatenate([_rms(jnp.concatenate([o_blk[n * len(heads) + h] for n in range(nblocks)], axis=0), ng_ref[...])
                         for h in heads], axis=1)
    o_ref[...] = o * _silu(z_ref[...])


def gated_delta_net(proj, col0, a_raw, b_raw, conv_w, a_log, dt_bias, norm_g, tt=256):
    s = proj.shape[0]
    tt = min(tt, s)
    hh, c, g = GDN_HEADS, GDN_CHUNK, GDN_HEAD_GROUP
    w = g * GDN_DK
    assert col0 % w == 0
    b0 = col0 // w
    nblk = GDN_HEADS * GDN_DK // w

    def cur(grp):
        return pl.BlockSpec((tt, w), lambda h, i: (i, b0 + grp * nblk + h))

    def prev(grp):
        return pl.BlockSpec((SUBLANES, w), lambda h, i: (jnp.maximum(i * (tt // SUBLANES) - 1, 0), b0 + grp * nblk + h))

    def wspec(grp):
        return pl.BlockSpec((CONV_WIDTH, w), lambda h, i: (0, grp * nblk + h))

    a_t, b_t = a_raw.T, b_raw.T
    c = 2 * c
    rowspec = pl.BlockSpec((g, tt // c, 1, c), lambda h, i: (h, i, 0, 0))
    hspec = pl.BlockSpec((g, 1, LANES), lambda h, i: (h, 0, 0))
    bcast = lambda p: jnp.broadcast_to(p.astype(F32)[:, None, None], (hh, 1, LANES))
    return pl.pallas_call(
        _gdn_kernel,
        grid=(hh // g, s // tt),
        in_specs=[cur(0), prev(0), cur(1), prev(1), cur(2), prev(2), cur(3), wspec(0), wspec(1), wspec(2),
                  rowspec, rowspec, hspec, hspec, pl.BlockSpec((1, GDN_DV), lambda h, i: (0, 0))],
        out_specs=pl.BlockSpec((tt, w), lambda h, i: (i, h)),
        out_shape=jax.ShapeDtypeStruct((s, hh * GDN_DV), F32),
        scratch_shapes=[pltpu.VMEM((g, GDN_DK, GDN_DV), F32)],
        compiler_params=_params("parallel", "arbitrary"),
        name="gated_delta_net",
    )(proj, proj, proj, proj, proj, proj, proj, conv_w, conv_w, conv_w,
      a_t.reshape(hh, s // c, 1, c), b_t.reshape(hh, s // c, 1, c),
      bcast(a_log), bcast(dt_bias), norm_g.reshape(1, GDN_DV).astype(F32))


def _rot_half_cols(w, half):
    return jnp.concatenate([-w[..., half:], w[..., :half]], axis=-1)


def _pad_cols(w, n):
    return jnp.pad(w, ((0, 0), (0, n - w.shape[1])))


AB_QKVZ = 0
AB_CQ = 4 * GDN_HEADS * GDN_DK
AB_CKV = AB_CQ + MLA_Q_RANK
AB_KR = AB_CKV + MLA_KV_RANK
AB_GATES = AB_KR + 2 * MLA_ROPE
AB_PAD = AB_GATES + LANES


def prep_ab(w_in, w_uq, w_ukv, w_out):
    rq, rkv, rr = MLA_Q_RANK, MLA_KV_RANK, MLA_ROPE
    w_kr = w_in[:, rq + rkv:rq + rkv + rr]
    gdn_w = 4 * GDN_HEADS * GDN_DK
    g0 = rq + rkv + rr
    w_main = jnp.concatenate([w_in[:, g0:g0 + gdn_w], w_in[:, :rq + rkv], w_kr, _rot_half_cols(w_kr, rr // 2),
                              w_in[:, g0 + gdn_w:]], axis=1)
    w_main = _pad_cols(w_main, AB_PAD).astype(BF16)
    uq = w_uq.reshape(rq, MLA_HEADS, MLA_NOPE + MLA_ROPE)
    uq_r = uq[..., MLA_NOPE:]
    uq2 = jnp.concatenate([uq[..., :MLA_NOPE].reshape(rq, -1), uq_r.reshape(rq, -1),
                           _rot_half_cols(uq_r, rr // 2).reshape(rq, -1)], axis=1).astype(BF16)
    ukv = w_ukv.reshape(rkv, MLA_HEADS, MLA_NOPE + MLA_V)
    ukv2 = jnp.concatenate([ukv[..., :MLA_NOPE].reshape(rkv, -1), ukv[..., MLA_NOPE:].reshape(rkv, -1)], axis=1).astype(BF16)
    return w_main, uq2, ukv2, w_out.astype(BF16)


def mixer_ab(x, ln, w_main, uq2, ukv2, w_out, q_norm, kv_norm, conv_w, a_log, dt_bias, gdn_norm, cos2, sin2,
             tm=512, t_attn=512, t_gdn=256):
    s = x.shape[0]
    rq, rkv, rr = MLA_Q_RANK, MLA_KV_RANK, MLA_ROPE
    proj = norm_matmul(x, w_main, gain=ln, tm=min(2 * tm, x.shape[0]), tn=1024)
    qfull = norm_matmul(proj, uq2, gain=q_norm, tm=tm, x_col=AB_CQ)
    kv = norm_matmul(proj, ukv2, gain=kv_norm, tm=tm, out_dtype=BF16, x_col=AB_CKV)
    nn = MLA_HEADS * MLA_NOPE
    o_a = mla_attention(qfull.T, kv, kv[:, nn:].T, proj[:, AB_KR:AB_KR + rr], proj[:, AB_KR + rr:AB_GATES],
                        cos2, sin2, cos2.T, sin2.T, tq=2 * t_attn, tk=t_attn).T
    o_b = gated_delta_net(proj, AB_QKVZ, proj[:, AB_GATES:AB_GATES + GDN_HEADS],
                          proj[:, AB_GATES + GDN_HEADS:AB_GATES + 2 * GDN_HEADS], conv_w, a_log, dt_bias, gdn_norm, tt=t_gdn)
    return out_proj(o_a, o_b, w_out, x, tm=min(2 * tm, x.shape[0]), tn=1024)


CD_R, CD_K, CD_V, CD_Z, CD_DT, CD_X = (i * 1024 for i in range(6))
CD_B = 6144
CD_C = CD_B + SSD_GROUPS * SSD_STATE
CD_WA = CD_C + SSD_GROUPS * SSD_STATE
CD_G = CD_WA + LANES
CD_PAD = 7168
GROUP_W = SSD_INNER // SSD_GROUPS


def _lower_ones(c):
    r = lax.broadcasted_iota(jnp.int32, (c, c), 0)
    q = lax.broadcasted_iota(jnp.int32, (c, c), 1)
    return r >= q, r > q


def _ssd_kernel(x_ref, xp_ref, b_ref, bp_ref, c_ref, cp_ref, z_ref, dt_ref, wx_ref, wb_ref, wc_ref,
                bx_ref, bb_ref, bc_ref, dtb_ref, alog_ref, dskip_ref, ng_ref, o_ref, st_ref):
    c = SSD_CHUNK
    tt = x_ref.shape[0]
    first = pl.program_id(1) == 0

    @pl.when(first)
    def _():
        st_ref[...] = jnp.zeros_like(st_ref)

    xs_all = _silu(_causal_conv(x_ref, xp_ref, wx_ref, first, bx_ref[...]))
    bm_all = _silu(_causal_conv(b_ref, bp_ref, wb_ref, first, bb_ref[...]))
    cm_all = _silu(_causal_conv(c_ref, cp_ref, wc_ref, first, bc_ref[...]))
    dt_all = _softplus(dt_ref[...] + dtb_ref[...])
    a_all = -jnp.exp(alog_ref[...]) * dt_all
    incl, _ = _lower_ones(c)
    tri = incl.astype(F32)
    left = lax.broadcasted_iota(jnp.int32, (c, LANES), 1) < SSD_HEADDIM
    npair = GROUP_W // LANES
    outs = []
    for n in range(tt // c):
        sl = slice(n * c, (n + 1) * c)
        xs, bm, cm, dt = xs_all[sl], bm_all[sl], cm_all[sl], dt_all[sl]
        acs = _bdot(tri, a_all[sl])
        xdt = xs * dt
        cb = _bdot_nt(cm, bm)
        bm_t = bm.T
        ys = []
        for p in range(npair):
            ls = slice(p * LANES, (p + 1) * LANES)
            acs_p = acs[:, ls]
            acs_t = acs_p.T
            xp = xdt[:, ls]
            yd = []
            for hd in range(2):
                col = acs_p[:, hd * SSD_HEADDIM:hd * SSD_HEADDIM + 1]
                row = acs_t[hd * SSD_HEADDIM:hd * SSD_HEADDIM + 1, :]
                lmat = jnp.where(incl, jnp.exp(jnp.where(incl, col - row, 0.0)), 0.0)
                yd.append(_bdot(cb * lmat, xp))
            last = acs_p[c - 1:c]
            prev_t = st_ref[p]
            y_off = _bdot(cm, prev_t) * jnp.exp(acs_p)
            st_ref[p] = jnp.exp(last) * prev_t + _bdot(bm_t, xp * jnp.exp(last - acs_p))
            ys.append(jnp.where(left, yd[0], yd[1]) + y_off)
        outs.append(jnp.concatenate(ys, axis=1) + xs * dskip_ref[...])
    y = jnp.concatenate(outs, axis=0) * _silu(z_ref[...])
    o_ref[...] = _rms(y, ng_ref[...])


def mamba2_ssd(proj, conv_w, conv_b, dt_bias, a_log, d_skip, norm_g, tt=256):
    s = proj.shape[0]
    tt = min(tt, s)
    gw, ns = GROUP_W, SSD_STATE
    per = lambda v: jnp.repeat(v.astype(F32), SSD_HEADDIM).reshape(1, SSD_INNER)

    def cur(col, w):
        return pl.BlockSpec((tt, w), lambda g, i: (i, col // w + g))

    def prev(col, w):
        return pl.BlockSpec((SUBLANES, w), lambda g, i: (jnp.maximum(i * (tt // SUBLANES) - 1, 0), col // w + g))

    def par(rows, col, w):
        return pl.BlockSpec((rows, w), lambda g, i: (0, col // w + g))

    cb = conv_b.reshape(1, -1).astype(F32)
    return pl.pallas_call(
        _ssd_kernel,
        grid=(SSD_GROUPS, s // tt),
        in_specs=[cur(CD_X, gw), prev(CD_X, gw), cur(CD_B, ns), prev(CD_B, ns), cur(CD_C, ns), prev(CD_C, ns),
                  cur(CD_Z, gw), cur(CD_DT, gw),
                  par(CONV_WIDTH, 0, gw), par(CONV_WIDTH, SSD_INNER, ns), par(CONV_WIDTH, SSD_INNER + SSD_GROUPS * ns, ns),
                  par(1, 0, gw), par(1, SSD_INNER, ns), par(1, SSD_INNER + SSD_GROUPS * ns, ns),
                  par(1, 0, gw), par(1, 0, gw), par(1, 0, gw), par(1, 0, gw)],
        out_specs=pl.BlockSpec((tt, gw), lambda g, i: (i, g)),
        out_shape=jax.ShapeDtypeStruct((s, SSD_INNER), F32),
        scratch_shapes=[pltpu.VMEM((gw // LANES, ns, LANES), F32)],
        compiler_params=_params("parallel", "arbitrary"),
        name="mamba2_ssd",
    )(proj, proj, proj, proj, proj, proj, proj, proj, conv_w, conv_w, conv_w, cb, cb, cb,
      per(dt_bias), per(a_log), per(d_skip), norm_g.reshape(1, SSD_INNER).astype(F32))


def _pair_ones():
    r = lax.broadcasted_iota(jnp.int32, (LANES, LANES), 0)
    q = lax.broadcasted_iota(jnp.int32, (LANES, LANES), 1)
    return (r // RWKV_HEAD) == (q // RWKV_HEAD)


def _head_sums(x, ones):
    return jnp.concatenate([_hdot(x[:, i * LANES:(i + 1) * LANES], ones) for i in range(x.shape[1] // LANES)], axis=1)


def _rwkv_prep_kernel(r_ref, rp_ref, k_ref, kp_ref, v_ref, vp_ref, wa_ref, wap_ref, g0_ref, g0p_ref, g1_ref, g1p_ref,
                      mur_ref, muk_ref, muv_ref, muwa_ref, mug0_ref, mug1_ref, w0_ref, w2_ref, a0_ref, a2_ref, g2_ref,
                      kk_ref, ka_ref, rk_ref,
                      ro_ref, lw_ref, ko_ref, vo_ref, po_ref, qo_ref, go_ref, bo_ref):
    first = pl.program_id(0) == 0

    def mix(cur_ref, prev_ref, mu_ref):
        cur = cur_ref[...]
        prev = jnp.where(first, 0.0, prev_ref[...])
        shifted = pltpu.roll(jnp.concatenate([prev, cur], axis=0), 1, 0)[SUBLANES:]
        return cur + (shifted - cur) * mu_ref[...]

    r = mix(r_ref, rp_ref, mur_ref)
    k = mix(k_ref, kp_ref, muk_ref)
    v = mix(v_ref, vp_ref, muv_ref)
    wa = mix(wa_ref, wap_ref, muwa_ref)
    g0 = mix(g0_ref, g0p_ref, mug0_ref)
    g1 = mix(g1_ref, g1p_ref, mug1_ref)
    log_w = -math.exp(-0.5) * _sigmoid(w0_ref[...] + _bdot(jnp.tanh(wa), w2_ref[...]))
    a = _sigmoid(a0_ref[...] + _bdot(wa, a2_ref[...]))
    gate = _bdot(_sigmoid(g0), g2_ref[0:LANES, :]) + _bdot(_sigmoid(g1), g2_ref[LANES:2 * LANES, :])
    ones = _pair_ones().astype(F32)
    kx = k * kk_ref[...]
    kk = kx * lax.rsqrt(_head_sums(kx * kx, ones) + 1e-6)
    k_mod = k * (1.0 + (a - 1.0) * ka_ref[...])
    ro_ref[...] = r
    lw_ref[...] = log_w
    ko_ref[...] = k_mod
    vo_ref[...] = v
    po_ref[...] = -kk * a
    qo_ref[...] = kk
    go_ref[...] = gate
    bo_ref[...] = _head_sums(r * k_mod * rk_ref[...], ones) * v


def rwkv_prep(proj, mu, w0, w2, a0, a2, g2, k_k, k_a, r_k, tt=256):
    s = proj.shape[0]
    tt = min(tt, s)
    ri = RWKV_INNER
    row = lambda v: v.reshape(1, -1).astype(F32)
    mu_r, mu_k, mu_v = (row(mu[i * ri:(i + 1) * ri]) for i in range(3))
    mu_wa = row(mu[3 * ri:3 * ri + LANES])
    mu_g = row(jnp.pad(mu[3 * ri + LANES:], (0, 2 * LANES - RWKV_G_LORA)))
    zeros = jnp.zeros((RWKV_W_LORA, ri), F32)
    w2p = jnp.concatenate([w2, zeros], axis=0).astype(BF16)
    a2p = jnp.concatenate([zeros, a2], axis=0).astype(BF16)
    g2p = jnp.pad(g2, ((0, 2 * LANES - RWKV_G_LORA), (0, 0))).astype(BF16)

    def cur(col, w):
        return pl.BlockSpec((tt, w), lambda i: (i, col // w))

    def prev(col, w):
        return pl.BlockSpec((SUBLANES, w), lambda i: (jnp.maximum(i * (tt // SUBLANES) - 1, 0), col // w))

    full = lambda a: pl.BlockSpec(a.shape, lambda i: (0, 0))
    params = [mu_r, mu_k, mu_v, mu_wa, mu_g[:, :LANES], mu_g[:, LANES:], row(w0), w2p, row(a0), a2p, g2p,
              row(k_k), row(k_a), row(r_k)]
    out = jax.ShapeDtypeStruct((s, ri), F32)
    return pl.pallas_call(
        _rwkv_prep_kernel,
        grid=(s // tt,),
        in_specs=[cur(CD_R, ri), prev(CD_R, ri), cur(CD_K, ri), prev(CD_K, ri), cur(CD_V, ri), prev(CD_V, ri),
                  cur(CD_WA, LANES), prev(CD_WA, LANES), cur(CD_G, LANES), prev(CD_G, LANES),
                  cur(CD_G + LANES, LANES), prev(CD_G + LANES, LANES)] + [full(a) for a in params],
        out_specs=[pl.BlockSpec((tt, ri), lambda i: (i, 0))] * 8,
        out_shape=[out] * 8,
        compiler_params=_params("arbitrary"),
        name="rwkv_prep",
    )(*([proj] * 12), *params)


RWKV_MY_CHUNK = 64


def _rwkv_scan_kernel(r_ref, lw_ref, k_ref, v_ref, p_ref, q_ref, g_ref, b_ref, lnw_ref, lnb_ref, o_ref, st_ref):
    c = RWKV_MY_CHUNK
    tt = r_ref.shape[0]

    @pl.when(pl.program_id(1) == 0)
    def _():
        st_ref[...] = jnp.zeros_like(st_ref)

    tri = _lower_ones(c)[0].astype(F32)
    left = lax.broadcasted_iota(jnp.int32, (c, LANES), 1) < RWKV_HEAD
    pair = _pair_ones()
    ones = pair.astype(F32)
    r_id = lax.broadcasted_iota(jnp.int32, (LANES, LANES), 0)
    c_id = lax.broadcasted_iota(jnp.int32, (LANES, LANES), 1)
    eye = r_id == c_id
    top = r_id < c
    strict = pair & ((r_id % c) > (c_id % c))
    incl = pair & ((r_id % c) >= (c_id % c))
    stack2 = lambda a: jnp.concatenate([a, a], axis=0)
    by_head = lambda a: jnp.concatenate([jnp.where(left, a, 0.0), jnp.where(left, 0.0, a)], axis=0)
    unstack = lambda a: jnp.where(left, a[:c], a[c:])
    pairs = range(r_ref.shape[1] // LANES)
    nchunks = tt // c
    items = [(n, pr) for n in range(nchunks) for pr in pairs]
    blk = lambda ref, n, pr: ref[n * c:(n + 1) * c, pr * LANES:(pr + 1) * LANES]
    w = [blk(lw_ref, n, pr) for n, pr in items]
    v = [blk(v_ref, n, pr) for n, pr in items]
    lw = [_bdot(tri, x) for x in w]
    lam_in = [jnp.exp(x) for x in lw]
    inv_lam = [jnp.exp(-x) for x in lw]
    q_bar = [blk(q_ref, n, pr) * jnp.exp(a - b) for (n, pr), a, b in zip(items, lw, w)]
    r_bar = [blk(r_ref, n, pr) * x for (n, pr), x in zip(items, lam_in)]
    pk = [jnp.concatenate([blk(p_ref, n, pr) * x, blk(k_ref, n, pr) * x], axis=0) for (n, pr), x in zip(items, inv_lam)]
    mq = [_bdot_nt(by_head(a), b) for a, b in zip(q_bar, pk)]
    mr = [_bdot_nt(by_head(a), b) for a, b in zip(r_bar, pk)]
    mq_sw = [pltpu.roll(x, c, 1) for x in mq]
    mr_sw = [pltpu.roll(x, c, 1) for x in mr]
    m_qp = [jnp.where(strict, jnp.where(top, a, b), 0.0) for a, b in zip(mq, mq_sw)]
    m_qk = [jnp.where(strict, jnp.where(top, b, a), 0.0) for a, b in zip(mq, mq_sw)]
    m_rp = [jnp.where(incl, jnp.where(top, a, b), 0.0) for a, b in zip(mr, mr_sw)]
    m_rk = [jnp.where(incl, jnp.where(top, b, a), 0.0) for a, b in zip(mr, mr_sw)]
    vv = [stack2(x) for x in v]
    inv = _unit_lower_inverses([-x for x in m_qp], c)
    qkv = [_bdot(a, b) for a, b in zip(m_qk, vv)]
    sol = [_bdot(x, jnp.concatenate([stack2(a), b], axis=1)) for x, a, b in zip(inv, q_bar, qkv)]
    ws = [unstack(x[:, :LANES]) for x in sol]
    wv = [unstack(x[:, LANES:]) for x in sol]
    y_loc = [unstack(_bdot(a, b)) for a, b in zip(m_rk, vv)]
    lam_end = [x[c - 1:c] for x in lam_in]
    lam_col = [jnp.sum(jnp.where(eye, x, 0.0), axis=1, keepdims=True) for x in lam_end]
    pk_end_t = [(a * b).T for a, b in zip(pk, lam_end)]
    state = [st_ref[pr] for pr in pairs]
    us, y_state = [], []
    for it, (n, pr) in enumerate(items):
        u = _bdot(ws[it], state[pr]) + wv[it]
        y_state.append(_bdot(r_bar[it], state[pr]))
        upd = _bdot(pk_end_t[it], jnp.concatenate([u, v[it]], axis=0))
        state[pr] = lam_col[it] * state[pr] + jnp.where(pair, upd, 0.0)
        us.append(u)
    for pr in pairs:
        st_ref[pr] = state[pr]
    outs = [a + unstack(_bdot(b, stack2(u))) + d for a, b, u, d in zip(y_state, m_rp, us, y_loc)]
    ys = []
    for pr in pairs:
        y = jnp.concatenate([outs[n * len(pairs) + pr] for n in range(nchunks)], axis=0)
        mean = _hdot(y, ones) * (1.0 / RWKV_HEAD)
        yc = y - mean
        var = _hdot(yc * yc, ones) * (1.0 / RWKV_HEAD)
        ys.append(yc * lax.rsqrt(var + RWKV_GN_EPS))
    y = jnp.concatenate(ys, axis=1) * lnw_ref[...] + lnb_ref[...]
    o_ref[...] = (y + b_ref[...]) * g_ref[...]


RWKV_PAIR_GROUP = 4


def rwkv_scan(r, lw, k, v, p, q, gate, bonus, ln_w, ln_b, tt=256):
    s = r.shape[0]
    tt = min(tt, s)
    w = RWKV_PAIR_GROUP * LANES
    spec = pl.BlockSpec((tt, w), lambda h, i: (i, h))
    pspec = pl.BlockSpec((1, w), lambda h, i: (0, h))
    return pl.pallas_call(
        _rwkv_scan_kernel,
        grid=(RWKV_INNER // w, s // tt),
        in_specs=[spec] * 8 + [pspec] * 2,
        out_specs=spec,
        out_shape=jax.ShapeDtypeStruct((s, RWKV_INNER), F32),
        scratch_shapes=[pltpu.VMEM((RWKV_PAIR_GROUP, LANES, LANES), F32)],
        compiler_params=_params("parallel", "arbitrary"),
        name="rwkv_scan",
    )(r, lw, k, v, p, q, gate, bonus, ln_w.reshape(1, -1).astype(F32), ln_b.reshape(1, -1).astype(F32))


def prep_cd(w_in, w_out):
    si, ri = SSD_INNER, RWKV_INNER
    z, xbc, dt, rw = w_in[:, :si], w_in[:, si:2 * si + 512], w_in[:, 2 * si + 512:2 * si + 528], w_in[:, 2 * si + 528:]
    dt_exp = jnp.repeat(dt, SSD_HEADDIM, axis=1)
    cols = [rw[:, :3 * ri], z, dt_exp, xbc, rw[:, 3 * ri:]]
    return _pad_cols(jnp.concatenate(cols, axis=1), CD_PAD).astype(BF16), w_out.astype(BF16)


def mixer_cd(x, ln, w_main, w_out, ssd_conv_w, ssd_conv_b, ssd_dt_bias, ssd_a_log, ssd_d, ssd_norm,
             mu, w0, w2, a0, a2, g2, k_k, k_a, r_k, ln_w, ln_b, tm=512, tt=256):
    proj = norm_matmul(x, w_main, gain=ln, tm=min(2 * tm, x.shape[0]), tn=1024)
    o_c = mamba2_ssd(proj, ssd_conv_w, ssd_conv_b, ssd_dt_bias, ssd_a_log, ssd_d, ssd_norm, tt=tt)
    o_d = rwkv_scan(*rwkv_prep(proj, mu, w0, w2, a0, a2, g2, k_k, k_a, r_k.reshape(-1), tt=tt), ln_w, ln_b, tt=2 * tt)
    return out_proj(o_c, o_d, w_out, x, tm=min(2 * tm, x.shape[0]), tn=1024)


def _rope_tables(s):
    inv = 1.0 / (ROPE_THETA ** (jnp.arange(0, MLA_ROPE, 2, dtype=F32) / MLA_ROPE))
    ang = jnp.arange(s, dtype=F32)[:, None] * inv[None, :]
    cos, sin = jnp.cos(ang), jnp.sin(ang)
    return jnp.concatenate([cos, cos], axis=1), jnp.concatenate([sin, sin], axis=1)


def kernel(x, p, ln_mix, ln_ffn, ab_w_in, mla_q_norm, mla_w_uq, mla_kv_norm, mla_w_ukv, gdn_conv_w, gdn_a_log, gdn_dt_bias, gdn_norm, ab_w_out, cd_w_in, ssd_conv_w, ssd_conv_b, ssd_dt_bias, ssd_a_log, ssd_d, ssd_norm, rwkv_mu, rwkv_w0, rwkv_w2, rwkv_a0, rwkv_a2, rwkv_g2, rwkv_k_k, rwkv_k_a, rwkv_r_k, rwkv_ln_w, rwkv_ln_b, cd_w_out, peer_w_q, peer_keys, peer_u, peer_v, ple_w_proj, ple_norm, ple_w_gate, final_norm):
    assert x.shape[0] == 1
    s = x.shape[1]
    tm = min(512, s)
    tt = min(256, s)
    cos2, sin2 = _rope_tables(s)
    u_all, vt_all = prep_peer_tables(peer_u, peer_v)
    xs = x[0]
    for i in range(DEPTH):
        j = i // 2
        if i % 2 == 0:
            wts = prep_ab(ab_w_in[j], mla_w_uq[j], mla_w_ukv[j], ab_w_out[j])
            xs = mixer_ab(xs, ln_mix[i], *wts, mla_q_norm[j], mla_kv_norm[j], gdn_conv_w[j], gdn_a_log[j],
                          gdn_dt_bias[j], gdn_norm[j], cos2, sin2, tm=tm, t_attn=tm, t_gdn=tm)
        else:
            wts = prep_cd(cd_w_in[j], cd_w_out[j])
            xs = mixer_cd(xs, ln_mix[i], *wts, ssd_conv_w[j], ssd_conv_b[j], ssd_dt_bias[j], ssd_a_log[j], ssd_d[j],
                          ssd_norm[j], rwkv_mu[j], rwkv_w0[j], rwkv_w2[j], rwkv_a0[j], rwkv_a2[j], rwkv_g2[j],
                          rwkv_k_k[j], rwkv_k_a[j], rwkv_r_k[j], rwkv_ln_w[j], rwkv_ln_b[j], tm=tm, tt=tt)
        yt = peer(xs, ln_ffn[i], peer_w_q[i], peer_keys[i], u_all, vt_all, i, tm=tm, tt=tt)
        xs = ple_update(xs, yt, p[i, 0], ple_norm[i], ple_w_gate[i].astype(BF16), ple_w_proj[i].astype(BF16), tm=tm, tn=1024)
    return rmsnorm(xs, final_norm, tm=tm)[None]
```

```python
import functools
import math

import jax
import jax.numpy as jnp
from jax import lax
from jax.experimental import pallas as pl
from jax.experimental.pallas import tpu as pltpu

F32 = jnp.float32
BF16 = jnp.bfloat16
HIGHEST = lax.Precision.HIGHEST

D_MODEL = 2048
DEPTH = 4
PLE_DIM = 256
RMS_EPS = 1e-6
MLA_HEADS = 8
MLA_Q_RANK = 512
MLA_KV_RANK = 256
MLA_NOPE = 128
MLA_ROPE = 64
MLA_V = 128
ROPE_THETA = 10000.0
GDN_HEADS = 8
GDN_DK = 128
GDN_DV = 128
GDN_CHUNK = 64
SSD_HEADS = 16
SSD_HEADDIM = 64
SSD_GROUPS = 2
SSD_STATE = 128
SSD_CHUNK = 128
SSD_INNER = SSD_HEADS * SSD_HEADDIM
RWKV_HEADS = 16
RWKV_HEAD = 64
RWKV_INNER = RWKV_HEADS * RWKV_HEAD
RWKV_W_LORA = 64
RWKV_A_LORA = 64
RWKV_G_LORA = 160
RWKV_GN_EPS = 64e-5
CONV_WIDTH = 4
PEER_HEADS = 8
PEER_NKEYS = 128
PEER_EXPERTS = PEER_NKEYS * PEER_NKEYS
PEER_QDIM = 256
PEER_TOPK = 16

LANES = 128
SUBLANES = 8
VMEM_LIMIT = 56 * 1024 * 1024


def _params(*sem):
    return pltpu.CompilerParams(dimension_semantics=sem, vmem_limit_bytes=VMEM_LIMIT)


def _bdot(a, b):
    return jnp.dot(a.astype(BF16), b.astype(BF16), preferred_element_type=F32)


def _bdot_nt(a, b):
    return lax.dot_general(a.astype(BF16), b.astype(BF16), (((1,), (1,)), ((), ())), preferred_element_type=F32)


def _hdot(a, b):
    return jnp.dot(a, b, preferred_element_type=F32, precision=HIGHEST)


def _hdot_nt(a, b):
    return lax.dot_general(a, b, (((1,), (1,)), ((), ())), preferred_element_type=F32, precision=HIGHEST)


def _rms(x, gain):
    return x * lax.rsqrt(jnp.mean(x * x, axis=-1, keepdims=True) + RMS_EPS) * gain


def _sigmoid(x):
    return 1.0 / (1.0 + jnp.exp(-x))


def _silu(x):
    return x * _sigmoid(x)


def _softplus(x):
    return jnp.maximum(x, 0.0) + jnp.log(1.0 + jnp.exp(-jnp.abs(x)))


def _nm_kernel(*refs, has_norm, has_res):
    it = iter(refs)
    x_ref = next(it)
    g_ref = next(it) if has_norm else None
    w_ref = next(it)
    r_ref = next(it) if has_res else None
    o_ref = next(it)
    xn_ref = next(it)

    @pl.when(pl.program_id(1) == 0)
    def _():
        x = x_ref[...].astype(F32)
        if has_norm:
            x = _rms(x, g_ref[...])
        xn_ref[...] = x.astype(BF16)

    acc = jnp.dot(xn_ref[...], w_ref[...], preferred_element_type=F32)
    if has_res:
        acc = acc + r_ref[...]
    o_ref[...] = acc.astype(o_ref.dtype)


def norm_matmul(x, w, gain=None, residual=None, tm=512, tn=512, out_dtype=F32, x_col=0):
    m = x.shape[0]
    k, n = w.shape
    tn = min(tn, n)
    assert m % tm == 0 and n % tn == 0 and x_col % k == 0
    in_specs = [pl.BlockSpec((tm, k), lambda i, j: (i, x_col // k))]
    args = [x]
    if gain is not None:
        in_specs.append(pl.BlockSpec((1, k), lambda i, j: (0, 0)))
        args.append(gain.reshape(1, k).astype(F32))
    in_specs.append(pl.BlockSpec((k, tn), lambda i, j: (0, j)))
    args.append(w)
    if residual is not None:
        in_specs.append(pl.BlockSpec((tm, tn), lambda i, j: (i, j)))
        args.append(residual)
    return pl.pallas_call(
        functools.partial(_nm_kernel, has_norm=gain is not None, has_res=residual is not None),
        grid=(m // tm, n // tn),
        in_specs=in_specs,
        out_specs=pl.BlockSpec((tm, tn), lambda i, j: (i, j)),
        out_shape=jax.ShapeDtypeStruct((m, n), out_dtype),
        scratch_shapes=[pltpu.VMEM((tm, k), BF16)],
        compiler_params=_params("parallel", "arbitrary"),
        name="norm_matmul",
    )(*args)


def _out_proj_kernel(a_ref, b_ref, wa_ref, wb_ref, r_ref, o_ref):
    o_ref[...] = r_ref[...] + _bdot(a_ref[...], wa_ref[...]) + _bdot(b_ref[...], wb_ref[...])


def out_proj(a, b, w, residual, tm=512, tn=512):
    m, ka = a.shape
    n = w.shape[1]
    assert b.shape[1] == ka and w.shape[0] == 2 * ka
    return pl.pallas_call(
        _out_proj_kernel,
        grid=(m // tm, n // tn),
        in_specs=[
            pl.BlockSpec((tm, ka), lambda i, j: (i, 0)),
            pl.BlockSpec((tm, ka), lambda i, j: (i, 0)),
            pl.BlockSpec((ka, tn), lambda i, j: (0, j)),
            pl.BlockSpec((ka, tn), lambda i, j: (1, j)),
            pl.BlockSpec((tm, tn), lambda i, j: (i, j)),
        ],
        out_specs=pl.BlockSpec((tm, tn), lambda i, j: (i, j)),
        out_shape=jax.ShapeDtypeStruct((m, n), F32),
        compiler_params=_params("parallel", "arbitrary"),
        name="out_proj",
    )(a, b, w, w, residual)


def _ple_kernel(x_ref, yt_ref, g_ref, wg_ref, p_ref, wp_ref, o_ref, xn_ref, xs_ref):
    j = pl.program_id(1)
    tn = o_ref.shape[1]

    @pl.when(j == 0)
    def _():
        x = x_ref[...] + yt_ref[...].T
        xn_ref[...] = _rms(x, g_ref[...]).astype(BF16)
        for jj in range(xs_ref.shape[0]):
            xs_ref[jj] = x[:, jj * tn:(jj + 1) * tn]

    gate = _sigmoid(jnp.dot(xn_ref[...], wg_ref[...], preferred_element_type=F32))
    emb = _bdot(p_ref[...], wp_ref[...])
    o_ref[...] = xs_ref[j] + gate * emb


def ple_update(x, yt, p_i, norm_g, w_gate, w_proj, tm=512, tn=512):
    m, d = x.shape
    pd = p_i.shape[1]
    return pl.pallas_call(
        _ple_kernel,
        grid=(m // tm, d // tn),
        in_specs=[
            pl.BlockSpec((tm, d), lambda i, j: (i, 0)),
            pl.BlockSpec((d, tm), lambda i, j: (0, i)),
            pl.BlockSpec((1, d), lambda i, j: (0, 0)),
            pl.BlockSpec((d, tn), lambda i, j: (0, j)),
            pl.BlockSpec((tm, pd), lambda i, j: (i, 0)),
            pl.BlockSpec((pd, tn), lambda i, j: (0, j)),
        ],
        out_specs=pl.BlockSpec((tm, tn), lambda i, j: (i, j)),
        out_shape=jax.ShapeDtypeStruct((m, d), F32),
        scratch_shapes=[pltpu.VMEM((tm, d), BF16), pltpu.VMEM((d // tn, tm, tn), F32)],
        compiler_params=_params("parallel", "arbitrary"),
        name="ple_update",
    )(x, yt, norm_g.reshape(1, d), w_gate, p_i, w_proj)


def _rmsnorm_kernel(x_ref, g_ref, o_ref):
    o_ref[...] = _rms(x_ref[...], g_ref[...])


def rmsnorm(x, gain, tm=512):
    m, d = x.shape
    return pl.pallas_call(
        _rmsnorm_kernel,
        grid=(m // tm,),
        in_specs=[pl.BlockSpec((tm, d), lambda i: (i, 0)), pl.BlockSpec((1, d), lambda i: (0, 0))],
        out_specs=pl.BlockSpec((tm, d), lambda i: (i, 0)),
        out_shape=jax.ShapeDtypeStruct((m, d), F32),
        compiler_params=_params("parallel"),
        name="rmsnorm",
    )(x, gain.reshape(1, d))


def _peer_fold_kernel(keys_ref, wq_ref, o_ref):
    o_ref[0, 0] = _hdot_nt(keys_ref[0, 0], wq_ref[...])


def peer_fold(w_q, keys):
    hk = PEER_QDIM // 2
    return pl.pallas_call(
        _peer_fold_kernel,
        grid=(2, PEER_HEADS),
        in_specs=[
            pl.BlockSpec((1, 1, PEER_NKEYS, hk), lambda c, h: (h, c, 0, 0)),
            pl.BlockSpec((D_MODEL, hk), lambda c, h: (0, h * 2 + c)),
        ],
        out_specs=pl.BlockSpec((1, 1, PEER_NKEYS, D_MODEL), lambda c, h: (c, h, 0, 0)),
        out_shape=jax.ShapeDtypeStruct((2, PEER_HEADS, PEER_NKEYS, D_MODEL), F32),
        compiler_params=_params("parallel", "parallel"),
        name="peer_fold",
    )(keys, w_q)


def _rmsnorm_t_kernel(x_ref, g_ref, o_ref):
    o_ref[...] = _rms(x_ref[...], g_ref[...]).T.astype(BF16)


def rmsnorm_t(x, gain, tm=512):
    m, d = x.shape
    return pl.pallas_call(
        _rmsnorm_t_kernel,
        grid=(m // tm,),
        in_specs=[pl.BlockSpec((tm, d), lambda i: (i, 0)), pl.BlockSpec((1, d), lambda i: (0, 0))],
        out_specs=pl.BlockSpec((d, tm), lambda i: (0, i)),
        out_shape=jax.ShapeDtypeStruct((d, m), BF16),
        compiler_params=_params("parallel"),
        name="rmsnorm_t",
    )(x, gain.reshape(1, d))


def _sort_desc(v):
    v = list(v)
    n = len(v)
    k = 2
    while k <= n:
        j = k // 2
        while j >= 1:
            for i in range(n):
                l = i ^ j
                if l > i:
                    hi, lo = jnp.maximum(v[i], v[l]), jnp.minimum(v[i], v[l])
                    v[i], v[l] = (hi, lo) if (i & k) == 0 else (lo, hi)
            j //= 2
        k *= 2
    return v


def _merge_top(a, b):
    n = len(a)
    v = [jnp.maximum(a[i], b[n - 1 - i]) for i in range(n)]
    j = n // 2
    while j >= 1:
        for i in range(n):
            l = i ^ j
            if l > i:
                v[i], v[l] = jnp.maximum(v[i], v[l]), jnp.minimum(v[i], v[l])
        j //= 2
    return v


def _top_sorted(vals, n):
    vals = list(vals)
    while len(vals) % n:
        vals.append(jnp.full_like(vals[0], -jnp.inf))
    acc = _sort_desc(vals[:n])
    for g in range(1, len(vals) // n):
        acc = _merge_top(acc, _sort_desc(vals[g * n:(g + 1) * n]))
    return acc


def _count_leading(pred, b):
    t0 = pred(b[15])
    t1 = pred(b[7])
    t2 = pred(jnp.where(t1, b[11], b[3]))
    t3 = pred(jnp.where(t1, jnp.where(t2, b[13], b[9]), jnp.where(t2, b[5], b[1])))
    hi = jnp.where(t2, jnp.where(t3, b[14], b[12]), jnp.where(t3, b[10], b[8]))
    lo = jnp.where(t2, jnp.where(t3, b[6], b[4]), jnp.where(t3, b[2], b[0]))
    t4 = pred(jnp.where(t1, hi, lo))
    cnt = (jnp.where(t1, 8.0, 0.0) + jnp.where(t2, 4.0, 0.0)) + (jnp.where(t3, 2.0, 0.0) + jnp.where(t4, 1.0, 0.0))
    return jnp.where(t0, 16.0, cnt)


_PEER_PAIRS = [(i, j) for i in range(PEER_TOPK) for j in range(PEER_TOPK) if (i + 1) * (j + 1) <= PEER_TOPK]


def _peer_select_kernel(wf_ref, ht_ref, e1_ref, n1_ref, r2_ref, e2_ref, sub_ref):
    nk, k = PEER_NKEYS, PEER_TOPK
    tt = ht_ref.shape[1]
    ht = ht_ref[...]
    for c in range(2):
        sub_ref[c] = jnp.dot(wf_ref[c], ht, preferred_element_type=F32)
    row = lax.broadcasted_iota(jnp.int32, (SUBLANES, LANES), 0)

    def head_row(v, h):
        return jnp.sum(jnp.where(row == h, v, 0.0), axis=0, keepdims=True)

    def lane_group(lg, carry):
        lanes = pl.ds(pl.multiple_of(lg * LANES, LANES), LANES)

        def top_of_head(h, packed, c):
            base = pl.multiple_of(h * nk, nk)
            slabs = [sub_ref[c, pl.ds(base + SUBLANES * j, SUBLANES), lanes] for j in range(nk // SUBLANES)]
            top = _sort_desc(slabs)
            for sh in (4, 2, 1):
                top = _merge_top(top, [pltpu.roll(t, sh, 0) for t in top])
            return tuple(jnp.where(row == h, top[i], packed[i]) for i in range(k))

        zero = tuple(jnp.zeros((SUBLANES, LANES), F32) for _ in range(k))
        a = lax.fori_loop(0, PEER_HEADS, functools.partial(top_of_head, c=0), zero)
        b = lax.fori_loop(0, PEER_HEADS, functools.partial(top_of_head, c=1), zero)
        best = _top_sorted([a[i] + b[j] for i, j in _PEER_PAIRS], k)
        thr, vmax = best[k - 1], best[0]
        z = jnp.zeros((SUBLANES, LANES), F32)
        for i in range(k):
            z = z + jnp.exp(best[i] - vmax)
        inv_z = 1.0 / z

        def emit(h, carry):
            base = pl.multiple_of(h * nk, nk)
            s1 = sub_ref[0, pl.ds(base, nk), lanes]
            s2 = sub_ref[1, pl.ds(base, nk), lanes]
            thr_h = head_row(thr, h)
            bh = [head_row(b[j], h) for j in range(k)]
            n1 = _count_leading(lambda x: s1 + x >= thr_h, bh)
            r2 = _count_leading(lambda x: x > s2, bh)
            e1_ref[h, :, lanes] = jnp.exp(s1 - head_row(a[0], h)) * head_row(inv_z, h)
            n1_ref[h, :, lanes] = n1
            r2_ref[h, :, lanes] = r2.astype(BF16)
            e2_ref[h, :, lanes] = jnp.exp(s2 - head_row(b[0], h)).astype(BF16)
            return carry

        lax.fori_loop(0, PEER_HEADS, emit, 0)
        return carry

    lax.fori_loop(0, tt // LANES, lane_group, 0)


def peer_select(wf, ht, tt=256):
    d, s = ht.shape
    nrow = PEER_HEADS * PEER_NKEYS
    shape = (PEER_HEADS, PEER_NKEYS, s)
    ospec = pl.BlockSpec((PEER_HEADS, PEER_NKEYS, tt), lambda i: (0, 0, i))
    return pl.pallas_call(
        _peer_select_kernel,
        grid=(s // tt,),
        in_specs=[pl.BlockSpec((2, nrow, d), lambda i: (0, 0, 0)), pl.BlockSpec((d, tt), lambda i: (0, i))],
        out_specs=[ospec] * 4,
        out_shape=[jax.ShapeDtypeStruct(shape, F32)] * 2 + [jax.ShapeDtypeStruct(shape, BF16)] * 2,
        scratch_shapes=[pltpu.VMEM((2, nrow, tt), F32)],
        compiler_params=_params("parallel"),
        name="peer_select",
    )(wf, ht)


def _gelu_tanh(x):
    return 0.5 * x * (1.0 + jnp.tanh(math.sqrt(2.0 / math.pi) * (x + 0.044715 * (x * x * x))))


def _peer_dense_kernel(u_ref, ht_ref, vt_ref, e1_ref, n1_ref, r2_ref, e2_ref, o_ref, act0_ref, act1_ref, ga_ref):
    nk = PEER_NKEYS
    j = pl.program_id(1)
    last = pl.num_programs(1) - 1
    eb = u_ref.shape[0]
    nblk = eb // nk
    done = jnp.maximum(j - 1, 0)

    def step(prev_ref, next_ref):
        if prev_ref is not None:
            for ii in range(nblk):
                i1 = done * nblk + ii
                act = _gelu_tanh(prev_ref[pl.ds(ii * nk, nk), :].astype(BF16))
                gate = None
                for h in range(PEER_HEADS):
                    n1 = n1_ref[h, pl.ds(i1, 1), :].astype(BF16)
                    e1 = e1_ref[h, pl.ds(i1, 1), :].astype(BF16)
                    g = jnp.where(r2_ref[h] < n1, e2_ref[h] * e1, jnp.zeros((), BF16))
                    gate = g if gate is None else gate + g
                ga_ref[pl.ds(ii * nk, nk), :] = gate * act
        if next_ref is not None:
            next_ref[...] = jnp.dot(u_ref[...], ht_ref[...], preferred_element_type=F32)
        if prev_ref is not None:
            o_ref[...] += jnp.dot(vt_ref[...], ga_ref[...], preferred_element_type=F32)
        else:
            o_ref[...] = jnp.zeros_like(o_ref)

    pl.when(j == 0)(functools.partial(step, None, act0_ref))
    pl.when(j % 2 == 1)(functools.partial(step, act0_ref, act1_ref))
    pl.when((j % 2 == 0) & (j > 0) & (j < last))(functools.partial(step, act1_ref, act0_ref))
    pl.when(j == last)(functools.partial(step, act1_ref, None))


PEER_EB = 1024


def peer_dense(u_all, ht, vt_all, layer, e1, n1, r2, e2, tt=512):
    d, s = ht.shape
    ne, eb = u_all.shape[1], PEER_EB
    assert (ne // eb) % 2 == 0
    last = ne // eb - 1
    gspec = pl.BlockSpec((PEER_HEADS, PEER_NKEYS, tt), lambda i, j: (0, 0, i))
    return pl.pallas_call(
        _peer_dense_kernel,
        grid=(s // tt, ne // eb + 1),
        in_specs=[
            pl.BlockSpec((None, eb, d), lambda i, j: (layer, jnp.minimum(j, last), 0)),
            pl.BlockSpec((d, tt), lambda i, j: (0, i)),
            pl.BlockSpec((None, None, d, eb), lambda i, j: (layer, jnp.maximum(j - 1, 0), 0, 0)),
            gspec, gspec, gspec, gspec,
        ],
        out_specs=pl.BlockSpec((d, tt), lambda i, j: (0, i)),
        out_shape=jax.ShapeDtypeStruct((d, s), F32),
        scratch_shapes=[pltpu.VMEM((eb, tt), F32), pltpu.VMEM((eb, tt), F32), pltpu.VMEM((eb, tt), BF16)],
        compiler_params=_params("parallel", "arbitrary"),
        name="peer_dense",
    )(u_all, ht, vt_all, e1, n1, r2, e2)


def prep_peer_tables(peer_u, peer_v):
    nl, ne, d = peer_v.shape
    vt_all = peer_v.astype(BF16).reshape(nl, ne // PEER_EB, PEER_EB, d).transpose(0, 1, 3, 2)
    return peer_u.astype(BF16), vt_all


def peer(x, ln_g, w_q, keys, u_all, vt_all, layer, tm=512, tt=256):
    wf = peer_fold(w_q, keys).reshape(2, PEER_HEADS * PEER_NKEYS, D_MODEL).astype(BF16)
    ht = rmsnorm_t(x, ln_g, tm=tm)
    return peer_dense(u_all, ht, vt_all, layer, *peer_select(wf, ht, tt=tt), tt=tm)


MLA_HEAD_GROUP = 4
MLA_QUERY_STRIP = 512


def _mla_kernel(qi_ref, ki_ref, qn_ref, qr_ref, qrot_ref, cq_ref, sq_ref, kn_ref, kr_ref, krot_ref, ck_ref, sk_ref,
                vt_ref, o_ref, q1_ref, q2_ref, m_ref, l_ref, acc_ref, *, scale):
    t = pl.program_id(1)
    qi, ki = qi_ref[t], ki_ref[t]
    tq, tk = qn_ref.shape[1], kn_ref.shape[0]
    ratio = tq // tk
    dn, dr, dv = MLA_NOPE, MLA_ROPE, MLA_V
    heads = range(qn_ref.shape[0] // dn)

    @pl.when(ki == 0)
    def _():
        q1_ref[...] = (qn_ref[...] * scale).astype(BF16)
        for h in heads:
            rows = pl.ds(h * dr, dr)
            q2_ref[rows, :] = ((qr_ref[rows, :] * cq_ref[...] + qrot_ref[rows, :] * sq_ref[...]) * scale).astype(BF16)
        m_ref[...] = jnp.full_like(m_ref, -jnp.inf)
        l_ref[...] = jnp.zeros_like(l_ref)
        acc_ref[...] = jnp.zeros_like(acc_ref)

    def step(masked):
        kr = (kr_ref[...] * ck_ref[...] + krot_ref[...] * sk_ref[...]).astype(BF16)
        sw = min(MLA_QUERY_STRIP, tq)
        for st in range(tq // sw):
            cols = pl.ds(st * sw, sw)
            s = [jnp.dot(kn_ref[:, h * dn:(h + 1) * dn], q1_ref[h * dn:(h + 1) * dn, cols], preferred_element_type=F32)
                 + jnp.dot(kr, q2_ref[h * dr:(h + 1) * dr, cols], preferred_element_type=F32) for h in heads]
            if masked:
                kpos = ki * tk + lax.broadcasted_iota(jnp.int32, (tk, sw), 0)
                qpos = qi * tq + st * sw + lax.broadcasted_iota(jnp.int32, (tk, sw), 1)
                keep = kpos <= qpos
                s = [jnp.where(keep, x, -jnp.inf) for x in s]
            m_old = [m_ref[h, :, cols] for h in heads]
            m_new = [jnp.maximum(a, jnp.max(x, axis=0, keepdims=True)) for a, x in zip(m_old, s)]
            alpha = [jnp.exp(a - b) for a, b in zip(m_old, m_new)]
            p = [jnp.exp(x - b) for x, b in zip(s, m_new)]
            for h in heads:
                rows = pl.ds(h * dv, dv)
                l_ref[h, :, cols] = alpha[h] * l_ref[h, :, cols] + jnp.sum(p[h], axis=0, keepdims=True)
                acc_ref[rows, cols] = alpha[h] * acc_ref[rows, cols] + jnp.dot(vt_ref[rows, :], p[h].astype(BF16),
                                                                                 preferred_element_type=F32)
                m_ref[h, :, cols] = m_new[h]

    pl.when(ki < qi * ratio)(functools.partial(step, False))
    pl.when(ki >= qi * ratio)(functools.partial(step, True))

    @pl.when(ki == (qi + 1) * ratio - 1)
    def _():
        for h in heads:
            rows = pl.ds(h * dv, dv)
            o_ref[rows, :] = acc_ref[rows, :] / l_ref[h]


def mla_attention(qt, kv, vt, kr, krot, cos, sin, cos_t, sin_t, tq=1024, tk=512):
    s = kv.shape[0]
    tq, tk = min(tq, s), min(tk, s)
    ratio = tq // tk
    hh, dn, dr = MLA_HEADS, MLA_NOPE, MLA_ROPE
    pairs = [(qi, ki) for qi in range(s // tq) for ki in range((qi + 1) * ratio)]
    qi_tab = jnp.array([pr[0] for pr in pairs], jnp.int32)
    ki_tab = jnp.array([pr[1] for pr in pairs], jnp.int32)
    g = MLA_HEAD_GROUP
    r0 = hh * dn // (g * dr)
    qmap = lambda h, t, qt_, kt_: (0, qt_[t])
    kmap = lambda h, t, qt_, kt_: (kt_[t], 0)
    grid_spec = pltpu.PrefetchScalarGridSpec(
        num_scalar_prefetch=2,
        grid=(hh // g, len(pairs)),
        in_specs=[
            pl.BlockSpec((g * dn, tq), lambda h, t, qt_, kt_: (h, qt_[t])),
            pl.BlockSpec((g * dr, tq), lambda h, t, qt_, kt_: (r0 + h, qt_[t])),
            pl.BlockSpec((g * dr, tq), lambda h, t, qt_, kt_: (r0 + hh // g + h, qt_[t])),
            pl.BlockSpec((dr, tq), qmap),
            pl.BlockSpec((dr, tq), qmap),
            pl.BlockSpec((tk, g * dn), lambda h, t, qt_, kt_: (kt_[t], h)),
            pl.BlockSpec((tk, dr), kmap),
            pl.BlockSpec((tk, dr), kmap),
            pl.BlockSpec((tk, dr), kmap),
            pl.BlockSpec((tk, dr), kmap),
            pl.BlockSpec((g * MLA_V, tk), lambda h, t, qt_, kt_: (h, kt_[t])),
        ],
        out_specs=pl.BlockSpec((g * MLA_V, tq), lambda h, t, qt_, kt_: (h, qt_[t])),
        scratch_shapes=[pltpu.VMEM((g * dn, tq), BF16), pltpu.VMEM((g * dr, tq), BF16), pltpu.VMEM((g, 1, tq), F32),
                        pltpu.VMEM((g, 1, tq), F32), pltpu.VMEM((g * MLA_V, tq), F32)],
    )
    return pl.pallas_call(
        functools.partial(_mla_kernel, scale=(MLA_NOPE + MLA_ROPE) ** -0.5),
        grid_spec=grid_spec,
        out_shape=jax.ShapeDtypeStruct((hh * MLA_V, s), F32),
        compiler_params=_params("parallel", "arbitrary"),
        name="mla_attention",
    )(qi_tab, ki_tab, qt, qt, qt, cos_t, sin_t, kv, kr, krot, cos, sin, vt)


def _causal_conv(cur_ref, prev_ref, w_ref, first, bias=None):
    prev = jnp.where(first, 0.0, prev_ref[...])
    xe = jnp.concatenate([prev, cur_ref[...]], axis=0)
    w = w_ref[...]
    acc = w[CONV_WIDTH - 1:CONV_WIDTH] * xe[SUBLANES:]
    for j in range(CONV_WIDTH - 1):
        acc = acc + w[j:j + 1] * pltpu.roll(xe, CONV_WIDTH - 1 - j, 0)[SUBLANES:]
    return acc if bias is None else acc + bias


def _unit_lower_inverses(lows, c):
    n = lows[0].shape[0]
    r = lax.broadcasted_iota(jnp.int32, (n, n), 0)
    q = lax.broadcasted_iota(jnp.int32, (n, n), 1)
    eye = (r == q).astype(F32)
    prev = [jnp.where((r // 16) == (q // 16), low, 0.0) for low in lows]
    ps = [-d for d in prev]
    xs = [eye + p for p in ps]
    for _ in range(3):
        ps = [_bdot(p, p) for p in ps]
        xs = [x + _bdot(x, p) for x, p in zip(xs, ps)]
    size = 32
    while size <= c:
        cur = [jnp.where((r // size) == (q // size), low, 0.0) for low in lows] if size < c else lows
        ts = [_bdot(x, cu - pr) for x, cu, pr in zip(xs, cur, prev)]
        xs = [x - _bdot(t, x) for x, t in zip(xs, ts)]
        prev = cur
        size *= 2
    return xs


GDN_HEAD_GROUP = 4


def _gdn_kernel(q_ref, qp_ref, k_ref, kp_ref, v_ref, vp_ref, z_ref, wq_ref, wk_ref, wv_ref,
                ar_ref, br_ref, alog_ref, dtb_ref, ng_ref, o_ref, st_ref):
    c = GDN_CHUNK
    tt = q_ref.shape[0]
    heads = range(q_ref.shape[1] // GDN_DK)
    first = pl.program_id(1) == 0

    @pl.when(first)
    def _():
        st_ref[...] = jnp.zeros_like(st_ref)

    def l2n(x):
        return x * lax.rsqrt(jnp.sum(x * x, axis=-1, keepdims=True) + 1e-6)

    lanes = lambda x, h: x[:, h * GDN_DK:(h + 1) * GDN_DK]
    q_all = _silu(_causal_conv(q_ref, qp_ref, wq_ref, first))
    k_all = _silu(_causal_conv(k_ref, kp_ref, wk_ref, first))
    v_all = _silu(_causal_conv(v_ref, vp_ref, wv_ref, first))
    q = [l2n(lanes(q_all, h)) * (GDN_DK ** -0.5) for h in heads]
    k = [l2n(lanes(k_all, h)) for h in heads]
    v = [lanes(v_all, h) for h in heads]
    neg_a = [-jnp.exp(alog_ref[h, :, 0:1]) for h in heads]
    dtb = [dtb_ref[h, :, 0:1] for h in heads]
    nb = 2 * c
    r = lax.broadcasted_iota(jnp.int32, (nb, nb), 0)
    cc = lax.broadcasted_iota(jnp.int32, (nb, nb), 1)
    same = (r // c) == (cc // c)
    incl, strict = same & (r >= cc), same & (r > cc)
    incl_t = same & (r <= cc)
    top = lax.broadcasted_iota(jnp.int32, (nb, 1), 0) < c
    nblocks = tt // nb
    items = [(n, h) for n in range(nblocks) for h in heads]
    sl = lambda n: slice(n * nb, (n + 1) * nb)
    qs = [q[h][sl(n)] for n, h in items]
    ks = [k[h][sl(n)] for n, h in items]
    vs = [v[h][sl(n)] for n, h in items]
    as_col = lambda row: jnp.sum(jnp.where(r == cc, row, 0.0), axis=1, keepdims=True)
    g_row = [neg_a[h] * _softplus(ar_ref[h, n] + dtb[h]) for n, h in items]
    g_col = [as_col(g) for g in g_row]
    beta = [as_col(_sigmoid(br_ref[h, n])) for n, h in items]
    gc = [jnp.sum(jnp.where(incl, g, 0.0), axis=1, keepdims=True) for g in g_row]
    gr = [jnp.sum(jnp.where(incl_t, g, 0.0), axis=0, keepdims=True) for g in g_col]
    decay = [jnp.where(incl, jnp.exp(jnp.where(incl, a - b, 0.0)), 0.0) for a, b in zip(gc, gr)]
    kk = [_bdot_nt(x, x) for x in ks]
    qk = [_bdot_nt(a, b) for a, b in zip(qs, ks)]
    inv = _unit_lower_inverses([jnp.where(strict, b * m * d, 0.0) for b, m, d in zip(beta, kk, decay)], c)
    eg = [jnp.exp(g) for g in gc]
    sol = [_bdot(x, jnp.concatenate([b * vv, (b * e) * kx], axis=1)) for x, b, e, vv, kx in zip(inv, beta, eg, vs, ks)]
    a_qk = [m * d for m, d in zip(qk, decay)]
    k_end = [kx * jnp.exp(jnp.where(top, g[c - 1:c], g[nb - 1:nb]) - g) for g, kx in zip(gc, ks)]
    k_end_t = [[ke[:c].T, ke[c:].T] for ke in k_end]
    q_dec = [a * e for a, e in zip(qs, eg)]
    state = [st_ref[h] for h in heads]
    us = [[None, None] for _ in items]
    o_state = [[None, None] for _ in items]
    for n in range(nblocks):
        for j in range(2):
            cs = slice(j * c, (j + 1) * c)
            for h in heads:
                it = n * len(heads) + h
                u = sol[it][cs, :GDN_DV] - _bdot(sol[it][cs, GDN_DV:], state[h])
                o_state[it][j] = _bdot(q_dec[it][cs], state[h])
                state[h] = jnp.exp(gc[it][(j + 1) * c - 1:(j + 1) * c]) * state[h] + _bdot(k_end_t[it][j], u)
                us[it][j] = u
    for h in heads:
        st_ref[h] = state[h]
    o_blk = [jnp.concatenate(o_state[it], axis=0) + _bdot(a_qk[it], jnp.concatenate(us[it], axis=0))
             for it in range(len(items))]
    o = jnp.concatenate([_rms(jnp.concatenate([o_blk[n * len(heads) + h] for n in range(nblocks)], axis=0), ng_ref[...])
                         for h in heads], axis=1)
    o_ref[...] = o * _silu(z_ref[...])


def gated_delta_net(proj, col0, a_raw, b_raw, conv_w, a_log, dt_bias, norm_g, tt=256):
    s = proj.shape[0]
    tt = min(tt, s)
    hh, c, g = GDN_HEADS, GDN_CHUNK, GDN_HEAD_GROUP
    w = g * GDN_DK
    assert col0 % w == 0
    b0 = col0 // w
    nblk = GDN_HEADS * GDN_DK // w

    def cur(grp):
        return pl.BlockSpec((tt, w), lambda h, i: (i, b0 + grp * nblk + h))

    def prev(grp):
        return pl.BlockSpec((SUBLANES, w), lambda h, i: (jnp.maximum(i * (tt // SUBLANES) - 1, 0), b0 + grp * nblk + h))

    def wspec(grp):
        return pl.BlockSpec((CONV_WIDTH, w), lambda h, i: (0, grp * nblk + h))

    a_t, b_t = a_raw.T, b_raw.T
    c = 2 * c
    rowspec = pl.BlockSpec((g, tt // c, 1, c), lambda h, i: (h, i, 0, 0))
    hspec = pl.BlockSpec((g, 1, LANES), lambda h, i: (h, 0, 0))
    bcast = lambda p: jnp.broadcast_to(p.astype(F32)[:, None, None], (hh, 1, LANES))
    return pl.pallas_call(
        _gdn_kernel,
        grid=(hh // g, s // tt),
        in_specs=[cur(0), prev(0), cur(1), prev(1), cur(2), prev(2), cur(3), wspec(0), wspec(1), wspec(2),
                  rowspec, rowspec, hspec, hspec, pl.BlockSpec((1, GDN_DV), lambda h, i: (0, 0))],
        out_specs=pl.BlockSpec((tt, w), lambda h, i: (i, h)),
        out_shape=jax.ShapeDtypeStruct((s, hh * GDN_DV), F32),
        scratch_shapes=[pltpu.VMEM((g, GDN_DK, GDN_DV), F32)],
        compiler_params=_params("parallel", "arbitrary"),
        name="gated_delta_net",
    )(proj, proj, proj, proj, proj, proj, proj, conv_w, conv_w, conv_w,
      a_t.reshape(hh, s // c, 1, c), b_t.reshape(hh, s // c, 1, c),
      bcast(a_log), bcast(dt_bias), norm_g.reshape(1, GDN_DV).astype(F32))


def _rot_half_cols(w, half):
    return jnp.concatenate([-w[..., half:], w[..., :half]], axis=-1)


def _pad_cols(w, n):
    return jnp.pad(w, ((0, 0), (0, n - w.shape[1])))


AB_QKVZ = 0
AB_CQ = 4 * GDN_HEADS * GDN_DK
AB_CKV = AB_CQ + MLA_Q_RANK
AB_KR = AB_CKV + MLA_KV_RANK
AB_GATES = AB_KR + 2 * MLA_ROPE
AB_PAD = AB_GATES + LANES


def prep_ab(w_in, w_uq, w_ukv, w_out):
    rq, rkv, rr = MLA_Q_RANK, MLA_KV_RANK, MLA_ROPE
    w_kr = w_in[:, rq + rkv:rq + rkv + rr]
    gdn_w = 4 * GDN_HEADS * GDN_DK
    g0 = rq + rkv + rr
    w_main = jnp.concatenate([w_in[:, g0:g0 + gdn_w], w_in[:, :rq + rkv], w_kr, _rot_half_cols(w_kr, rr // 2),
                              w_in[:, g0 + gdn_w:]], axis=1)
    w_main = _pad_cols(w_main, AB_PAD).astype(BF16)
    uq = w_uq.reshape(rq, MLA_HEADS, MLA_NOPE + MLA_ROPE)
    uq_r = uq[..., MLA_NOPE:]
    uq2 = jnp.concatenate([uq[..., :MLA_NOPE].reshape(rq, -1), uq_r.reshape(rq, -1),
                           _rot_half_cols(uq_r, rr // 2).reshape(rq, -1)], axis=1).astype(BF16)
    ukv = w_ukv.reshape(rkv, MLA_HEADS, MLA_NOPE + MLA_V)
    ukv2 = jnp.concatenate([ukv[..., :MLA_NOPE].reshape(rkv, -1), ukv[..., MLA_NOPE:].reshape(rkv, -1)], axis=1).astype(BF16)
    return w_main, uq2, ukv2, w_out.astype(BF16)


def mixer_ab(x, ln, w_main, uq2, ukv2, w_out, q_norm, kv_norm, conv_w, a_log, dt_bias, gdn_norm, cos2, sin2,
             tm=512, t_attn=512, t_gdn=256):
    s = x.shape[0]
    rq, rkv, rr = MLA_Q_RANK, MLA_KV_RANK, MLA_ROPE
    proj = norm_matmul(x, w_main, gain=ln, tm=min(2 * tm, x.shape[0]), tn=1024)
    qfull = norm_matmul(proj, uq2, gain=q_norm, tm=tm, x_col=AB_CQ)
    kv = norm_matmul(proj, ukv2, gain=kv_norm, tm=tm, out_dtype=BF16, x_col=AB_CKV)
    nn = MLA_HEADS * MLA_NOPE
    o_a = mla_attention(qfull.T, kv, kv[:, nn:].T, proj[:, AB_KR:AB_KR + rr], proj[:, AB_KR + rr:AB_GATES],
                        cos2, sin2, cos2.T, sin2.T, tq=2 * t_attn, tk=t_attn).T
    o_b = gated_delta_net(proj, AB_QKVZ, proj[:, AB_GATES:AB_GATES + GDN_HEADS],
                          proj[:, AB_GATES + GDN_HEADS:AB_GATES + 2 * GDN_HEADS], conv_w, a_log, dt_bias, gdn_norm, tt=t_gdn)
    return out_proj(o_a, o_b, w_out, x, tm=min(2 * tm, x.shape[0]), tn=1024)


CD_R, CD_K, CD_V, CD_Z, CD_DT, CD_X = (i * 1024 for i in range(6))
CD_B = 6144
CD_C = CD_B + SSD_GROUPS * SSD_STATE
CD_WA = CD_C + SSD_GROUPS * SSD_STATE
CD_G = CD_WA + LANES
CD_PAD = 7168
GROUP_W = SSD_INNER // SSD_GROUPS


def _lower_ones(c):
    r = lax.broadcasted_iota(jnp.int32, (c, c), 0)
    q = lax.broadcasted_iota(jnp.int32, (c, c), 1)
    return r >= q, r > q


def _ssd_kernel(x_ref, xp_ref, b_ref, bp_ref, c_ref, cp_ref, z_ref, dt_ref, wx_ref, wb_ref, wc_ref,
                bx_ref, bb_ref, bc_ref, dtb_ref, alog_ref, dskip_ref, ng_ref, o_ref, st_ref):
    c = SSD_CHUNK
    tt = x_ref.shape[0]
    first = pl.program_id(1) == 0

    @pl.when(first)
    def _():
        st_ref[...] = jnp.zeros_like(st_ref)

    xs_all = _silu(_causal_conv(x_ref, xp_ref, wx_ref, first, bx_ref[...]))
    bm_all = _silu(_causal_conv(b_ref, bp_ref, wb_ref, first, bb_ref[...]))
    cm_all = _silu(_causal_conv(c_ref, cp_ref, wc_ref, first, bc_ref[...]))
    dt_all = _softplus(dt_ref[...] + dtb_ref[...])
    a_all = -jnp.exp(alog_ref[...]) * dt_all
    incl, _ = _lower_ones(c)
    tri = incl.astype(F32)
    left = lax.broadcasted_iota(jnp.int32, (c, LANES), 1) < SSD_HEADDIM
    npair = GROUP_W // LANES
    outs = []
    for n in range(tt // c):
        sl = slice(n * c, (n + 1) * c)
        xs, bm, cm, dt = xs_all[sl], bm_all[sl], cm_all[sl], dt_all[sl]
        acs = _bdot(tri, a_all[sl])
        xdt = xs * dt
        cb = _bdot_nt(cm, bm)
        bm_t = bm.T
        ys = []
        for p in range(npair):
            ls = slice(p * LANES, (p + 1) * LANES)
            acs_p = acs[:, ls]
            acs_t = acs_p.T
            xp = xdt[:, ls]
            yd = []
            for hd in range(2):
                col = acs_p[:, hd * SSD_HEADDIM:hd * SSD_HEADDIM + 1]
                row = acs_t[hd * SSD_HEADDIM:hd * SSD_HEADDIM + 1, :]
                lmat = jnp.where(incl, jnp.exp(jnp.where(incl, col - row, 0.0)), 0.0)
                yd.append(_bdot(cb * lmat, xp))
            last = acs_p[c - 1:c]
            prev_t = st_ref[p]
            y_off = _bdot(cm, prev_t) * jnp.exp(acs_p)
            st_ref[p] = jnp.exp(last) * prev_t + _bdot(bm_t, xp * jnp.exp(last - acs_p))
            ys.append(jnp.where(left, yd[0], yd[1]) + y_off)
        outs.append(jnp.concatenate(ys, axis=1) + xs * dskip_ref[...])
    y = jnp.concatenate(outs, axis=0) * _silu(z_ref[...])
    o_ref[...] = _rms(y, ng_ref[...])


def mamba2_ssd(proj, conv_w, conv_b, dt_bias, a_log, d_skip, norm_g, tt=256):
    s = proj.shape[0]
    tt = min(tt, s)
    gw, ns = GROUP_W, SSD_STATE
    per = lambda v: jnp.repeat(v.astype(F32), SSD_HEADDIM).reshape(1, SSD_INNER)

    def cur(col, w):
        return pl.BlockSpec((tt, w), lambda g, i: (i, col // w + g))

    def prev(col, w):
        return pl.BlockSpec((SUBLANES, w), lambda g, i: (jnp.maximum(i * (tt // SUBLANES) - 1, 0), col // w + g))

    def par(rows, col, w):
        return pl.BlockSpec((rows, w), lambda g, i: (0, col // w + g))

    cb = conv_b.reshape(1, -1).astype(F32)
    return pl.pallas_call(
        _ssd_kernel,
        grid=(SSD_GROUPS, s // tt),
        in_specs=[cur(CD_X, gw), prev(CD_X, gw), cur(CD_B, ns), prev(CD_B, ns), cur(CD_C, ns), prev(CD_C, ns),
                  cur(CD_Z, gw), cur(CD_DT, gw),
                  par(CONV_WIDTH, 0, gw), par(CONV_WIDTH, SSD_INNER, ns), par(CONV_WIDTH, SSD_INNER + SSD_GROUPS * ns, ns),
                  par(1, 0, gw), par(1, SSD_INNER, ns), par(1, SSD_INNER + SSD_GROUPS * ns, ns),
                  par(1, 0, gw), par(1, 0, gw), par(1, 0, gw), par(1, 0, gw)],
        out_specs=pl.BlockSpec((tt, gw), lambda g, i: (i, g)),
        out_shape=jax.ShapeDtypeStruct((s, SSD_INNER), F32),
        scratch_shapes=[pltpu.VMEM((gw // LANES, ns, LANES), F32)],
        compiler_params=_params("parallel", "arbitrary"),
        name="mamba2_ssd",
    )(proj, proj, proj, proj, proj, proj, proj, proj, conv_w, conv_w, conv_w, cb, cb, cb,
      per(dt_bias), per(a_log), per(d_skip), norm_g.reshape(1, SSD_INNER).astype(F32))


def _pair_ones():
    r = lax.broadcasted_iota(jnp.int32, (LANES, LANES), 0)
    q = lax.broadcasted_iota(jnp.int32, (LANES, LANES), 1)
    return (r // RWKV_HEAD) == (q // RWKV_HEAD)


def _head_sums(x, ones):
    return jnp.concatenate([_hdot(x[:, i * LANES:(i + 1) * LANES], ones) for i in range(x.shape[1] // LANES)], axis=1)


def _rwkv_prep_kernel(r_ref, rp_ref, k_ref, kp_ref, v_ref, vp_ref, wa_ref, wap_ref, g0_ref, g0p_ref, g1_ref, g1p_ref,
                      mur_ref, muk_ref, muv_ref, muwa_ref, mug0_ref, mug1_ref, w0_ref, w2_ref, a0_ref, a2_ref, g2_ref,
                      kk_ref, ka_ref, rk_ref,
                      ro_ref, lw_ref, ko_ref, vo_ref, po_ref, qo_ref, go_ref, bo_ref):
    first = pl.program_id(0) == 0

    def mix(cur_ref, prev_ref, mu_ref):
        cur = cur_ref[...]
        prev = jnp.where(first, 0.0, prev_ref[...])
        shifted = pltpu.roll(jnp.concatenate([prev, cur], axis=0), 1, 0)[SUBLANES:]
        return cur + (shifted - cur) * mu_ref[...]

    r = mix(r_ref, rp_ref, mur_ref)
    k = mix(k_ref, kp_ref, muk_ref)
    v = mix(v_ref, vp_ref, muv_ref)
    wa = mix(wa_ref, wap_ref, muwa_ref)
    g0 = mix(g0_ref, g0p_ref, mug0_ref)
    g1 = mix(g1_ref, g1p_ref, mug1_ref)
    log_w = -math.exp(-0.5) * _sigmoid(w0_ref[...] + _bdot(jnp.tanh(wa), w2_ref[...]))
    a = _sigmoid(a0_ref[...] + _bdot(wa, a2_ref[...]))
    gate = _bdot(_sigmoid(g0), g2_ref[0:LANES, :]) + _bdot(_sigmoid(g1), g2_ref[LANES:2 * LANES, :])
    ones = _pair_ones().astype(F32)
    kx = k * kk_ref[...]
    kk = kx * lax.rsqrt(_head_sums(kx * kx, ones) + 1e-6)
    k_mod = k * (1.0 + (a - 1.0) * ka_ref[...])
    ro_ref[...] = r
    lw_ref[...] = log_w
    ko_ref[...] = k_mod
    vo_ref[...] = v
    po_ref[...] = -kk * a
    qo_ref[...] = kk
    go_ref[...] = gate
    bo_ref[...] = _head_sums(r * k_mod * rk_ref[...], ones) * v


def rwkv_prep(proj, mu, w0, w2, a0, a2, g2, k_k, k_a, r_k, tt=256):
    s = proj.shape[0]
    tt = min(tt, s)
    ri = RWKV_INNER
    row = lambda v: v.reshape(1, -1).astype(F32)
    mu_r, mu_k, mu_v = (row(mu[i * ri:(i + 1) * ri]) for i in range(3))
    mu_wa = row(mu[3 * ri:3 * ri + LANES])
    mu_g = row(jnp.pad(mu[3 * ri + LANES:], (0, 2 * LANES - RWKV_G_LORA)))
    zeros = jnp.zeros((RWKV_W_LORA, ri), F32)
    w2p = jnp.concatenate([w2, zeros], axis=0).astype(BF16)
    a2p = jnp.concatenate([zeros, a2], axis=0).astype(BF16)
    g2p = jnp.pad(g2, ((0, 2 * LANES - RWKV_G_LORA), (0, 0))).astype(BF16)

    def cur(col, w):
        return pl.BlockSpec((tt, w), lambda i: (i, col // w))

    def prev(col, w):
        return pl.BlockSpec((SUBLANES, w), lambda i: (jnp.maximum(i * (tt // SUBLANES) - 1, 0), col // w))

    full = lambda a: pl.BlockSpec(a.shape, lambda i: (0, 0))
    params = [mu_r, mu_k, mu_v, mu_wa, mu_g[:, :LANES], mu_g[:, LANES:], row(w0), w2p, row(a0), a2p, g2p,
              row(k_k), row(k_a), row(r_k)]
    out = jax.ShapeDtypeStruct((s, ri), F32)
    return pl.pallas_call(
        _rwkv_prep_kernel,
        grid=(s // tt,),
        in_specs=[cur(CD_R, ri), prev(CD_R, ri), cur(CD_K, ri), prev(CD_K, ri), cur(CD_V, ri), prev(CD_V, ri),
                  cur(CD_WA, LANES), prev(CD_WA, LANES), cur(CD_G, LANES), prev(CD_G, LANES),
                  cur(CD_G + LANES, LANES), prev(CD_G + LANES, LANES)] + [full(a) for a in params],
        out_specs=[pl.BlockSpec((tt, ri), lambda i: (i, 0))] * 8,
        out_shape=[out] * 8,
        compiler_params=_params("arbitrary"),
        name="rwkv_prep",
    )(*([proj] * 12), *params)


RWKV_MY_CHUNK = 64


def _rwkv_scan_kernel(r_ref, lw_ref, k_ref, v_ref, p_ref, q_ref, g_ref, b_ref, lnw_ref, lnb_ref, o_ref, st_ref):
    c = RWKV_MY_CHUNK
    tt = r_ref.shape[0]

    @pl.when(pl.program_id(1) == 0)
    def _():
        st_ref[...] = jnp.zeros_like(st_ref)

    tri = _lower_ones(c)[0].astype(F32)
    left = lax.broadcasted_iota(jnp.int32, (c, LANES), 1) < RWKV_HEAD
    pair = _pair_ones()
    ones = pair.astype(F32)
    r_id = lax.broadcasted_iota(jnp.int32, (LANES, LANES), 0)
    c_id = lax.broadcasted_iota(jnp.int32, (LANES, LANES), 1)
    eye = r_id == c_id
    top = r_id < c
    strict = pair & ((r_id % c) > (c_id % c))
    incl = pair & ((r_id % c) >= (c_id % c))
    stack2 = lambda a: jnp.concatenate([a, a], axis=0)
    by_head = lambda a: jnp.concatenate([jnp.where(left, a, 0.0), jnp.where(left, 0.0, a)], axis=0)
    unstack = lambda a: jnp.where(left, a[:c], a[c:])
    pairs = range(r_ref.shape[1] // LANES)
    nchunks = tt // c
    items = [(n, pr) for n in range(nchunks) for pr in pairs]
    blk = lambda ref, n, pr: ref[n * c:(n + 1) * c, pr * LANES:(pr + 1) * LANES]
    w = [blk(lw_ref, n, pr) for n, pr in items]
    v = [blk(v_ref, n, pr) for n, pr in items]
    lw = [_bdot(tri, x) for x in w]
    lam_in = [jnp.exp(x) for x in lw]
    inv_lam = [jnp.exp(-x) for x in lw]
    q_bar = [blk(q_ref, n, pr) * jnp.exp(a - b) for (n, pr), a, b in zip(items, lw, w)]
    r_bar = [blk(r_ref, n, pr) * x for (n, pr), x in zip(items, lam_in)]
    pk = [jnp.concatenate([blk(p_ref, n, pr) * x, blk(k_ref, n, pr) * x], axis=0) for (n, pr), x in zip(items, inv_lam)]
    mq = [_bdot_nt(by_head(a), b) for a, b in zip(q_bar, pk)]
    mr = [_bdot_nt(by_head(a), b) for a, b in zip(r_bar, pk)]
    mq_sw = [pltpu.roll(x, c, 1) for x in mq]
    mr_sw = [pltpu.roll(x, c, 1) for x in mr]
    m_qp = [jnp.where(strict, jnp.where(top, a, b), 0.0) for a, b in zip(mq, mq_sw)]
    m_qk = [jnp.where(strict, jnp.where(top, b, a), 0.0) for a, b in zip(mq, mq_sw)]
    m_rp = [jnp.where(incl, jnp.where(top, a, b), 0.0) for a, b in zip(mr, mr_sw)]
    m_rk = [jnp.where(incl, jnp.where(top, b, a), 0.0) for a, b in zip(mr, mr_sw)]
    vv = [stack2(x) for x in v]
    inv = _unit_lower_inverses([-x for x in m_qp], c)
    qkv = [_bdot(a, b) for a, b in zip(m_qk, vv)]
    sol = [_bdot(x, jnp.concatenate([stack2(a), b], axis=1)) for x, a, b in zip(inv, q_bar, qkv)]
    ws = [unstack(x[:, :LANES]) for x in sol]
    wv = [unstack(x[:, LANES:]) for x in sol]
    y_loc = [unstack(_bdot(a, b)) for a, b in zip(m_rk, vv)]
    lam_end = [x[c - 1:c] for x in lam_in]
    lam_col = [jnp.sum(jnp.where(eye, x, 0.0), axis=1, keepdims=True) for x in lam_end]
    pk_end_t = [(a * b).T for a, b in zip(pk, lam_end)]
    state = [st_ref[pr] for pr in pairs]
    us, y_state = [], []
    for it, (n, pr) in enumerate(items):
        u = _bdot(ws[it], state[pr]) + wv[it]
        y_state.append(_bdot(r_bar[it], state[pr]))
        upd = _bdot(pk_end_t[it], jnp.concatenate([u, v[it]], axis=0))
        state[pr] = lam_col[it] * state[pr] + jnp.where(pair, upd, 0.0)
        us.append(u)
    for pr in pairs:
        st_ref[pr] = state[pr]
    outs = [a + unstack(_bdot(b, stack2(u))) + d for a, b, u, d in zip(y_state, m_rp, us, y_loc)]
    ys = []
    for pr in pairs:
        y = jnp.concatenate([outs[n * len(pairs) + pr] for n in range(nchunks)], axis=0)
        mean = _hdot(y, ones) * (1.0 / RWKV_HEAD)
        yc = y - mean
        var = _hdot(yc * yc, ones) * (1.0 / RWKV_HEAD)
        ys.append(yc * lax.rsqrt(var + RWKV_GN_EPS))
    y = jnp.concatenate(ys, axis=1) * lnw_ref[...] + lnb_ref[...]
    o_ref[...] = (y + b_ref[...]) * g_ref[...]


RWKV_PAIR_GROUP = 4


def rwkv_scan(r, lw, k, v, p, q, gate, bonus, ln_w, ln_b, tt=256):
    s = r.shape[0]
    tt = min(tt, s)
    w = RWKV_PAIR_GROUP * LANES
    spec = pl.BlockSpec((tt, w), lambda h, i: (i, h))
    pspec = pl.BlockSpec((1, w), lambda h, i: (0, h))
    return pl.pallas_call(
        _rwkv_scan_kernel,
        grid=(RWKV_INNER // w, s // tt),
        in_specs=[spec] * 8 + [pspec] * 2,
        out_specs=spec,
        out_shape=jax.ShapeDtypeStruct((s, RWKV_INNER), F32),
        scratch_shapes=[pltpu.VMEM((RWKV_PAIR_GROUP, LANES, LANES), F32)],
        compiler_params=_params("parallel", "arbitrary"),
        name="rwkv_scan",
    )(r, lw, k, v, p, q, gate, bonus, ln_w.reshape(1, -1).astype(F32), ln_b.reshape(1, -1).astype(F32))


def prep_cd(w_in, w_out):
    si, ri = SSD_INNER, RWKV_INNER
    z, xbc, dt, rw = w_in[:, :si], w_in[:, si:2 * si + 512], w_in[:, 2 * si + 512:2 * si + 528], w_in[:, 2 * si + 528:]
    dt_exp = jnp.repeat(dt, SSD_HEADDIM, axis=1)
    cols = [rw[:, :3 * ri], z, dt_exp, xbc, rw[:, 3 * ri:]]
    return _pad_cols(jnp.concatenate(cols, axis=1), CD_PAD).astype(BF16), w_out.astype(BF16)


def mixer_cd(x, ln, w_main, w_out, ssd_conv_w, ssd_conv_b, ssd_dt_bias, ssd_a_log, ssd_d, ssd_norm,
             mu, w0, w2, a0, a2, g2, k_k, k_a, r_k, ln_w, ln_b, tm=512, tt=256):
    proj = norm_matmul(x, w_main, gain=ln, tm=min(2 * tm, x.shape[0]), tn=1024)
    o_c = mamba2_ssd(proj, ssd_conv_w, ssd_conv_b, ssd_dt_bias, ssd_a_log, ssd_d, ssd_norm, tt=tt)
    o_d = rwkv_scan(*rwkv_prep(proj, mu, w0, w2, a0, a2, g2, k_k, k_a, r_k.reshape(-1), tt=tt), ln_w, ln_b, tt=2 * tt)
    return out_proj(o_c, o_d, w_out, x, tm=min(2 * tm, x.shape[0]), tn=1024)


def _rope_tables(s):
    inv = 1.0 / (ROPE_THETA ** (jnp.arange(0, MLA_ROPE, 2, dtype=F32) / MLA_ROPE))
    ang = jnp.arange(s, dtype=F32)[:, None] * inv[None, :]
    cos, sin = jnp.cos(ang), jnp.sin(ang)
    return jnp.concatenate([cos, cos], axis=1), jnp.concatenate([sin, sin], axis=1)


def kernel(x, p, ln_mix, ln_ffn, ab_w_in, mla_q_norm, mla_w_uq, mla_kv_norm, mla_w_ukv, gdn_conv_w, gdn_a_log, gdn_dt_bias, gdn_norm, ab_w_out, cd_w_in, ssd_conv_w, ssd_conv_b, ssd_dt_bias, ssd_a_log, ssd_d, ssd_norm, rwkv_mu, rwkv_w0, rwkv_w2, rwkv_a0, rwkv_a2, rwkv_g2, rwkv_k_k, rwkv_k_a, rwkv_r_k, rwkv_ln_w, rwkv_ln_b, cd_w_out, peer_w_q, peer_keys, peer_u, peer_v, ple_w_proj, ple_norm, ple_w_gate, final_norm):
    assert x.shape[0] == 1
    s = x.shape[1]
    tm = min(512, s)
    tt = min(256, s)
    cos2, sin2 = _rope_tables(s)
    u_all, vt_all = prep_peer_tables(peer_u, peer_v)
    xs = x[0]
    for i in range(DEPTH):
        j = i // 2
        if i % 2 == 0:
            wts = prep_ab(ab_w_in[j], mla_w_uq[j], mla_w_ukv[j], ab_w_out[j])
            xs = mixer_ab(xs, ln_mix[i], *wts, mla_q_norm[j], mla_kv_norm[j], gdn_conv_w[j], gdn_a_log[j],
                          gdn_dt_bias[j], gdn_norm[j], cos2, sin2, tm=tm, t_attn=tm, t_gdn=tm)
        else:
            wts = prep_cd(cd_w_in[j], cd_w_out[j])
            xs = mixer_cd(xs, ln_mix[i], *wts, ssd_conv_w[j], ssd_conv_b[j], ssd_dt_bias[j], ssd_a_log[j], ssd_d[j],
                          ssd_norm[j], rwkv_mu[j], rwkv_w0[j], rwkv_w2[j], rwkv_a0[j], rwkv_a2[j], rwkv_g2[j],
                          rwkv_k_k[j], rwkv_k_a[j], rwkv_r_k[j], rwkv_ln_w[j], rwkv_ln_b[j], tm=tm, tt=tt)
        yt = peer(xs, ln_ffn[i], peer_w_q[i], peer_keys[i], u_all, vt_all, i, tm=tm, tt=tt)
        xs = ple_update(xs, yt, p[i, 0], ple_norm[i], ple_w_gate[i].astype(BF16), ple_w_proj[i].astype(BF16), tm=tm, tn=1024)
    return rmsnorm(xs, final_norm, tm=tm)[None]
```
